```python
import math
import jax, jax.numpy as jnp
from jax import lax
import numpy as np

D_MODEL = 2048
BATCH = 4
SEQ = 2048
DEPTH = 1
DEC_BATCH = 128
DEC_SEQ = 1
PAST_LEN = 16384
PAGE_SIZE = 128

D_MIX = 2 * D_MODEL
D_CONV_GRP = D_MIX // 4
D_SSM = D_MIX - D_CONV_GRP
HEAD_DIM = 64
N_SSM_HEADS = D_SSM // HEAD_DIM
N_SSM_GROUPS = 8
HEADS_PER_GROUP = N_SSM_HEADS // N_SSM_GROUPS
D_STATE = 128
SSM_CONV_W = 4
CONV_MOD_W = 31
CHUNK = 128
D_XBC = D_SSM + 2 * N_SSM_GROUPS * D_STATE
D_IN_PROJ = 2 * D_CONV_GRP + D_SSM + D_XBC + N_SSM_HEADS
D_FF = 5632
PLE_DIM = 256
EPS = 1e-6

kernel_name = "hymba_conformer_ssd_macaron_step"


def rmsnorm(x, g):
    xf = x.astype(jnp.float32)
    y = xf * lax.rsqrt(jnp.mean(xf * xf, axis=-1, keepdims=True) + EPS)
    return (y * g.astype(jnp.float32)).astype(x.dtype)


def layernorm(x, g, b):
    xf = x.astype(jnp.float32)
    mu = jnp.mean(xf, axis=-1, keepdims=True)
    xc = xf - mu
    y = xc * lax.rsqrt(jnp.mean(xc * xc, axis=-1, keepdims=True) + EPS)
    return (y * g.astype(jnp.float32) + b.astype(jnp.float32)).astype(x.dtype)


def swiglu(x, wg, wu, wd):
    return (jax.nn.silu(x @ wg) * (x @ wu)) @ wd


def causal_dwconv(x, buf, w, b):
    k = w.shape[0]
    xp = jnp.concatenate([buf.astype(x.dtype), x], axis=1)
    out = lax.conv_general_dilated(
        xp, w.astype(x.dtype)[:, None, :], window_strides=(1,), padding='VALID',
        dimension_numbers=('NWC', 'WIO', 'NWC'), feature_group_count=x.shape[-1])
    new_buf = xp[:, xp.shape[1] - (k - 1):]
    return out + b.astype(x.dtype), new_buf


def ssd(x, dt, a, bm, cm, d_skip, h0):
    b_, l = x.shape[0], x.shape[1]
    q = min(CHUNK, l)
    pad = (-l) % q
    if pad:
        padw = lambda t: jnp.pad(t, [(0, 0), (0, pad)] + [(0, 0)] * (t.ndim - 2))
        x, dt, bm, cm = padw(x), padw(dt), padw(bm), padw(cm)
    c = (l + pad) // q
    xc = x.reshape(b_, c, q, N_SSM_GROUPS, HEADS_PER_GROUP, HEAD_DIM)
    dtc = dt.reshape(b_, c, q, N_SSM_GROUPS, HEADS_PER_GROUP)
    bc = bm.reshape(b_, c, q, N_SSM_GROUPS, D_STATE)
    cc = cm.reshape(b_, c, q, N_SSM_GROUPS, D_STATE)
    a_cs = jnp.cumsum(dtc * a, axis=2)
    xdt = xc * dtc[..., None]
    idx = jnp.arange(q)
    mask = (idx[:, None] >= idx[None, :])[None, None, :, :, None, None]
    seg = a_cs[:, :, :, None] - a_cs[:, :, None, :]
    decay_ls = jnp.where(mask, jnp.exp(jnp.where(mask, seg, 0.0)), 0.0)
    cb = jnp.einsum('bclgn,bcsgn->bclsg', cc, bc)
    y_diag = jnp.einsum('bclsg,bclsgr,bcsgrp->bclgrp', cb, decay_ls, xdt)
    decay_to_end = jnp.exp(a_cs[:, :, -1:] - a_cs)
    chunk_states = jnp.einsum('bcsgn,bcsgr,bcsgrp->bcgrpn', bc, decay_to_end, xdt)
    chunk_decay = jnp.exp(a_cs[:, :, -1])

    def step(h, inp):
        dec, s = inp
        return h * dec[..., None, None] + s, h

    h_final, h_prev = lax.scan(step, h0, (jnp.moveaxis(chunk_decay, 1, 0),
                                          jnp.moveaxis(chunk_states, 1, 0)))
    h_prev = jnp.moveaxis(h_prev, 0, 1)
    y_off = jnp.einsum('bclgn,bcgrpn,bclgr->bclgrp', cc, h_prev, jnp.exp(a_cs))
    y = y_diag + y_off + d_skip[..., None] * xc
    y = y.reshape(b_, c * q, N_SSM_GROUPS, HEADS_PER_GROUP, HEAD_DIM)[:, :l]
    return y, h_final


def token_mixer(u, conv_buf, xbc_buf, h0, w_in, conv_mod_w, conv_mod_b, conv_mod_ln_g,
                conv_mod_ln_b, ssm_conv_w, ssm_conv_b, dt_bias, a_log, d_skip, ssm_norm_g, w_out):
    f32 = jnp.float32
    b_, l, _ = u.shape
    proj = u @ w_in
    s1 = D_CONV_GRP
    s2 = 2 * D_CONV_GRP
    s3 = s2 + D_SSM
    s4 = s3 + D_XBC
    glu_a, glu_b, z, xbc, dt_raw = jnp.split(proj, [s1, s2, s3, s4], axis=-1)
    v = glu_a * jax.nn.sigmoid(glu_b)
    v, new_conv_buf = causal_dwconv(v, conv_buf, conv_mod_w, conv_mod_b)
    v = jax.nn.silu(layernorm(v, conv_mod_ln_g, conv_mod_ln_b))
    xbc, new_xbc_buf = causal_dwconv(xbc, xbc_buf, ssm_conv_w, ssm_conv_b)
    xbc = jax.nn.silu(xbc)
    xs, bm, cm = jnp.split(xbc, [D_SSM, D_SSM + N_SSM_GROUPS * D_STATE], axis=-1)
    xs = xs.astype(f32).reshape(b_, l, N_SSM_GROUPS, HEADS_PER_GROUP, HEAD_DIM)
    bm = bm.astype(f32).reshape(b_, l, N_SSM_GROUPS, D_STATE)
    cm = cm.astype(f32).reshape(b_, l, N_SSM_GROUPS, D_STATE)
    dt = jax.nn.softplus(dt_raw.astype(f32) + dt_bias.astype(f32)).reshape(
        b_, l, N_SSM_GROUPS, HEADS_PER_GROUP)
    a = -jnp.exp(a_log.astype(f32)).reshape(N_SSM_GROUPS, HEADS_PER_GROUP)
    h0r = h0.astype(f32).reshape(b_, N_SSM_GROUPS, HEADS_PER_GROUP, HEAD_DIM, D_STATE)
    y, h_new = ssd(xs, dt, a, bm, cm,
                   d_skip.astype(f32).reshape(N_SSM_GROUPS, HEADS_PER_GROUP), h0r)
    yg = y.reshape(b_, l, D_SSM) * jax.nn.silu(z.astype(f32))
    yg = yg.reshape(b_, l, N_SSM_GROUPS, D_SSM // N_SSM_GROUPS)
    yg = yg * lax.rsqrt(jnp.mean(yg * yg, axis=-1, keepdims=True) + EPS)
    yg = (yg.reshape(b_, l, D_SSM) * ssm_norm_g.astype(f32)).astype(u.dtype)
    out = jnp.concatenate([v, yg], axis=-1) @ w_out
    h_new = h_new.reshape(b_, N_SSM_HEADS, HEAD_DIM, D_STATE)
    return out, new_conv_buf, new_xbc_buf, h_new


def decoder_layer(h, p, conv_buf, xbc_buf, h0,
                  norm_ffn1_pre, w_ffn1_gate, w_ffn1_up, w_ffn1_down, norm_ffn1_post,
                  norm_mix_pre, w_in, conv_mod_w, conv_mod_b, conv_mod_ln_g, conv_mod_ln_b,
                  ssm_conv_w, ssm_conv_b, dt_bias, a_log, d_skip, ssm_norm_g, w_out, norm_mix_post,
                  norm_ffn2_pre, w_ffn2_gate, w_ffn2_up, w_ffn2_down, norm_ffn2_post,
                  norm_ple_pre, w_ple_gate, w_ple_proj, norm_ple_post):
    h = h + 0.5 * rmsnorm(swiglu(rmsnorm(h, norm_ffn1_pre), w_ffn1_gate, w_ffn1_up, w_ffn1_down),
                          norm_ffn1_post)
    mix, cb, xb, hs = token_mixer(rmsnorm(h, norm_mix_pre), conv_buf, xbc_buf, h0, w_in,
                                  conv_mod_w, conv_mod_b, conv_mod_ln_g, conv_mod_ln_b,
                                  ssm_conv_w, ssm_conv_b, dt_bias, a_log, d_skip, ssm_norm_g, w_out)
    h = h + rmsnorm(mix, norm_mix_post)
    h = h + 0.5 * rmsnorm(swiglu(rmsnorm(h, norm_ffn2_pre), w_ffn2_gate, w_ffn2_up, w_ffn2_down),
                          norm_ffn2_post)
    gate = jax.nn.sigmoid(rmsnorm(h, norm_ple_pre) @ w_ple_gate)
    h = h + rmsnorm(gate * (p.astype(h.dtype) @ w_ple_proj), norm_ple_post)
    return h, cb, xb, hs


def setup_inputs(seed: int = 0) -> dict:
    key = jax.random.key(seed)
    ks = iter(jax.random.split(key, 48))
    f32 = jnp.float32

    def nrm(shape, scale):
        return scale * jax.random.normal(next(ks), shape, f32)

    def gain(shape):
        return 1.0 + 0.05 * jax.random.normal(next(ks), shape, f32)

    d = {}
    d['x_prompt'] = nrm((BATCH, SEQ, D_MODEL), 1.0)
    d['x_sample'] = nrm((DEC_BATCH, DEC_SEQ, D_MODEL), 1.0)
    d['state_conv_mod'] = nrm((DEPTH, DEC_BATCH, CONV_MOD_W - 1, D_CONV_GRP), 0.5)
    d['state_ssm_conv'] = nrm((DEPTH, DEC_BATCH, SSM_CONV_W - 1, D_XBC), 1.0)
    d['state_ssm'] = nrm((DEPTH, DEC_BATCH, N_SSM_HEADS, HEAD_DIM, D_STATE), 0.1)
    d['p_prompt'] = nrm((DEPTH, BATCH, SEQ, PLE_DIM), 1.0)
    d['p_sample'] = nrm((DEPTH, DEC_BATCH, DEC_SEQ, PLE_DIM), 1.0)
    d['norm_ffn1_pre'] = gain((DEPTH, D_MODEL))
    d['w_ffn1_gate'] = nrm((DEPTH, D_MODEL, D_FF), D_MODEL ** -0.5)
    d['w_ffn1_up'] = nrm((DEPTH, D_MODEL, D_FF), D_MODEL ** -0.5)
    d['w_ffn1_down'] = nrm((DEPTH, D_FF, D_MODEL), D_FF ** -0.5)
    d['norm_ffn1_post'] = gain((DEPTH, D_MODEL))
    d['norm_mix_pre'] = gain((DEPTH, D_MODEL))
    d['w_in'] = nrm((DEPTH, D_MODEL, D_IN_PROJ), D_MODEL ** -0.5)
    d['conv_mod_w'] = nrm((DEPTH, CONV_MOD_W, D_CONV_GRP), CONV_MOD_W ** -0.5)
    d['conv_mod_b'] = nrm((DEPTH, D_CONV_GRP), 0.02)
    d['conv_mod_ln_g'] = gain((DEPTH, D_CONV_GRP))
    d['conv_mod_ln_b'] = nrm((DEPTH, D_CONV_GRP), 0.02)
    d['ssm_conv_w'] = nrm((DEPTH, SSM_CONV_W, D_XBC), SSM_CONV_W ** -0.5)
    d['ssm_conv_b'] = nrm((DEPTH, D_XBC), 0.02)
    dt0 = jnp.exp(jax.random.uniform(next(ks), (DEPTH, N_SSM_HEADS), f32,
                                     minval=math.log(1e-3), maxval=math.log(1e-1)))
    d['dt_bias'] = dt0 + jnp.log(-jnp.expm1(-dt0))
    d['a_log'] = jnp.log(jax.random.uniform(next(ks), (DEPTH, N_SSM_HEADS), f32,
                                            minval=1.0, maxval=16.0))
    d['d_skip'] = gain((DEPTH, N_SSM_HEADS))
    d['ssm_norm_g'] = gain((DEPTH, D_SSM))
    d['w_out'] = nrm((DEPTH, D_MIX, D_MODEL), D_MIX ** -0.5)
    d['norm_mix_post'] = gain((DEPTH, D_MODEL))
    d['norm_ffn2_pre'] = gain((DEPTH, D_MODEL))
    d['w_ffn2_gate'] = nrm((DEPTH, D_MODEL, D_FF), D_MODEL ** -0.5)
    d['w_ffn2_up'] = nrm((DEPTH, D_MODEL, D_FF), D_MODEL ** -0.5)
    d['w_ffn2_down'] = nrm((DEPTH, D_FF, D_MODEL), D_FF ** -0.5)
    d['norm_ffn2_post'] = gain((DEPTH, D_MODEL))
    d['norm_ple_pre'] = gain((DEPTH, D_MODEL))
    d['w_ple_gate'] = nrm((DEPTH, D_MODEL, D_MODEL), D_MODEL ** -0.5)
    d['w_ple_proj'] = nrm((DEPTH, PLE_DIM, D_MODEL), PLE_DIM ** -0.5)
    d['norm_ple_post'] = gain((DEPTH, D_MODEL))
    return d


def reference(x_prompt, x_sample, state_conv_mod, state_ssm_conv, state_ssm, p_prompt, p_sample,
              norm_ffn1_pre, w_ffn1_gate, w_ffn1_up, w_ffn1_down, norm_ffn1_post,
              norm_mix_pre, w_in, conv_mod_w, conv_mod_b, conv_mod_ln_g, conv_mod_ln_b,
              ssm_conv_w, ssm_conv_b, dt_bias, a_log, d_skip, ssm_norm_g, w_out, norm_mix_post,
              norm_ffn2_pre, w_ffn2_gate, w_ffn2_up, w_ffn2_down, norm_ffn2_post,
              norm_ple_pre, w_ple_gate, w_ple_proj, norm_ple_post):
    hp, hs = x_prompt, x_sample
    bp = x_prompt.shape[0]
    cp_l, xp_l, sp_l, cs_l, xs_l, ss_l = [], [], [], [], [], []
    for i in range(DEPTH):
        lw = (norm_ffn1_pre[i], w_ffn1_gate[i], w_ffn1_up[i], w_ffn1_down[i], norm_ffn1_post[i],
              norm_mix_pre[i], w_in[i], conv_mod_w[i], conv_mod_b[i], conv_mod_ln_g[i],
              conv_mod_ln_b[i], ssm_conv_w[i], ssm_conv_b[i], dt_bias[i], a_log[i], d_skip[i],
              ssm_norm_g[i], w_out[i], norm_mix_post[i],
              norm_ffn2_pre[i], w_ffn2_gate[i], w_ffn2_up[i], w_ffn2_down[i], norm_ffn2_post[i],
              norm_ple_pre[i], w_ple_gate[i], w_ple_proj[i], norm_ple_post[i])
        zc = jnp.zeros((bp, CONV_MOD_W - 1, D_CONV_GRP), hp.dtype)
        zx = jnp.zeros((bp, SSM_CONV_W - 1, D_XBC), hp.dtype)
        zs = jnp.zeros((bp, N_SSM_HEADS, HEAD_DIM, D_STATE), jnp.float32)
        hp, cp, xp, sp = decoder_layer(hp, p_prompt[i], zc, zx, zs, *lw)
        hs, cs, xs, ss = decoder_layer(hs, p_sample[i], state_conv_mod[i], state_ssm_conv[i],
                                       state_ssm[i], *lw)
        cp_l.append(cp); xp_l.append(xp); sp_l.append(sp.astype(hp.dtype))
        cs_l.append(cs); xs_l.append(xs); ss_l.append(ss.astype(hs.dtype))
    new_conv_mod_prompt = jnp.stack(cp_l, axis=0)
    new_ssm_conv_prompt = jnp.stack(xp_l, axis=0)
    new_ssm_prompt = jnp.stack(sp_l, axis=0)
    new_conv_mod_sample = jnp.stack(cs_l, axis=0)
    new_ssm_conv_sample = jnp.stack(xs_l, axis=0)
    new_ssm_sample = jnp.stack(ss_l, axis=0)
    return (hp, hs, new_conv_mod_prompt, new_ssm_conv_prompt, new_ssm_prompt,
            new_conv_mod_sample, new_ssm_conv_sample, new_ssm_sample)
```

```python
import functools

import jax
import jax.numpy as jnp
from jax import lax
from jax.experimental import pallas as pl
from jax.experimental.pallas import tpu as pltpu

F32 = jnp.float32
BF16 = jnp.bfloat16

D_MODEL = 2048
D_FF = 5632
D_CONV = 1024
D_SSM = 3072
N_HEADS = 48
HEAD_DIM = 64
N_GROUPS = 8
HEADS_PER_GROUP = 6
GROUP_W = HEADS_PER_GROUP * HEAD_DIM
D_STATE = 128
D_XBC = D_SSM + 2 * N_GROUPS * D_STATE
CONV_W = 31
SSM_CONV_W = 4
CHUNK = 128
PLE_DIM = 256
EPS = 1e-6
HEAD_PAD = 128

TM = 640
TF = 512
TN = 512
TL = 256
CARRY = 32
VMEM_LIMIT = 56 * 1024 * 1024


def _params(dims, vmem=VMEM_LIMIT):
    return pltpu.CompilerParams(dimension_semantics=dims, vmem_limit_bytes=vmem)


def _rms(x, g):
    return x * lax.rsqrt(jnp.mean(x * x, axis=-1, keepdims=True) + EPS) * g


def _silu(x):
    return x * jax.nn.sigmoid(x)


def _dot(a, b):
    return jnp.dot(a, b, preferred_element_type=F32)


def _split3(x):
    hi = x.astype(BF16)
    r = x - hi.astype(F32)
    mid = r.astype(BF16)
    lo = (r - mid.astype(F32)).astype(BF16)
    return hi, mid, lo


def _dot3_rhs(a_bf16, x):
    hi, mid, lo = _split3(x)
    return _dot(a_bf16, hi) + _dot(a_bf16, mid) + _dot(a_bf16, lo)


def _dot3_lhs(x, b_bf16):
    hi, mid, lo = _split3(x)
    return _dot(hi, b_bf16) + _dot(mid, b_bf16) + _dot(lo, b_bf16)


def _head_expand_matrix():
    head = lax.broadcasted_iota(jnp.int32, (HEAD_PAD, D_SSM), 0)
    chan = lax.broadcasted_iota(jnp.int32, (HEAD_PAD, D_SSM), 1)
    return jnp.where((chan >> 6) == head, 1.0, 0.0).astype(BF16)


def _ffn_kernel(x_ref, gpre_ref, wg_ref, wu_ref, wd_ref, gpost_ref, *rest, n_next):
    if n_next:
        gnext_ref, o_ref, unext_ref, u_scr = rest
    else:
        o_ref, u_scr = rest
    j = pl.program_id(1)

    @pl.when(j == 0)
    def _():
        u_scr[...] = _rms(x_ref[...], gpre_ref[...]).astype(BF16)

    u = u_scr[...]
    gate = _dot(u, wg_ref[...])
    up = _dot(u, wu_ref[...])
    act = (_silu(gate) * up).astype(BF16)
    part = _dot(act, wd_ref[...])

    @pl.when(j == 0)
    def _():
        o_ref[...] = part

    @pl.when(j > 0)
    def _():
        o_ref[...] += part

    @pl.when(j == pl.num_programs(1) - 1)
    def _():
        h = x_ref[...] + 0.5 * _rms(o_ref[...], gpost_ref[...])
        o_ref[...] = h
        if n_next:
            unext_ref[...] = _rms(h, gnext_ref[...]).astype(BF16)


def _ffn(x, gpre, wg, wu, wd, gpost, gnext=None):
    m = x.shape[0]
    n_next = gnext is not None
    row = pl.BlockSpec((TM, D_MODEL), lambda i, j: (i, 0))
    vec = pl.BlockSpec((1, D_MODEL), lambda i, j: (0, 0))
    in_specs = [row, vec,
                pl.BlockSpec((D_MODEL, TF), lambda i, j: (0, j)),
                pl.BlockSpec((D_MODEL, TF), lambda i, j: (0, j)),
                pl.BlockSpec((TF, D_MODEL), lambda i, j: (j, 0)),
                vec]
    args = [x, gpre, wg, wu, wd, gpost]
    out_shape = [jax.ShapeDtypeStruct((m, D_MODEL), F32)]
    out_specs = [row]
    if n_next:
        in_specs.append(vec)
        args.append(gnext)
        out_shape.append(jax.ShapeDtypeStruct((m, D_MODEL), BF16))
        out_specs.append(row)
    res = pl.pallas_call(
        functools.partial(_ffn_kernel, n_next=n_next),
        grid=(m // TM, D_FF // TF),
        in_specs=in_specs, out_specs=out_specs, out_shape=out_shape,
        scratch_shapes=[pltpu.VMEM((TM, D_MODEL), BF16)],
        compiler_params=_params(("parallel", "arbitrary")),
        name="ffn_next" if n_next else "ffn",
    )(*args)
    return res if n_next else res[0]


def _glu_kernel(u_ref, wa_ref, wb_ref, o_ref):
    u = u_ref[...]
    o_ref[...] = _dot(u, wa_ref[...]) * jax.nn.sigmoid(_dot(u, wb_ref[...]))


def _proj_kernel(u_ref, w_ref, o_ref):
    o_ref[...] = _dot(u_ref[...], w_ref[...])


def _dt_kernel(u_ref, w_ref, b_ref, o_ref):
    x = _dot(u_ref[...], w_ref[...]) + b_ref[...]
    o_ref[...] = jnp.maximum(x, 0.0) + jnp.log1p(jnp.exp(-jnp.abs(x)))


def _in_proj(u, w_in, w_dt, dt_bias):
    m = u.shape[0]
    urow = pl.BlockSpec((TM, D_MODEL), lambda i, j: (i, 0))

    def wcol(off):
        return pl.BlockSpec((D_MODEL, TN), lambda i, j: (0, j + off))

    def out(n):
        return dict(out_specs=pl.BlockSpec((TM, TN), lambda i, j: (i, j)),
                    out_shape=jax.ShapeDtypeStruct((m, n), F32),
                    compiler_params=_params(("parallel", "arbitrary")))

    glu = pl.pallas_call(_glu_kernel, grid=(m // TM, D_CONV // TN),
                         in_specs=[urow, wcol(0), wcol(D_CONV // TN)], name="proj_glu",
                         **out(D_CONV))(u, w_in, w_in)
    z = pl.pallas_call(_proj_kernel, grid=(m // TM, D_SSM // TN),
                       in_specs=[urow, wcol(2 * D_CONV // TN)], name="proj_z",
                       **out(D_SSM))(u, w_in)
    xbc = pl.pallas_call(_proj_kernel, grid=(m // TM, D_XBC // TN),
                         in_specs=[urow, wcol((2 * D_CONV + D_SSM) // TN)], name="proj_xbc",
                         **out(D_XBC))(u, w_in)
    dt = pl.pallas_call(
        _dt_kernel, grid=(m // TM,),
        in_specs=[pl.BlockSpec((TM, D_MODEL), lambda i: (i, 0)),
                  pl.BlockSpec((D_MODEL, HEAD_PAD), lambda i: (0, 0)),
                  pl.BlockSpec((1, HEAD_PAD), lambda i: (0, 0))],
        out_specs=pl.BlockSpec((TM, HEAD_PAD), lambda i: (i, 0)),
        out_shape=jax.ShapeDtypeStruct((m, HEAD_PAD), F32),
        compiler_params=_params(("parallel",)), name="proj_dt")(u, w_dt, dt_bias)
    return glu, z, xbc, dt


def _ln_swish(y, g, b):
    mu = jnp.mean(y, axis=-1, keepdims=True)
    yc = y - mu
    yn = yc * lax.rsqrt(jnp.mean(yc * yc, axis=-1, keepdims=True) + EPS) * g + b
    return _silu(yn)


def _pconv_kernel(v_ref, w_ref, b_ref, lg_ref, lb_ref, o_ref, xpad_scr, conv_scr):
    t = pl.program_id(1)

    @pl.when(t == 0)
    def _():
        xpad_scr[0:CARRY, :] = jnp.zeros((CARRY, D_CONV), F32)

    xpad_scr[CARRY:CARRY + TL, :] = v_ref[...]
    first = CARRY - (CONV_W - 1)
    for cb in range(D_CONV // 128):
        lanes = slice(cb * 128, (cb + 1) * 128)
        acc = jnp.broadcast_to(b_ref[:, lanes], (TL, 128))
        for k in range(CONV_W):
            acc = acc + w_ref[k:k + 1, lanes] * xpad_scr[first + k:first + k + TL, lanes]
        conv_scr[:, lanes] = acc
    xpad_scr[0:CARRY, :] = xpad_scr[TL:TL + CARRY, :]
    o_ref[...] = _ln_swish(conv_scr[...], lg_ref[...], lb_ref[...]).astype(BF16)


def _prompt_conv(v, n_batch, seq, w, b, lg, lb):
    vec = pl.BlockSpec((1, D_CONV), lambda bi, t: (0, 0))
    steps = seq // TL
    return pl.pallas_call(
        _pconv_kernel, grid=(n_batch, steps),
        in_specs=[pl.BlockSpec((TL, D_CONV), lambda bi, t: (bi * steps + t, 0)),
                  pl.BlockSpec((CONV_W, D_CONV), lambda bi, t: (0, 0)), vec, vec, vec],
        out_specs=pl.BlockSpec((TL, D_CONV), lambda bi, t: (bi * steps + t, 0)),
        out_shape=jax.ShapeDtypeStruct((n_batch * seq, D_CONV), BF16),
        scratch_shapes=[pltpu.VMEM((CARRY + TL, D_CONV), F32), pltpu.VMEM((TL, D_CONV), F32)],
        compiler_params=_params(("arbitrary", "arbitrary")), name="prompt_conv")(v, w, b, lg, lb)


def _gated_norm(y, z, g):
    yg = y * _silu(z)
    return yg * lax.rsqrt(jnp.mean(yg * yg, axis=-1, keepdims=True) + EPS) * g


def _pssd_kernel(xbc_ref, z_ref, dt_ref, cw_ref, cb_ref, a_ref, aexp_ref, dexp_ref, ng_ref,
                 y_ref, hfin_ref, state_scr, xpad_scr, xc_scr):
    c = pl.program_id(1)
    q = CHUNK

    @pl.when(c == 0)
    def _():
        state_scr[...] = jnp.zeros_like(state_scr)
        xpad_scr[0:8, :] = jnp.zeros((8, D_XBC), F32)

    xpad_scr[8:8 + q, :] = xbc_ref[...]
    for cb in range(D_XBC // 512):
        lanes = slice(cb * 512, (cb + 1) * 512)
        acc = jnp.broadcast_to(cb_ref[:, lanes], (q, 512))
        for j in range(SSM_CONV_W):
            k = SSM_CONV_W - 1 - j
            acc = acc + cw_ref[k:k + 1, lanes] * xpad_scr[8 - j:8 - j + q, lanes]
        xc_scr[:, lanes] = _silu(acc)
    xpad_scr[0:8, :] = xpad_scr[q:q + 8, :]

    row_i = lax.broadcasted_iota(jnp.int32, (q, q), 0)
    col_i = lax.broadcasted_iota(jnp.int32, (q, q), 1)
    tril = row_i >= col_i
    tri = jnp.where(tril, 1.0, 0.0).astype(BF16)
    expand = _head_expand_matrix()

    dt = dt_ref[...]
    a_cs = _dot3_rhs(tri, dt * a_ref[...])
    a_cs_t = a_cs.T
    dt_exp = _dot3_lhs(dt, expand)
    acs_exp = _dot3_rhs(tri, dt_exp * aexp_ref[...])
    last = acs_exp[q - 1:q, :]
    lane_lo = lax.broadcasted_iota(jnp.int32, (q, 128), 1) < HEAD_DIM

    for g in range(N_GROUPS):
        ch = slice(g * GROUP_W, (g + 1) * GROUP_W)
        xs = xc_scr[:, ch]
        bg = xc_scr[:, D_SSM + g * D_STATE:D_SSM + (g + 1) * D_STATE]
        cg = xc_scr[:, D_SSM + (N_GROUPS + g) * D_STATE:D_SSM + (N_GROUPS + g + 1) * D_STATE]
        bg16 = bg.astype(BF16)
        cg16 = cg.astype(BF16)
        xdt = xs * dt_exp[:, ch]
        acs_g = acs_exp[:, ch]
        cb = lax.dot_general(cg16, bg16, (((1,), (1,)), ((), ())), preferred_element_type=F32)
        st = state_scr[:, ch]
        y = _dot(cg16, st.astype(BF16)) * jnp.exp(acs_g)
        pieces = []
        for pr in range(HEADS_PER_GROUP // 2):
            xpair = xdt[:, pr * 128:(pr + 1) * 128]
            acc = None
            for half in range(2):
                h = g * HEADS_PER_GROUP + 2 * pr + half
                seg = a_cs[:, h:h + 1] - a_cs_t[h:h + 1, :]
                decay = jnp.where(tril, jnp.exp(jnp.where(tril, seg, 0.0)), 0.0)
                m = (cb * decay).astype(BF16)
                keep = lane_lo if half == 0 else jnp.logical_not(lane_lo)
                term = _dot(m, jnp.where(keep, xpair, 0.0).astype(BF16))
                acc = term if acc is None else acc + term
            pieces.append(acc)
        y = y + jnp.concatenate(pieces, axis=1) + dexp_ref[:, ch] * xs
        xdec = (xdt * jnp.exp(last[:, ch] - acs_g)).astype(BF16)
        s_new = lax.dot_general(bg16, xdec, (((0,), (0,)), ((), ())), preferred_element_type=F32)
        state_scr[:, ch] = st * jnp.exp(last[:, ch]) + s_new
        y_ref[:, ch] = _gated_norm(y, z_ref[:, ch], ng_ref[:, ch]).astype(BF16)

    @pl.when(c == pl.num_programs(1) - 1)
    def _():
        hfin_ref[0] = state_scr[...].T


def _prompt_ssd(xbc, z, dt, n_batch, seq, cw, cb, a_pad, a_exp, d_exp, ng):
    nc = seq // CHUNK

    def rows(w):
        return pl.BlockSpec((CHUNK, w), lambda bi, c: (bi * nc + c, 0))

    def vec(w, r=1):
        return pl.BlockSpec((r, w), lambda bi, c: (0, 0))

    return pl.pallas_call(
        _pssd_kernel, grid=(n_batch, nc),
        in_specs=[rows(D_XBC), rows(D_SSM), rows(HEAD_PAD), vec(D_XBC, SSM_CONV_W), vec(D_XBC),
                  vec(HEAD_PAD), vec(D_SSM), vec(D_SSM), vec(D_SSM)],
        out_specs=[rows(D_SSM), pl.BlockSpec((1, D_SSM, D_STATE), lambda bi, c: (bi, 0, 0))],
        out_shape=[jax.ShapeDtypeStruct((n_batch * seq, D_SSM), BF16),
                   jax.ShapeDtypeStruct((n_batch, D_SSM, D_STATE), F32)],
        scratch_shapes=[pltpu.VMEM((D_STATE, D_SSM), F32), pltpu.VMEM((8 + CHUNK, D_XBC), F32),
                        pltpu.VMEM((CHUNK, D_XBC), F32)],
        compiler_params=_params(("arbitrary", "arbitrary")), name="prompt_ssd",
    )(xbc, z, dt, cw, cb, a_pad, a_exp, d_exp, ng)


def _sprep_kernel(v_ref, cst_ref, w_ref, b_ref, lg_ref, lb_ref,
                  xbc_ref, xst_ref, cw_ref, cb_ref, dt_ref, a_ref, dexp_ref,
                  co_ref, ncst_ref, nxst_ref, xdt_ref, bc_ref, dec_ref, skip_ref):
    nb = v_ref.shape[0]
    v = v_ref[...]
    acc = jnp.broadcast_to(b_ref[...], (nb, D_CONV)) + w_ref[CONV_W - 1:CONV_W, :] * v
    for k in range(CONV_W - 1):
        acc = acc + w_ref[k:k + 1, :] * cst_ref[k]
    co_ref[...] = _ln_swish(acc, lg_ref[...], lb_ref[...]).astype(BF16)
    for k in range(CONV_W - 2):
        ncst_ref[k] = cst_ref[k + 1]
    ncst_ref[CONV_W - 2] = v

    xn = xbc_ref[...]
    acc = jnp.broadcast_to(cb_ref[...], (nb, D_XBC)) + cw_ref[SSM_CONV_W - 1:SSM_CONV_W, :] * xn
    for k in range(SSM_CONV_W - 1):
        acc = acc + cw_ref[k:k + 1, :] * xst_ref[k]
    xc = _silu(acc)
    for k in range(SSM_CONV_W - 2):
        nxst_ref[k] = xst_ref[k + 1]
    nxst_ref[SSM_CONV_W - 2] = xn

    xs = xc[:, :D_SSM]
    bc_ref[...] = xc[:, D_SSM:]
    dt = dt_ref[...]
    dec_ref[...] = jnp.exp(dt * a_ref[...])
    dt_exp = _dot3_lhs(dt, _head_expand_matrix())
    xdt_ref[...] = xs * dt_exp
    skip_ref[...] = dexp_ref[...] * xs


def _sssd_kernel(dec_ref, h0_ref, xdt_ref, bc_ref, skip_ref, z_ref, ng_ref,
                 hn_ref, y_ref, yt_scr, xdt_t_scr):
    b = pl.program_id(0)
    nb = pl.num_programs(0)

    @pl.when(b == 0)
    def _():
        yt_scr[...] = jnp.zeros_like(yt_scr)
        xdt_t_scr[...] = xdt_ref[...].T.astype(BF16)

    brow = bc_ref[pl.ds(b, 1), :]
    is_b = lax.broadcasted_iota(jnp.int32, (nb, D_STATE), 0) == b
    for g in range(N_GROUPS):
        rows = slice(g * GROUP_W, (g + 1) * GROUP_W)
        b_sel = jnp.where(is_b, brow[:, g * D_STATE:(g + 1) * D_STATE], 0.0).astype(BF16)
        c_sel = jnp.where(is_b, brow[:, (N_GROUPS + g) * D_STATE:(N_GROUPS + g + 1) * D_STATE],
                          0.0).astype(BF16)
        s_new = _dot(xdt_t_scr[rows, :], b_sel)
        parts = []
        for r in range(HEADS_PER_GROUP):
            h = g * HEADS_PER_GROUP + r
            hr = slice(h * HEAD_DIM, (h + 1) * HEAD_DIM)
            parts.append(h0_ref[0, hr, :] * dec_ref[b * N_HEADS + h]
                         + s_new[r * HEAD_DIM:(r + 1) * HEAD_DIM, :])
        h_new = jnp.concatenate(parts, axis=0)
        hn_ref[0, rows, :] = h_new
        yt_scr[rows, :] += lax.dot_general(h_new.astype(BF16), c_sel, (((1,), (1,)), ((), ())),
                                           preferred_element_type=F32)

    @pl.when(b == nb - 1)
    def _():
        y = yt_scr[...].T + skip_ref[...]
        for g in range(N_GROUPS):
            ch = slice(g * GROUP_W, (g + 1) * GROUP_W)
            y_ref[:, ch] = _gated_norm(y[:, ch], z_ref[:, ch], ng_ref[:, ch]).astype(BF16)


def _outproj_kernel(c_ref, y_ref, w_ref, h_ref, g_ref, o_ref):
    k = pl.program_id(1)

    @pl.when(k == 0)
    def _():
        o_ref[...] = _dot(c_ref[...], w_ref[...])

    @pl.when(k > 0)
    def _():
        o_ref[...] += _dot(y_ref[...], w_ref[...])

    @pl.when(k == pl.num_programs(1) - 1)
    def _():
        o_ref[...] = h_ref[...] + _rms(o_ref[...], g_ref[...])


def _out_proj(cmix, ymix, w_out, h, g):
    m = h.shape[0]
    kt = D_CONV
    nk = (D_CONV + D_SSM) // kt
    row = pl.BlockSpec((TM, D_MODEL), lambda i, k: (i, 0))
    return pl.pallas_call(
        _outproj_kernel, grid=(m // TM, nk),
        in_specs=[pl.BlockSpec((TM, kt), lambda i, k: (i, 0)),
                  pl.BlockSpec((TM, kt), lambda i, k: (i, jnp.maximum(k - 1, 0))),
                  pl.BlockSpec((kt, D_MODEL), lambda i, k: (k, 0)),
                  row, pl.BlockSpec((1, D_MODEL), lambda i, k: (0, 0))],
        out_specs=row, out_shape=jax.ShapeDtypeStruct((m, D_MODEL), F32),
        compiler_params=_params(("parallel", "arbitrary")), name="out_proj")(cmix, ymix, w_out, h, g)


def _ple_kernel(h_ref, p_ref, gpre_ref, wg_ref, wp_ref, gpost_ref, o_ref):
    h = h_ref[...]
    gate = jax.nn.sigmoid(_dot(_rms(h, gpre_ref[...]).astype(BF16), wg_ref[...]))
    emb = _dot(p_ref[...].astype(BF16), wp_ref[...])
    o_ref[...] = h + _rms(gate * emb, gpost_ref[...])


def _ple(h, p, gpre, wg, wp, gpost):
    m = h.shape[0]
    row = pl.BlockSpec((TM, D_MODEL), lambda i: (i, 0))
    vec = pl.BlockSpec((1, D_MODEL), lambda i: (0, 0))
    return pl.pallas_call(
        _ple_kernel, grid=(m // TM,),
        in_specs=[row, pl.BlockSpec((TM, PLE_DIM), lambda i: (i, 0)), vec,
                  pl.BlockSpec((D_MODEL, D_MODEL), lambda i: (0, 0)),
                  pl.BlockSpec((PLE_DIM, D_MODEL), lambda i: (0, 0)), vec],
        out_specs=row, out_shape=jax.ShapeDtypeStruct((m, D_MODEL), F32),
        compiler_params=_params(("parallel",)), name="ple")(h, p, gpre, wg, wp, gpost)


def _layer(x, p, n_batch, seq, cst, xst, h0, lw):
    (norm_ffn1_pre, w_ffn1_gate, w_ffn1_up, w_ffn1_down, norm_ffn1_post,
     norm_mix_pre, w_in, conv_mod_w, conv_mod_b, conv_mod_ln_g, conv_mod_ln_b,
     ssm_conv_w, ssm_conv_b, dt_bias, a_log, d_skip, ssm_norm_g, w_out, norm_mix_post,
     norm_ffn2_pre, w_ffn2_gate, w_ffn2_up, w_ffn2_down, norm_ffn2_post,
     norm_ple_pre, w_ple_gate, w_ple_proj, norm_ple_post) = lw
    m = x.shape[0]
    n_prompt = n_batch * seq
    n_samp = m - n_prompt
    row2 = lambda t: t.reshape(1, -1)
    bf = lambda t: t.astype(BF16)

    h1, u = _ffn(x, row2(norm_ffn1_pre), bf(w_ffn1_gate), bf(w_ffn1_up), bf(w_ffn1_down),
                 row2(norm_ffn1_post), row2(norm_mix_pre))

    d_proj = w_in.shape[1]
    w_dt = jnp.pad(bf(w_in[:, d_proj - N_HEADS:]), ((0, 0), (0, HEAD_PAD - N_HEADS)))
    pad_h = lambda t: jnp.pad(t.astype(F32), (0, HEAD_PAD - N_HEADS)).reshape(1, HEAD_PAD)
    glu, z, xbc, dt = _in_proj(u, bf(w_in), w_dt, pad_h(dt_bias))

    a = -jnp.exp(a_log.astype(F32))
    a_pad = pad_h(a)
    a_exp = row2(jnp.repeat(a, HEAD_DIM))
    d_exp = row2(jnp.repeat(d_skip.astype(F32), HEAD_DIM))
    cw, cb = ssm_conv_w, row2(ssm_conv_b)
    mw, mb, lg, lb = conv_mod_w, row2(conv_mod_b), row2(conv_mod_ln_g), row2(conv_mod_ln_b)
    ng = row2(ssm_norm_g)

    c_prompt = _prompt_conv(glu, n_batch, seq, mw, mb, lg, lb)
    y_prompt, hfin_prompt = _prompt_ssd(xbc, z, dt, n_batch, seq, cw, cb, a_pad, a_exp, d_exp, ng)

    sb = 32
    off = n_prompt // sb
    srow = lambda w: pl.BlockSpec((sb, w), lambda i: (off + i, 0))
    orow = lambda w: pl.BlockSpec((sb, w), lambda i: (i, 0))
    taps = lambda k, w: pl.BlockSpec((k, sb, w), lambda i: (0, i, 0))
    const = lambda *s: pl.BlockSpec(s, lambda i: (0,) * len(s))
    cst_t = jnp.swapaxes(cst, 0, 1)
    xst_t = jnp.swapaxes(xst, 0, 1)
    c_samp, ncst_t, nxst_t, xdt, bc, dec, skip = pl.pallas_call(
        _sprep_kernel, grid=(n_samp // sb,),
        in_specs=[srow(D_CONV), taps(CONV_W - 1, D_CONV), const(CONV_W, D_CONV), const(1, D_CONV),
                  const(1, D_CONV), const(1, D_CONV),
                  srow(D_XBC), taps(SSM_CONV_W - 1, D_XBC), const(SSM_CONV_W, D_XBC),
                  const(1, D_XBC), srow(HEAD_PAD), const(1, HEAD_PAD), const(1, D_SSM)],
        out_specs=[orow(D_CONV), taps(CONV_W - 1, D_CONV), taps(SSM_CONV_W - 1, D_XBC),
                   orow(D_SSM), orow(D_XBC - D_SSM), orow(HEAD_PAD), orow(D_SSM)],
        out_shape=[jax.ShapeDtypeStruct((n_samp, D_CONV), BF16),
                   jax.ShapeDtypeStruct((CONV_W - 1, n_samp, D_CONV), F32),
                   jax.ShapeDtypeStruct((SSM_CONV_W - 1, n_samp, D_XBC), F32),
                   jax.ShapeDtypeStruct((n_samp, D_SSM), F32),
                   jax.ShapeDtypeStruct((n_samp, D_XBC - D_SSM), F32),
                   jax.ShapeDtypeStruct((n_samp, HEAD_PAD), F32),
                   jax.ShapeDtypeStruct((n_samp, D_SSM), F32)],
        compiler_params=_params(("parallel",)), name="sample_prep",
    )(glu, cst_t, mw, mb, lg, lb, xbc, xst_t, cw, cb, dt, a_pad, d_exp)

    dec_flat = dec[:, :N_HEADS].reshape(-1)
    hn, y_samp = pl.pallas_call(
        _sssd_kernel, grid=(n_samp,),
        in_specs=[pl.BlockSpec(memory_space=pltpu.SMEM),
                  pl.BlockSpec((1, D_SSM, D_STATE), lambda b: (b, 0, 0)),
                  const(n_samp, D_SSM), const(n_samp, D_XBC - D_SSM), const(n_samp, D_SSM),
                  pl.BlockSpec((n_samp, D_SSM), lambda b: (n_prompt // n_samp, 0)), const(1, D_SSM)],
        out_specs=[pl.BlockSpec((1, D_SSM, D_STATE), lambda b: (b, 0, 0)),
                   const(n_samp, D_SSM)],
        out_shape=[jax.ShapeDtypeStruct((n_samp, D_SSM, D_STATE), F32),
                   jax.ShapeDtypeStruct((n_samp, D_SSM), BF16)],
        scratch_shapes=[pltpu.VMEM((D_SSM, n_samp), F32), pltpu.VMEM((D_SSM, n_samp), BF16)],
        compiler_params=_params(("arbitrary",)), name="sample_ssd",
    )(dec_flat, h0.reshape(n_samp, D_SSM, D_STATE), xdt, bc, skip, z, ng)

    cmix = jnp.concatenate([c_prompt, c_samp], axis=0)
    ymix = jnp.concatenate([y_prompt, y_samp], axis=0)
    h2 = _out_proj(cmix, ymix, bf(w_out), h1, row2(norm_mix_post))
    h3 = _ffn(h2, row2(norm_ffn2_pre), bf(w_ffn2_gate), bf(w_ffn2_up), bf(w_ffn2_down),
              row2(norm_ffn2_post))
    h4 = _ple(h3, p, row2(norm_ple_pre), bf(w_ple_gate), bf(w_ple_proj), row2(norm_ple_post))

    new_cst_prompt = glu[:n_prompt].reshape(n_batch, seq, D_CONV)[:, seq - (CONV_W - 1):]
    new_xst_prompt = xbc[:n_prompt].reshape(n_batch, seq, D_XBC)[:, seq - (SSM_CONV_W - 1):]
    new_h_prompt = hfin_prompt.reshape(n_batch, N_HEADS, HEAD_DIM, D_STATE)
    new_cst_samp = jnp.swapaxes(ncst_t, 0, 1)
    new_xst_samp = jnp.swapaxes(nxst_t, 0, 1)
    new_h_samp = hn.reshape(n_samp, N_HEADS, HEAD_DIM, D_STATE)
    return (h4, new_cst_prompt, new_xst_prompt, new_h_prompt, new_cst_samp, new_xst_samp, new_h_samp)


def kernel(x_prompt, x_sample, state_conv_mod, state_ssm_conv, state_ssm, p_prompt, p_sample,
           norm_ffn1_pre, w_ffn1_gate, w_ffn1_up, w_ffn1_down, norm_ffn1_post,
           norm_mix_pre, w_in, conv_mod_w, conv_mod_b, conv_mod_ln_g, conv_mod_ln_b,
           ssm_conv_w, ssm_conv_b, dt_bias, a_log, d_skip, ssm_norm_g, w_out, norm_mix_post,
           norm_ffn2_pre, w_ffn2_gate, w_ffn2_up, w_ffn2_down, norm_ffn2_post,
           norm_ple_pre, w_ple_gate, w_ple_proj, norm_ple_post):
    weights = (norm_ffn1_pre, w_ffn1_gate, w_ffn1_up, w_ffn1_down, norm_ffn1_post,
               norm_mix_pre, w_in, conv_mod_w, conv_mod_b, conv_mod_ln_g, conv_mod_ln_b,
               ssm_conv_w, ssm_conv_b, dt_bias, a_log, d_skip, ssm_norm_g, w_out, norm_mix_post,
               norm_ffn2_pre, w_ffn2_gate, w_ffn2_up, w_ffn2_down, norm_ffn2_post,
               norm_ple_pre, w_ple_gate, w_ple_proj, norm_ple_post)
    n_batch, seq, _ = x_prompt.shape
    n_samp = x_sample.shape[0]
    n_prompt = n_batch * seq
    depth = norm_ffn1_pre.shape[0]
    x = jnp.concatenate([x_prompt.reshape(n_prompt, D_MODEL), x_sample.reshape(n_samp, D_MODEL)], axis=0)
    outs = [[] for _ in range(6)]
    for i in range(depth):
        p = jnp.concatenate([p_prompt[i].reshape(n_prompt, PLE_DIM),
                             p_sample[i].reshape(n_samp, PLE_DIM)], axis=0)
        res = _layer(x, p, n_batch, seq, state_conv_mod[i], state_ssm_conv[i], state_ssm[i],
                     tuple(w[i] for w in weights))
        x = res[0]
        for lst, r in zip(outs, res[1:]):
            lst.append(r)
    y_prompt = x[:n_prompt].reshape(n_batch, seq, D_MODEL)
    y_sample = x[n_prompt:].reshape(n_samp, 1, D_MODEL)
    return (y_prompt, y_sample) + tuple(jnp.stack(lst, axis=0) for lst in outs)
```

```python
import functools

import jax
import jax.numpy as jnp
from jax import lax
from jax.experimental import pallas as pl
from jax.experimental.pallas import tpu as pltpu

F32 = jnp.float32
BF16 = jnp.bfloat16

D_MODEL = 2048
D_FF = 5632
D_CONV = 1024
D_SSM = 3072
N_HEADS = 48
HEAD_DIM = 64
N_GROUPS = 8
HEADS_PER_GROUP = 6
GROUP_W = HEADS_PER_GROUP * HEAD_DIM
D_STATE = 128
D_XBC = D_SSM + 2 * N_GROUPS * D_STATE
CONV_W = 31
SSM_CONV_W = 4
CHUNK = 128
PLE_DIM = 256
EPS = 1e-6
HEAD_PAD = 128

TM = 640
TF = 512
TN = 1024
TL = 256
CARRY = 32
XCARRY = 8
SAMPLE_BLOCK = 32
VMEM_LIMIT = 56 * 1024 * 1024


def _params(dims, vmem=VMEM_LIMIT):
    return pltpu.CompilerParams(dimension_semantics=dims, vmem_limit_bytes=vmem)


def _rms(x, g):
    return x * lax.rsqrt(jnp.mean(x * x, axis=-1, keepdims=True) + EPS) * g


def _silu(x):
    return x * jax.nn.sigmoid(x)


def _dot(a, b):
    return jnp.dot(a, b, preferred_element_type=F32)


def _split3(x):
    hi = x.astype(BF16)
    r = x - hi.astype(F32)
    mid = r.astype(BF16)
    lo = (r - mid.astype(F32)).astype(BF16)
    return hi, mid, lo


def _dot3_rhs(a_bf16, x):
    hi, mid, lo = _split3(x)
    return _dot(a_bf16, hi) + _dot(a_bf16, mid) + _dot(a_bf16, lo)


def _dot3_lhs(x, b_bf16):
    hi, mid, lo = _split3(x)
    return _dot(hi, b_bf16) + _dot(mid, b_bf16) + _dot(lo, b_bf16)


def _head_expand_matrix():
    head = lax.broadcasted_iota(jnp.int32, (HEAD_PAD, D_SSM), 0)
    chan = lax.broadcasted_iota(jnp.int32, (HEAD_PAD, D_SSM), 1)
    return jnp.where((chan >> 6) == head, 1.0, 0.0).astype(BF16)


def _on_tile_rows(i, a_ref, b_ref, fn):
    if b_ref is None:
        fn(slice(0, TM), a_ref, slice(0, TM))
        return
    nb = b_ref.shape[0]
    na = TM - nb
    last = pl.num_programs(0) - 1

    @pl.when(i != last)
    def _():
        fn(slice(0, TM), a_ref, slice(0, TM))

    @pl.when(i == last)
    def _():
        fn(slice(0, na), a_ref, slice(0, na))
        fn(slice(na, TM), b_ref, slice(0, nb))


def _check_split(n_prompt, n_samp):
    assert (n_prompt + n_samp) % TM == 0 and n_samp < TM and n_samp % 16 == 0


def _ffn_kernel(*refs, split, n_next, n_cast):
    refs = list(refs)
    xa_ref = refs.pop(0)
    xb_ref = refs.pop(0) if split else None
    gpre_ref, wg_ref, wu_ref, wd_ref, gpost_ref = refs[:5]
    refs = refs[5:]
    gnext_ref = refs.pop(0) if n_next else None
    cast_in, refs = refs[:n_cast], refs[n_cast:]
    o_ref = refs.pop(0)
    unext_ref = refs.pop(0) if n_next else None
    cast_out, refs = refs[:n_cast], refs[n_cast:]
    (u_scr,) = refs
    i = pl.program_id(0)
    j = pl.program_id(1)

    @pl.when(j == 0)
    def _():
        def pre(rows, src, srows):
            u_scr[rows, :] = _rms(src[srows, :], gpre_ref[...]).astype(BF16)
        _on_tile_rows(i, xa_ref, xb_ref, pre)
        o_ref[...] = jnp.zeros_like(o_ref)

    for ci, co in zip(cast_in, cast_out):
        co[...] = ci[...].astype(BF16)

    u = u_scr[...]
    act = (_silu(_dot(u, wg_ref[...])) * _dot(u, wu_ref[...])).astype(BF16)
    o_ref[...] += _dot(act, wd_ref[...])

    @pl.when(j == pl.num_programs(1) - 1)
    def _():
        def post(rows, src, srows):
            h = src[srows, :] + 0.5 * _rms(o_ref[rows, :], gpost_ref[...])
            o_ref[rows, :] = h
            if n_next:
                unext_ref[rows, :] = _rms(h, gnext_ref[...]).astype(BF16)
        _on_tile_rows(i, xa_ref, xb_ref, post)


def _ffn(x, gpre, wg, wu, wd, gpost, gnext=None, x_tail=None, casts=()):
    m = x.shape[0] + (0 if x_tail is None else x_tail.shape[0])
    grid = (m // TM, D_FF // TF)
    steps = grid[0] * grid[1]
    row = pl.BlockSpec((TM, D_MODEL), lambda i, j: (i, 0))
    vec = pl.BlockSpec((1, D_MODEL), lambda i, j: (0, 0))
    in_specs, args = [row], [x]
    if x_tail is not None:
        in_specs.append(pl.BlockSpec(x_tail.shape, lambda i, j: (0, 0)))
        args.append(x_tail)
    in_specs += [vec,
                 pl.BlockSpec((D_MODEL, TF), lambda i, j: (0, j)),
                 pl.BlockSpec((D_MODEL, TF), lambda i, j: (0, j)),
                 pl.BlockSpec((TF, D_MODEL), lambda i, j: (j, 0)),
                 vec]
    args += [gpre, wg, wu, wd, gpost]
    out_shape = [jax.ShapeDtypeStruct((m, D_MODEL), F32)]
    out_specs = [row]
    if gnext is not None:
        in_specs.append(vec)
        args.append(gnext)
        out_shape.append(jax.ShapeDtypeStruct((m, D_MODEL), BF16))
        out_specs.append(row)
    for w, r in casts:
        nblk = w.shape[0] // r
        assert w.shape[0] % r == 0 and r % 16 == 0 and nblk <= steps
        spec = pl.BlockSpec((r, w.shape[1]),
                            lambda i, j, nblk=nblk: (jnp.minimum(i * grid[1] + j, nblk - 1), 0))
        in_specs.append(spec)
        args.append(w)
        out_specs.append(spec)
        out_shape.append(jax.ShapeDtypeStruct(w.shape, BF16))
    return pl.pallas_call(
        functools.partial(_ffn_kernel, split=x_tail is not None, n_next=gnext is not None,
                          n_cast=len(casts)),
        grid=grid, in_specs=in_specs, out_specs=out_specs, out_shape=out_shape,
        scratch_shapes=[pltpu.VMEM((TM, D_MODEL), BF16)],
        compiler_params=_params(("arbitrary", "arbitrary")),
        name="ffn_first" if casts else "ffn",
    )(*args)


N_Z_STEPS = D_SSM // TN
N_XBC_STEPS = D_XBC // TN


def _inproj_kernel(u_ref, w_ref, wb_ref, wdt_ref, dtb_ref, glu_ref, z_ref, xbc_ref, dt_ref):
    j = pl.program_id(1)
    u = u_ref[...]

    @pl.when(j == 0)
    def _():
        glu_ref[...] = _dot(u, w_ref[...]) * jax.nn.sigmoid(_dot(u, wb_ref[...]))

    @pl.when(jnp.logical_and(j >= 1, j <= N_Z_STEPS))
    def _():
        z_ref[...] = _dot(u, w_ref[...])

    @pl.when(j > N_Z_STEPS)
    def _():
        xbc_ref[...] = _dot(u, w_ref[...])

    @pl.when(j == pl.num_programs(1) - 1)
    def _():
        x = _dot(u, wdt_ref[...]) + dtb_ref[...]
        dt_ref[...] = jnp.maximum(x, 0.0) + jnp.log1p(jnp.exp(-jnp.abs(x)))


def _in_proj(u, w_in, w_dt, dt_bias):
    m = u.shape[0]
    assert D_CONV == TN
    nj = 1 + N_Z_STEPS + N_XBC_STEPS
    return pl.pallas_call(
        _inproj_kernel, grid=(m // TM, nj),
        in_specs=[pl.BlockSpec((TM, D_MODEL), lambda i, j: (i, 0)),
                  pl.BlockSpec((D_MODEL, TN), lambda i, j: (0, jnp.where(j == 0, 0, j + 1))),
                  pl.BlockSpec((D_MODEL, TN), lambda i, j: (0, 1)),
                  pl.BlockSpec((D_MODEL, HEAD_PAD), lambda i, j: (0, 0)),
                  pl.BlockSpec((1, HEAD_PAD), lambda i, j: (0, 0))],
        out_specs=[pl.BlockSpec((TM, TN), lambda i, j: (i, 0)),
                   pl.BlockSpec((TM, TN), lambda i, j: (i, jnp.clip(j - 1, 0, N_Z_STEPS - 1))),
                   pl.BlockSpec((TM, TN), lambda i, j: (i, jnp.clip(j - 1 - N_Z_STEPS, 0, N_XBC_STEPS - 1))),
                   pl.BlockSpec((TM, HEAD_PAD), lambda i, j: (i, 0))],
        out_shape=[jax.ShapeDtypeStruct((m, D_CONV), F32), jax.ShapeDtypeStruct((m, D_SSM), F32),
                   jax.ShapeDtypeStruct((m, D_XBC), F32), jax.ShapeDtypeStruct((m, HEAD_PAD), F32)],
        compiler_params=_params(("arbitrary", "arbitrary")), name="in_proj",
    )(u, w_in, w_in, w_dt, dt_bias)


def _ln_swish(y, g, b):
    mu = jnp.mean(y, axis=-1, keepdims=True)
    yc = y - mu
    yn = yc * lax.rsqrt(jnp.mean(yc * yc, axis=-1, keepdims=True) + EPS) * g + b
    return _silu(yn)


def _pconv_kernel(v_ref, w_ref, b_ref, lg_ref, lb_ref, o_ref, tail_ref, xpad_scr, conv_scr):
    t = pl.program_id(1)

    @pl.when(t == 0)
    def _():
        xpad_scr[0:CARRY, :] = jnp.zeros((CARRY, D_CONV), F32)

    xpad_scr[CARRY:CARRY + TL, :] = v_ref[...]
    first = CARRY - (CONV_W - 1)
    for cb in range(D_CONV // 128):
        lanes = slice(cb * 128, (cb + 1) * 128)
        acc = jnp.broadcast_to(b_ref[:, lanes], (TL, 128))
        for k in range(CONV_W):
            acc = acc + w_ref[k:k + 1, lanes] * xpad_scr[first + k:first + k + TL, lanes]
        conv_scr[:, lanes] = acc
    xpad_scr[0:CARRY, :] = xpad_scr[TL:TL + CARRY, :]
    o_ref[...] = _ln_swish(conv_scr[...], lg_ref[...], lb_ref[...]).astype(BF16)

    @pl.when(t == pl.num_programs(1) - 1)
    def _():
        tail_ref[0] = xpad_scr[0:CARRY, :]


def _prompt_conv(v, m, n_batch, seq, w, b, lg, lb):
    vec = pl.BlockSpec((1, D_CONV), lambda bi, t: (0, 0))
    steps = seq // TL
    return pl.pallas_call(
        _pconv_kernel, grid=(n_batch, steps),
        in_specs=[pl.BlockSpec((TL, D_CONV), lambda bi, t: (bi * steps + t, 0)),
                  pl.BlockSpec((CONV_W, D_CONV), lambda bi, t: (0, 0)), vec, vec, vec],
        out_specs=[pl.BlockSpec((TL, D_CONV), lambda bi, t: (bi * steps + t, 0)),
                   pl.BlockSpec((1, CARRY, D_CONV), lambda bi, t: (bi, 0, 0))],
        out_shape=[jax.ShapeDtypeStruct((m, D_CONV), BF16),
                   jax.ShapeDtypeStruct((n_batch, CARRY, D_CONV), F32)],
        scratch_shapes=[pltpu.VMEM((CARRY + TL, D_CONV), F32), pltpu.VMEM((TL, D_CONV), F32)],
        compiler_params=_params(("arbitrary", "arbitrary")), name="prompt_conv")(v, w, b, lg, lb)


def _gated_norm(y, z, g):
    yg = y * _silu(z)
    return yg * lax.rsqrt(jnp.mean(yg * yg, axis=-1, keepdims=True) + EPS) * g


def _pssd_kernel(xbc_ref, z_ref, dt_ref, cw_ref, cb_ref, a_ref, aexp_ref, dexp_ref, ng_ref,
                 y_ref, hfin_ref, tail_ref, state_scr, xpad_scr, xc_scr):
    c = pl.program_id(1)
    q = CHUNK

    @pl.when(c == 0)
    def _():
        state_scr[...] = jnp.zeros_like(state_scr)
        xpad_scr[0:XCARRY, :] = jnp.zeros((XCARRY, D_XBC), F32)

    xpad_scr[XCARRY:XCARRY + q, :] = xbc_ref[...]
    for cb in range(D_XBC // 512):
        lanes = slice(cb * 512, (cb + 1) * 512)
        acc = jnp.broadcast_to(cb_ref[:, lanes], (q, 512))
        for j in range(SSM_CONV_W):
            k = SSM_CONV_W - 1 - j
            acc = acc + cw_ref[k:k + 1, lanes] * xpad_scr[XCARRY - j:XCARRY - j + q, lanes]
        xc_scr[:, lanes] = _silu(acc)
    xpad_scr[0:XCARRY, :] = xpad_scr[q:q + XCARRY, :]

    row_i = lax.broadcasted_iota(jnp.int32, (q, q), 0)
    col_i = lax.broadcasted_iota(jnp.int32, (q, q), 1)
    tril = row_i >= col_i
    tri = jnp.where(tril, 1.0, 0.0).astype(BF16)
    expand = _head_expand_matrix()

    dt = dt_ref[...]
    a_cs = _dot3_rhs(tri, dt * a_ref[...])
    a_cs_t = a_cs.T
    dt_exp = _dot3_lhs(dt, expand)
    acs_exp = _dot3_rhs(tri, dt_exp * aexp_ref[...])
    last = acs_exp[q - 1:q, :]
    lane_lo = lax.broadcasted_iota(jnp.int32, (q, 128), 1) < HEAD_DIM

    for g in range(N_GROUPS):
        ch = slice(g * GROUP_W, (g + 1) * GROUP_W)
        xs = xc_scr[:, ch]
        bg = xc_scr[:, D_SSM + g * D_STATE:D_SSM + (g + 1) * D_STATE]
        cg = xc_scr[:, D_SSM + (N_GROUPS + g) * D_STATE:D_SSM + (N_GROUPS + g + 1) * D_STATE]
        bg16 = bg.astype(BF16)
        cg16 = cg.astype(BF16)
        xdt = xs * dt_exp[:, ch]
        acs_g = acs_exp[:, ch]
        cb = lax.dot_general(cg16, bg16, (((1,), (1,)), ((), ())), preferred_element_type=F32)
        st = state_scr[:, ch]
        y = _dot(cg16, st.astype(BF16)) * jnp.exp(acs_g)
        pieces = []
        for pr in range(HEADS_PER_GROUP // 2):
            xpair = xdt[:, pr * 128:(pr + 1) * 128]
            acc = None
            for half in range(2):
                h = g * HEADS_PER_GROUP + 2 * pr + half
                seg = a_cs[:, h:h + 1] - a_cs_t[h:h + 1, :]
                decay = jnp.where(tril, jnp.exp(jnp.where(tril, seg, 0.0)), 0.0)
                m = (cb * decay).astype(BF16)
                keep = lane_lo if half == 0 else jnp.logical_not(lane_lo)
                term = _dot(m, jnp.where(keep, xpair, 0.0).astype(BF16))
                acc = term if acc is None else acc + term
            pieces.append(acc)
        y = y + jnp.concatenate(pieces, axis=1) + dexp_ref[:, ch] * xs
        xdec = (xdt * jnp.exp(last[:, ch] - acs_g)).astype(BF16)
        s_new = lax.dot_general(bg16, xdec, (((0,), (0,)), ((), ())), preferred_element_type=F32)
        state_scr[:, ch] = st * jnp.exp(last[:, ch]) + s_new
        y_ref[:, ch] = _gated_norm(y, z_ref[:, ch], ng_ref[:, ch]).astype(BF16)

    @pl.when(c == pl.num_programs(1) - 1)
    def _():
        hfin_ref[0] = state_scr[...].T
        tail_ref[0] = xpad_scr[0:XCARRY, :]


def _prompt_ssd(xbc, z, dt, m, n_batch, seq, cw, cb, a_pad, a_exp, d_exp, ng):
    nc = seq // CHUNK

    def rows(w):
        return pl.BlockSpec((CHUNK, w), lambda bi, c: (bi * nc + c, 0))

    def vec(w, r=1):
        return pl.BlockSpec((r, w), lambda bi, c: (0, 0))

    return pl.pallas_call(
        _pssd_kernel, grid=(n_batch, nc),
        in_specs=[rows(D_XBC), rows(D_SSM), rows(HEAD_PAD), vec(D_XBC, SSM_CONV_W), vec(D_XBC),
                  vec(HEAD_PAD), vec(D_SSM), vec(D_SSM), vec(D_SSM)],
        out_specs=[rows(D_SSM), pl.BlockSpec((1, D_SSM, D_STATE), lambda bi, c: (bi, 0, 0)),
                   pl.BlockSpec((1, XCARRY, D_XBC), lambda bi, c: (bi, 0, 0))],
        out_shape=[jax.ShapeDtypeStruct((m, D_SSM), BF16),
                   jax.ShapeDtypeStruct((n_batch, D_SSM, D_STATE), F32),
                   jax.ShapeDtypeStruct((n_batch, XCARRY, D_XBC), F32)],
        scratch_shapes=[pltpu.VMEM((D_STATE, D_SSM), F32), pltpu.VMEM((XCARRY + CHUNK, D_XBC), F32),
                        pltpu.VMEM((CHUNK, D_XBC), F32)],
        compiler_params=_params(("arbitrary", "arbitrary")), name="prompt_ssd",
    )(xbc, z, dt, cw, cb, a_pad, a_exp, d_exp, ng)


def _sprep_kernel(v_ref, cst_ref, w_ref, b_ref, lg_ref, lb_ref,
                  xbc_ref, xst_ref, cw_ref, cb_ref, dt_ref, a_ref, dexp_ref, cmix_in_ref,
                  co_ref, ncst_ref, nxst_ref, xdt_ref, bc_ref, dec_ref, skip_ref):
    del cmix_in_ref
    nb = v_ref.shape[0]
    v = v_ref[...]
    acc = jnp.broadcast_to(b_ref[...], (nb, D_CONV)) + w_ref[CONV_W - 1:CONV_W, :] * v
    for k in range(CONV_W - 1):
        acc = acc + w_ref[k:k + 1, :] * cst_ref[k]
    co_ref[...] = _ln_swish(acc, lg_ref[...], lb_ref[...]).astype(BF16)
    for k in range(CONV_W - 2):
        ncst_ref[k] = cst_ref[k + 1]
    ncst_ref[CONV_W - 2] = v

    xn = xbc_ref[...]
    acc = jnp.broadcast_to(cb_ref[...], (nb, D_XBC)) + cw_ref[SSM_CONV_W - 1:SSM_CONV_W, :] * xn
    for k in range(SSM_CONV_W - 1):
        acc = acc + cw_ref[k:k + 1, :] * xst_ref[k]
    xc = _silu(acc)
    for k in range(SSM_CONV_W - 2):
        nxst_ref[k] = xst_ref[k + 1]
    nxst_ref[SSM_CONV_W - 2] = xn

    xs = xc[:, :D_SSM]
    bc_ref[...] = xc[:, D_SSM:]
    dt = dt_ref[...]
    dec_ref[...] = jnp.exp(dt * a_ref[...])
    dt_exp = _dot3_lhs(dt, _head_expand_matrix())
    xdt_ref[...] = xs * dt_exp
    skip_ref[...] = dexp_ref[...] * xs


def _sssd_kernel(dec_ref, h0_ref, xdt_ref, bc_ref, skip_ref, z_ref, ng_ref, ymix_in_ref,
                 hn_ref, y_ref, yt_scr, xdt_t_scr):
    del ymix_in_ref
    b = pl.program_id(0)
    nb = pl.num_programs(0)

    @pl.when(b == 0)
    def _():
        yt_scr[...] = jnp.zeros_like(yt_scr)
        xdt_t_scr[...] = xdt_ref[...].T.astype(BF16)

    brow = bc_ref[pl.ds(b, 1), :]
    is_b = lax.broadcasted_iota(jnp.int32, (nb, D_STATE), 0) == b
    for g in range(N_GROUPS):
        rows = slice(g * GROUP_W, (g + 1) * GROUP_W)
        b_sel = jnp.where(is_b, brow[:, g * D_STATE:(g + 1) * D_STATE], 0.0).astype(BF16)
        c_sel = jnp.where(is_b, brow[:, (N_GROUPS + g) * D_STATE:(N_GROUPS + g + 1) * D_STATE],
                          0.0).astype(BF16)
        s_new = _dot(xdt_t_scr[rows, :], b_sel)
        parts = []
        for r in range(HEADS_PER_GROUP):
            h = g * HEADS_PER_GROUP + r
            hr = slice(h * HEAD_DIM, (h + 1) * HEAD_DIM)
            parts.append(h0_ref[0, hr, :] * dec_ref[b * N_HEADS + h]
                         + s_new[r * HEAD_DIM:(r + 1) * HEAD_DIM, :])
        h_new = jnp.concatenate(parts, axis=0)
        hn_ref[0, rows, :] = h_new
        yt_scr[rows, :] += lax.dot_general(h_new.astype(BF16), c_sel, (((1,), (1,)), ((), ())),
                                           preferred_element_type=F32)

    @pl.when(b == nb - 1)
    def _():
        y = yt_scr[...].T + skip_ref[...]
        for g in range(N_GROUPS):
            ch = slice(g * GROUP_W, (g + 1) * GROUP_W)
            y_ref[:, ch] = _gated_norm(y[:, ch], z_ref[:, ch], ng_ref[:, ch]).astype(BF16)


def _sample_mixer(glu, z, xbc, dt, cmix, ymix, n_prompt, n_samp, cst, xst, h0,
                  mw, mb, lg, lb, cw, cb, a_pad, d_exp, ng):
    sb = SAMPLE_BLOCK
    off = n_prompt // sb
    srow = lambda w: pl.BlockSpec((sb, w), lambda i: (off + i, 0))
    orow = lambda w: pl.BlockSpec((sb, w), lambda i: (i, 0))
    taps = lambda k, w: pl.BlockSpec((k, sb, w), lambda i: (0, i, 0))
    const = lambda *s: pl.BlockSpec(s, lambda i: (0,) * len(s))
    anyspec = pl.BlockSpec(memory_space=pl.ANY)
    cst_t = jnp.swapaxes(cst, 0, 1)
    xst_t = jnp.swapaxes(xst, 0, 1)
    cmix, ncst_t, nxst_t, xdt, bc, dec, skip = pl.pallas_call(
        _sprep_kernel, grid=(n_samp // sb,),
        in_specs=[srow(D_CONV), taps(CONV_W - 1, D_CONV), const(CONV_W, D_CONV), const(1, D_CONV),
                  const(1, D_CONV), const(1, D_CONV),
                  srow(D_XBC), taps(SSM_CONV_W - 1, D_XBC), const(SSM_CONV_W, D_XBC),
                  const(1, D_XBC), srow(HEAD_PAD), const(1, HEAD_PAD), const(1, D_SSM), anyspec],
        out_specs=[srow(D_CONV), taps(CONV_W - 1, D_CONV), taps(SSM_CONV_W - 1, D_XBC),
                   orow(D_SSM), orow(D_XBC - D_SSM), orow(HEAD_PAD), orow(D_SSM)],
        out_shape=[jax.ShapeDtypeStruct(cmix.shape, BF16),
                   jax.ShapeDtypeStruct((CONV_W - 1, n_samp, D_CONV), F32),
                   jax.ShapeDtypeStruct((SSM_CONV_W - 1, n_samp, D_XBC), F32),
                   jax.ShapeDtypeStruct((n_samp, D_SSM), F32),
                   jax.ShapeDtypeStruct((n_samp, D_XBC - D_SSM), F32),
                   jax.ShapeDtypeStruct((n_samp, HEAD_PAD), F32),
                   jax.ShapeDtypeStruct((n_samp, D_SSM), F32)],
        input_output_aliases={13: 0},
        compiler_params=_params(("arbitrary",)), name="sample_prep",
    )(glu, cst_t, mw, mb, lg, lb, xbc, xst_t, cw, cb, dt, a_pad, d_exp, cmix)

    dec_flat = dec[:, :N_HEADS].reshape(-1)
    sblock = n_prompt // n_samp
    hn, ymix = pl.pallas_call(
        _sssd_kernel, grid=(n_samp,),
        in_specs=[pl.BlockSpec(memory_space=pltpu.SMEM),
                  pl.BlockSpec((1, D_SSM, D_STATE), lambda b: (b, 0, 0)),
                  const(n_samp, D_SSM), const(n_samp, D_XBC - D_SSM), const(n_samp, D_SSM),
                  pl.BlockSpec((n_samp, D_SSM), lambda b: (sblock, 0)), const(1, D_SSM), anyspec],
        out_specs=[pl.BlockSpec((1, D_SSM, D_STATE), lambda b: (b, 0, 0)),
                   pl.BlockSpec((n_samp, D_SSM), lambda b: (sblock, 0))],
        out_shape=[jax.ShapeDtypeStruct((n_samp, D_SSM, D_STATE), F32),
                   jax.ShapeDtypeStruct(ymix.shape, BF16)],
        scratch_shapes=[pltpu.VMEM((D_SSM, n_samp), F32), pltpu.VMEM((D_SSM, n_samp), BF16)],
        input_output_aliases={7: 1},
        compiler_params=_params(("arbitrary",)), name="sample_ssd",
    )(dec_flat, h0.reshape(n_samp, D_SSM, D_STATE), xdt, bc, skip, z, ng, ymix)
    return cmix, ymix, jnp.swapaxes(ncst_t, 0, 1), jnp.swapaxes(nxst_t, 0, 1), hn


def _outproj_kernel(c_ref, y_ref, w_ref, h_ref, g_ref, o_ref):
    k = pl.program_id(1)

    @pl.when(k == 0)
    def _():
        o_ref[...] = _dot(c_ref[...], w_ref[...])

    @pl.when(k > 0)
    def _():
        o_ref[...] += _dot(y_ref[...], w_ref[...])

    @pl.when(k == pl.num_programs(1) - 1)
    def _():
        o_ref[...] = h_ref[...] + _rms(o_ref[...], g_ref[...])


def _out_proj(cmix, ymix, w_out, h, g):
    m = h.shape[0]
    kt = D_CONV
    nk = (D_CONV + D_SSM) // kt
    row = pl.BlockSpec((TM, D_MODEL), lambda i, k: (i, 0))
    return pl.pallas_call(
        _outproj_kernel, grid=(m // TM, nk),
        in_specs=[pl.BlockSpec((TM, kt), lambda i, k: (i, 0)),
                  pl.BlockSpec((TM, kt), lambda i, k: (i, jnp.maximum(k - 1, 0))),
                  pl.BlockSpec((kt, D_MODEL), lambda i, k: (k, 0)),
                  row, pl.BlockSpec((1, D_MODEL), lambda i, k: (0, 0))],
        out_specs=row, out_shape=jax.ShapeDtypeStruct((m, D_MODEL), F32),
        compiler_params=_params(("arbitrary", "arbitrary")), name="out_proj")(cmix, ymix, w_out, h, g)


def _ple_kernel(h_ref, pa_ref, pb_ref, gpre_ref, wg_ref, wp_ref, gpost_ref, oa_ref, ob_ref, emb_scr):
    i = pl.program_id(0)

    def embed(rows, src, srows):
        emb_scr[rows, :] = _dot(src[srows, :].astype(BF16), wp_ref[...])
    _on_tile_rows(i, pa_ref, pb_ref, embed)

    h = h_ref[...]
    gate = jax.nn.sigmoid(_dot(_rms(h, gpre_ref[...]).astype(BF16), wg_ref[...]))
    oa_ref[...] = h + _rms(gate * emb_scr[...], gpost_ref[...])

    @pl.when(i == pl.num_programs(0) - 1)
    def _():
        nb = ob_ref.shape[0]
        ob_ref[...] = oa_ref[TM - nb:TM, :]


def _ple(h, p_prompt, p_samp, gpre, wg, wp, gpost):
    m = h.shape[0]
    n_prompt, n_samp = p_prompt.shape[0], p_samp.shape[0]
    row = pl.BlockSpec((TM, D_MODEL), lambda i: (i, 0))
    vec = pl.BlockSpec((1, D_MODEL), lambda i: (0, 0))
    return pl.pallas_call(
        _ple_kernel, grid=(m // TM,),
        in_specs=[row, pl.BlockSpec((TM, PLE_DIM), lambda i: (i, 0)),
                  pl.BlockSpec((n_samp, PLE_DIM), lambda i: (0, 0)), vec,
                  pl.BlockSpec((D_MODEL, D_MODEL), lambda i: (0, 0)),
                  pl.BlockSpec((PLE_DIM, D_MODEL), lambda i: (0, 0)), vec],
        out_specs=[row, pl.BlockSpec((n_samp, D_MODEL), lambda i: (0, 0))],
        out_shape=[jax.ShapeDtypeStruct((n_prompt, D_MODEL), F32),
                   jax.ShapeDtypeStruct((n_samp, D_MODEL), F32)],
        scratch_shapes=[pltpu.VMEM((TM, D_MODEL), F32)],
        compiler_params=_params(("arbitrary",)), name="ple")(h, p_prompt, p_samp, gpre, wg, wp, gpost)


def _layer(x_prompt, x_samp, p_prompt, p_samp, n_batch, seq, cst, xst, h0, lw):
    (norm_ffn1_pre, w_ffn1_gate, w_ffn1_up, w_ffn1_down, norm_ffn1_post,
     norm_mix_pre, w_in, conv_mod_w, conv_mod_b, conv_mod_ln_g, conv_mod_ln_b,
     ssm_conv_w, ssm_conv_b, dt_bias, a_log, d_skip, ssm_norm_g, w_out, norm_mix_post,
     norm_ffn2_pre, w_ffn2_gate, w_ffn2_up, w_ffn2_down, norm_ffn2_post,
     norm_ple_pre, w_ple_gate, w_ple_proj, norm_ple_post) = lw
    n_prompt, n_samp = x_prompt.shape[0], x_samp.shape[0]
    m = n_prompt + n_samp
    _check_split(n_prompt, n_samp)
    row2 = lambda t: t.reshape(1, -1)
    bf = lambda t: t.astype(BF16)

    later = ((w_in, 16), (w_out, 32), (w_ffn2_gate, 16), (w_ffn2_up, 16), (w_ffn2_down, 64),
             (w_ple_gate, 16), (w_ple_proj, 16))
    h1, u, w_in16, w_out16, wg2, wu2, wd2, wpg, wpp = _ffn(
        x_prompt, row2(norm_ffn1_pre), bf(w_ffn1_gate), bf(w_ffn1_up), bf(w_ffn1_down),
        row2(norm_ffn1_post), gnext=row2(norm_mix_pre), x_tail=x_samp, casts=later)

    d_proj = w_in.shape[1]
    w_dt = jnp.pad(w_in16[:, d_proj - N_HEADS:], ((0, 0), (0, HEAD_PAD - N_HEADS)))
    pad_h = lambda t: jnp.pad(t.astype(F32), (0, HEAD_PAD - N_HEADS)).reshape(1, HEAD_PAD)
    glu, z, xbc, dt = _in_proj(u, w_in16, w_dt, pad_h(dt_bias))

    a = -jnp.exp(a_log.astype(F32))
    a_pad = pad_h(a)
    a_exp = row2(jnp.repeat(a, HEAD_DIM))
    d_exp = row2(jnp.repeat(d_skip.astype(F32), HEAD_DIM))
    cw, cb = ssm_conv_w, row2(ssm_conv_b)
    mw, mb, lg, lb = conv_mod_w, row2(conv_mod_b), row2(conv_mod_ln_g), row2(conv_mod_ln_b)
    ng = row2(ssm_norm_g)

    cmix, glu_tail = _prompt_conv(glu, m, n_batch, seq, mw, mb, lg, lb)
    ymix, hfin_prompt, xbc_tail = _prompt_ssd(xbc, z, dt, m, n_batch, seq, cw, cb, a_pad, a_exp, d_exp, ng)
    cmix, ymix, new_cst_samp, new_xst_samp, hn = _sample_mixer(
        glu, z, xbc, dt, cmix, ymix, n_prompt, n_samp, cst, xst, h0,
        mw, mb, lg, lb, cw, cb, a_pad, d_exp, ng)

    h2 = _out_proj(cmix, ymix, w_out16, h1, row2(norm_mix_post))
    (h3,) = _ffn(h2, row2(norm_ffn2_pre), wg2, wu2, wd2, row2(norm_ffn2_post))
    y_prompt, y_samp = _ple(h3, p_prompt, p_samp, row2(norm_ple_pre), wpg, wpp, row2(norm_ple_post))

    new_cst_prompt = glu_tail[:, CARRY - (CONV_W - 1):]
    new_xst_prompt = xbc_tail[:, XCARRY - (SSM_CONV_W - 1):]
    new_h_prompt = hfin_prompt.reshape(n_batch, N_HEADS, HEAD_DIM, D_STATE)
    new_h_samp = hn.reshape(n_samp, N_HEADS, HEAD_DIM, D_STATE)
    return (y_prompt, y_samp, new_cst_prompt, new_xst_prompt, new_h_prompt,
            new_cst_samp, new_xst_samp, new_h_samp)


def kernel(x_prompt, x_sample, state_conv_mod, state_ssm_conv, state_ssm, p_prompt, p_sample,
           norm_ffn1_pre, w_ffn1_gate, w_ffn1_up, w_ffn1_down, norm_ffn1_post,
           norm_mix_pre, w_in, conv_mod_w, conv_mod_b, conv_mod_ln_g, conv_mod_ln_b,
           ssm_conv_w, ssm_conv_b, dt_bias, a_log, d_skip, ssm_norm_g, w_out, norm_mix_post,
           norm_ffn2_pre, w_ffn2_gate, w_ffn2_up, w_ffn2_down, norm_ffn2_post,
           norm_ple_pre, w_ple_gate, w_ple_proj, norm_ple_post):
    weights = (norm_ffn1_pre, w_ffn1_gate, w_ffn1_up, w_ffn1_down, norm_ffn1_post,
               norm_mix_pre, w_in, conv_mod_w, conv_mod_b, conv_mod_ln_g, conv_mod_ln_b,
               ssm_conv_w, ssm_conv_b, dt_bias, a_log, d_skip, ssm_norm_g, w_out, norm_mix_post,
               norm_ffn2_pre, w_ffn2_gate, w_ffn2_up, w_ffn2_down, norm_ffn2_post,
               norm_ple_pre, w_ple_gate, w_ple_proj, norm_ple_post)
    n_batch, seq, _ = x_prompt.shape
    n_samp = x_sample.shape[0]
    n_prompt = n_batch * seq
    depth = norm_ffn1_pre.shape[0]
    xp = x_prompt.reshape(n_prompt, D_MODEL)
    xs = x_sample.reshape(n_samp, D_MODEL)
    outs = [[] for _ in range(6)]
    for i in range(depth):
        res = _layer(xp, xs, p_prompt[i].reshape(n_prompt, PLE_DIM), p_sample[i].reshape(n_samp, PLE_DIM),
                     n_batch, seq, state_conv_mod[i], state_ssm_conv[i], state_ssm[i],
                     tuple(w[i] for w in weights))
        xp, xs = res[0], res[1]
        for lst, r in zip(outs, res[2:]):
            lst.append(r)
    return ((xp.reshape(n_batch, seq, D_MODEL), xs.reshape(n_samp, 1, D_MODEL))
            + tuple(jnp.stack(lst, axis=0) for lst in outs))
```

```python
import functools

import jax
import jax.numpy as jnp
from jax import lax
from jax.experimental import pallas as pl
from jax.experimental.pallas import tpu as pltpu

F32 = jnp.float32
BF16 = jnp.bfloat16

D_MODEL = 2048
D_FF = 5632
D_CONV = 1024
D_SSM = 3072
N_HEADS = 48
HEAD_DIM = 64
N_GROUPS = 8
HEADS_PER_GROUP = 6
GROUP_W = HEADS_PER_GROUP * HEAD_DIM
D_STATE = 128
D_XBC = D_SSM + 2 * N_GROUPS * D_STATE
CONV_W = 31
SSM_CONV_W = 4
CHUNK = 128
PLE_DIM = 256
EPS = 1e-6
NEG_BIG = -1e30
HEAD_PAD = 128

TM = 640
TF = 512
TMP = 1040
TN = 512
TL = 256
CARRY = 32
CONV_ROWS = 128
CONV_SPAN = TL + CARRY - 8
XCARRY = 8
SAMPLE_BLOCK = 32
SSD_BLOCK = 4
VMEM_LIMIT = 56 * 1024 * 1024


def _params(dims, vmem=VMEM_LIMIT):
    return pltpu.CompilerParams(dimension_semantics=dims, vmem_limit_bytes=vmem)


def _rms(x, g):
    return x * lax.rsqrt(jnp.mean(x * x, axis=-1, keepdims=True) + EPS) * g


def _silu(x):
    return x * jax.nn.sigmoid(x)


def _dot(a, b):
    return jnp.dot(a, b, preferred_element_type=F32)


def _split3(x):
    hi = x.astype(BF16)
    r = x - hi.astype(F32)
    mid = r.astype(BF16)
    lo = (r - mid.astype(F32)).astype(BF16)
    return hi, mid, lo


def _dot3_rhs(a_bf16, x):
    hi, mid, lo = _split3(x)
    return _dot(a_bf16, hi) + _dot(a_bf16, mid) + _dot(a_bf16, lo)


def _dot3_lhs(x, b_bf16):
    hi, mid, lo = _split3(x)
    return _dot(hi, b_bf16) + _dot(mid, b_bf16) + _dot(lo, b_bf16)


def _head_expand_matrix():
    head = lax.broadcasted_iota(jnp.int32, (HEAD_PAD, D_SSM), 0)
    chan = lax.broadcasted_iota(jnp.int32, (HEAD_PAD, D_SSM), 1)
    return jnp.where((chan >> 6) == head, 1.0, 0.0).astype(BF16)


def _on_tile_rows(i, a_ref, b_ref, fn):
    if b_ref is None:
        fn(slice(0, TM), a_ref, slice(0, TM))
        return
    nb = b_ref.shape[0]
    na = TM - nb
    last = pl.num_programs(0) - 1

    @pl.when(i != last)
    def _():
        fn(slice(0, TM), a_ref, slice(0, TM))

    @pl.when(i == last)
    def _():
        fn(slice(0, na), a_ref, slice(0, na))
        fn(slice(na, TM), b_ref, slice(0, nb))


def _check_split(n_prompt, n_samp):
    assert (n_prompt + n_samp) % TM == 0 and n_samp < TM and n_samp % 16 == 0


def _ffn_kernel(*refs, split, n_next, cast_t):
    n_cast = len(cast_t)
    refs = list(refs)
    xa_ref = refs.pop(0)
    xb_ref = refs.pop(0) if split else None
    gpre_ref, wg_ref, wu_ref, wd_ref, gpost_ref = refs[:5]
    refs = refs[5:]
    gnext_ref = refs.pop(0) if n_next else None
    cast_in, refs = refs[:n_cast], refs[n_cast:]
    o_ref = refs.pop(0)
    unext_ref = refs.pop(0) if n_next else None
    cast_out, refs = refs[:n_cast], refs[n_cast:]
    (u_scr,) = refs
    i = pl.program_id(0)
    j = pl.program_id(1)

    @pl.when(j == 0)
    def _():
        def pre(rows, src, srows):
            u_scr[rows, :] = _rms(src[srows, :], gpre_ref[...]).astype(BF16)
        _on_tile_rows(i, xa_ref, xb_ref, pre)
        o_ref[...] = jnp.zeros_like(o_ref)

    for ci, co, transpose in zip(cast_in, cast_out, cast_t):
        co[...] = (ci[...].T if transpose else ci[...]).astype(BF16)

    u = u_scr[...]
    act = (_silu(_dot(u, wg_ref[...])) * _dot(u, wu_ref[...])).astype(BF16)
    o_ref[...] += _dot(act, wd_ref[...])

    @pl.when(j == pl.num_programs(1) - 1)
    def _():
        def post(rows, src, srows):
            h = src[srows, :] + 0.5 * _rms(o_ref[rows, :], gpost_ref[...])
            o_ref[rows, :] = h
            if n_next:
                unext_ref[rows, :] = _rms(h, gnext_ref[...]).astype(BF16)
        _on_tile_rows(i, xa_ref, xb_ref, post)


def _ffn(x, gpre, wg, wu, wd, gpost, gnext=None, x_tail=None, casts=()):
    m = x.shape[0] + (0 if x_tail is None else x_tail.shape[0])
    grid = (m // TM, D_FF // TF)
    steps = grid[0] * grid[1]
    row = pl.BlockSpec((TM, D_MODEL), lambda i, j: (i, 0))
    vec = pl.BlockSpec((1, D_MODEL), lambda i, j: (0, 0))
    in_specs, args = [row], [x]
    if x_tail is not None:
        in_specs.append(pl.BlockSpec(x_tail.shape, lambda i, j: (0, 0)))
        args.append(x_tail)
    in_specs += [vec,
                 pl.BlockSpec((D_MODEL, TF), lambda i, j: (0, j)),
                 pl.BlockSpec((D_MODEL, TF), lambda i, j: (0, j)),
                 pl.BlockSpec((TF, D_MODEL), lambda i, j: (j, 0)),
                 vec]
    args += [gpre, wg, wu, wd, gpost]
    out_shape = [jax.ShapeDtypeStruct((m, D_MODEL), F32)]
    out_specs = [row]
    if gnext is not None:
        in_specs.append(vec)
        args.append(gnext)
        out_shape.append(jax.ShapeDtypeStruct((m, D_MODEL), BF16))
        out_specs.append(row)
    for w, r, transpose in casts:
        nblk = pl.cdiv(w.shape[0], r)
        assert r % 16 == 0 and nblk <= steps and (w.shape[0] % r == 0 or transpose)
        slab = lambda i, j, nblk=nblk: jnp.minimum(i * grid[1] + j, nblk - 1)
        in_specs.append(pl.BlockSpec((r, w.shape[1]), lambda i, j, slab=slab: (slab(i, j), 0)))
        args.append(w)
        if transpose:
            assert r % 128 == 0
            out_specs.append(pl.BlockSpec((w.shape[1], r), lambda i, j, slab=slab: (0, slab(i, j))))
            out_shape.append(jax.ShapeDtypeStruct(w.shape[::-1], BF16))
        else:
            out_specs.append(pl.BlockSpec((r, w.shape[1]), lambda i, j, slab=slab: (slab(i, j), 0)))
            out_shape.append(jax.ShapeDtypeStruct(w.shape, BF16))
    return pl.pallas_call(
        functools.partial(_ffn_kernel, split=x_tail is not None, n_next=gnext is not None,
                          cast_t=tuple(t for _, _, t in casts)),
        grid=grid, in_specs=in_specs, out_specs=out_specs, out_shape=out_shape,
        scratch_shapes=[pltpu.VMEM((TM, D_MODEL), BF16)],
        compiler_params=_params(("arbitrary", "arbitrary")),
        name="ffn_first" if casts else "ffn",
    )(*args)


N_GLU_STEPS = D_CONV // TN
N_Z_STEPS = D_SSM // TN
N_XBC_STEPS = D_XBC // TN


def _inproj_kernel(u_ref, w_ref, wb_ref, wdt_ref, dtb_ref, glu_ref, z_ref, xbc_ref, dt_ref):
    j = pl.program_id(1)
    u = u_ref[...]

    @pl.when(j < N_GLU_STEPS)
    def _():
        glu_ref[...] = _dot(u, w_ref[...]) * jax.nn.sigmoid(_dot(u, wb_ref[...]))

    @pl.when(jnp.logical_and(j >= N_GLU_STEPS, j < N_GLU_STEPS + N_Z_STEPS))
    def _():
        z_ref[...] = _dot(u, w_ref[...])

    @pl.when(j >= N_GLU_STEPS + N_Z_STEPS)
    def _():
        xbc_ref[...] = _dot(u, w_ref[...])

    @pl.when(j == pl.num_programs(1) - 1)
    def _():
        x = _dot(u, wdt_ref[...]) + dtb_ref[...]
        dt_ref[...] = jnp.maximum(x, 0.0) + jnp.log1p(jnp.exp(-jnp.abs(x)))


def _in_proj(u, w_in, w_dt, dt_bias):
    m = u.shape[0]
    assert m % TMP == 0
    g, nz, nx = N_GLU_STEPS, N_Z_STEPS, N_XBC_STEPS
    return pl.pallas_call(
        _inproj_kernel, grid=(m // TMP, g + nz + nx),
        in_specs=[pl.BlockSpec((TMP, D_MODEL), lambda i, j: (i, 0)),
                  pl.BlockSpec((D_MODEL, TN), lambda i, j: (0, jnp.where(j < g, j, j + g))),
                  pl.BlockSpec((D_MODEL, TN), lambda i, j: (0, g + jnp.minimum(j, g - 1))),
                  pl.BlockSpec((D_MODEL, HEAD_PAD), lambda i, j: (0, 0)),
                  pl.BlockSpec((1, HEAD_PAD), lambda i, j: (0, 0))],
        out_specs=[pl.BlockSpec((TMP, TN), lambda i, j: (i, jnp.minimum(j, g - 1))),
                   pl.BlockSpec((TMP, TN), lambda i, j: (i, jnp.clip(j - g, 0, nz - 1))),
                   pl.BlockSpec((TMP, TN), lambda i, j: (i, jnp.clip(j - g - nz, 0, nx - 1))),
                   pl.BlockSpec((TMP, HEAD_PAD), lambda i, j: (i, 0))],
        out_shape=[jax.ShapeDtypeStruct((m, D_CONV), F32), jax.ShapeDtypeStruct((m, D_SSM), F32),
                   jax.ShapeDtypeStruct((m, D_XBC), F32), jax.ShapeDtypeStruct((m, HEAD_PAD), F32)],
        compiler_params=_params(("arbitrary", "arbitrary")), name="in_proj",
    )(u, w_in, w_in, w_dt, dt_bias)


def _ln_swish(y, g, b):
    mu = jnp.mean(y, axis=-1, keepdims=True)
    yc = y - mu
    yn = yc * lax.rsqrt(jnp.mean(yc * yc, axis=-1, keepdims=True) + EPS) * g + b
    return _silu(yn)


def _pconv_kernel(v_ref, w_ref, b_ref, lg_ref, lb_ref, o_ref, tail_ref, xpad_scr, conv_scr, shift_scr):
    t = pl.program_id(1)

    @pl.when(t == 0)
    def _():
        xpad_scr[0:CARRY, :] = jnp.zeros((CARRY, D_CONV), F32)

    xpad_scr[CARRY:CARRY + TL, :] = v_ref[...]
    first = CARRY - (CONV_W - 1)
    hb = CONV_ROWS
    bases = range(0, TL, hb)
    for cb in range(D_CONV // 128):
        lanes = slice(cb * 128, (cb + 1) * 128)
        for phase in range(1, 8):
            shift_scr[phase - 1] = xpad_scr[phase:phase + CONV_SPAN, lanes]
        acc = [jnp.broadcast_to(b_ref[:, lanes], (hb, 128)) for _ in bases]
        for k in range(CONV_W):
            phase, off = (first + k) % 8, 8 * ((first + k) // 8)
            w = w_ref[k:k + 1, lanes]
            for n, base in enumerate(bases):
                if phase:
                    x = shift_scr[phase - 1, base + off:base + off + hb, :]
                else:
                    x = xpad_scr[base + off:base + off + hb, lanes]
                acc[n] = acc[n] + w * x
        for n, base in enumerate(bases):
            conv_scr[base:base + hb, lanes] = acc[n]
    xpad_scr[0:CARRY, :] = xpad_scr[TL:TL + CARRY, :]
    o_ref[...] = _ln_swish(conv_scr[...], lg_ref[...], lb_ref[...]).astype(BF16)

    @pl.when(t == pl.num_programs(1) - 1)
    def _():
        tail_ref[0] = xpad_scr[0:CARRY, :]


def _prompt_conv(v, m, n_batch, seq, w, b, lg, lb):
    vec = pl.BlockSpec((1, D_CONV), lambda bi, t: (0, 0))
    steps = seq // TL
    return pl.pallas_call(
        _pconv_kernel, grid=(n_batch, steps),
        in_specs=[pl.BlockSpec((TL, D_CONV), lambda bi, t: (bi * steps + t, 0)),
                  pl.BlockSpec((CONV_W, D_CONV), lambda bi, t: (0, 0)), vec, vec, vec],
        out_specs=[pl.BlockSpec((TL, D_CONV), lambda bi, t: (bi * steps + t, 0)),
                   pl.BlockSpec((1, CARRY, D_CONV), lambda bi, t: (bi, 0, 0))],
        out_shape=[jax.ShapeDtypeStruct((m, D_CONV), BF16),
                   jax.ShapeDtypeStruct((n_batch, CARRY, D_CONV), F32)],
        scratch_shapes=[pltpu.VMEM((CARRY + TL, D_CONV), F32), pltpu.VMEM((TL, D_CONV), F32),
                        pltpu.VMEM((7, CONV_SPAN, 128), F32)],
        compiler_params=_params(("arbitrary", "arbitrary")), name="prompt_conv")(v, w, b, lg, lb)


def _gated_norm(y, z, g):
    yg = y * _silu(z)
    return yg * lax.rsqrt(jnp.mean(yg * yg, axis=-1, keepdims=True) + EPS) * g


def _pssd_kernel(xbc_ref, z_ref, dt_ref, cw_ref, cb_ref, a_ref, expand_ref, dexp_ref, ng_ref,
                 y_ref, hfin_ref, tail_ref, state_scr, xpad_scr, xc_scr):
    c = pl.program_id(1)
    q = CHUNK

    @pl.when(c == 0)
    def _():
        state_scr[...] = jnp.zeros_like(state_scr)
        xpad_scr[0:XCARRY, :] = jnp.zeros((XCARRY, D_XBC), F32)

    xpad_scr[XCARRY:XCARRY + q, :] = xbc_ref[...]
    for cb in range(D_XBC // 512):
        lanes = slice(cb * 512, (cb + 1) * 512)
        acc = jnp.broadcast_to(cb_ref[:, lanes], (q, 512))
        for j in range(SSM_CONV_W):
            k = SSM_CONV_W - 1 - j
            acc = acc + cw_ref[k:k + 1, lanes] * xpad_scr[XCARRY - j:XCARRY - j + q, lanes]
        xc_scr[:, lanes] = _silu(acc)
    xpad_scr[0:XCARRY, :] = xpad_scr[q:q + XCARRY, :]

    row_i = lax.broadcasted_iota(jnp.int32, (q, q), 0)
    col_i = lax.broadcasted_iota(jnp.int32, (q, q), 1)
    tril = row_i >= col_i
    tri = jnp.where(tril, 1.0, 0.0).astype(BF16)
    expand = expand_ref[...]

    dt = dt_ref[...]
    a_cs = _dot3_rhs(tri, dt * a_ref[...])
    a_cs_t = a_cs.T
    dt_exp = _dot3_lhs(dt, expand)
    acs_exp = _dot3_lhs(a_cs, expand)
    last = acs_exp[q - 1:q, :]
    lane_lo = lax.broadcasted_iota(jnp.int32, (q, 128), 1) < HEAD_DIM

    for g in range(N_GROUPS):
        ch = slice(g * GROUP_W, (g + 1) * GROUP_W)
        xs = xc_scr[:, ch]
        bg = xc_scr[:, D_SSM + g * D_STATE:D_SSM + (g + 1) * D_STATE]
        cg = xc_scr[:, D_SSM + (N_GROUPS + g) * D_STATE:D_SSM + (N_GROUPS + g + 1) * D_STATE]
        bg16 = bg.astype(BF16)
        cg16 = cg.astype(BF16)
        xdt = xs * dt_exp[:, ch]
        acs_g = acs_exp[:, ch]
        cb = lax.dot_general(cg16, bg16, (((1,), (1,)), ((), ())), preferred_element_type=F32)
        st = state_scr[:, ch]
        y = _dot(cg16, st.astype(BF16)) * jnp.exp(acs_g)
        pieces = []
        for pr in range(HEADS_PER_GROUP // 2):
            xpair = xdt[:, pr * 128:(pr + 1) * 128]
            acc = None
            for half in range(2):
                h = g * HEADS_PER_GROUP + 2 * pr + half
                seg = a_cs[:, h:h + 1] - a_cs_t[h:h + 1, :]
                decay = jnp.exp(jnp.where(tril, seg, NEG_BIG))
                m = (cb * decay).astype(BF16)
                keep = lane_lo if half == 0 else jnp.logical_not(lane_lo)
                term = _dot(m, jnp.where(keep, xpair, 0.0).astype(BF16))
                acc = term if acc is None else acc + term
            pieces.append(acc)
        y = y + jnp.concatenate(pieces, axis=1) + dexp_ref[:, ch] * xs
        xdec = (xdt * jnp.exp(last[:, ch] - acs_g)).astype(BF16)
        s_new = lax.dot_general(bg16, xdec, (((0,), (0,)), ((), ())), preferred_element_type=F32)
        state_scr[:, ch] = st * jnp.exp(last[:, ch]) + s_new
        y_ref[:, ch] = _gated_norm(y, z_ref[:, ch], ng_ref[:, ch]).astype(BF16)

    @pl.when(c == pl.num_programs(1) - 1)
    def _():
        hfin_ref[0] = state_scr[...].T
        tail_ref[0] = xpad_scr[0:XCARRY, :]


def _prompt_ssd(xbc, z, dt, m, n_batch, seq, cw, cb, a_pad, d_exp, ng):
    nc = seq // CHUNK

    def rows(w):
        return pl.BlockSpec((CHUNK, w), lambda bi, c: (bi * nc + c, 0))

    def vec(w, r=1):
        return pl.BlockSpec((r, w), lambda bi, c: (0, 0))

    return pl.pallas_call(
        _pssd_kernel, grid=(n_batch, nc),
        in_specs=[rows(D_XBC), rows(D_SSM), rows(HEAD_PAD), vec(D_XBC, SSM_CONV_W), vec(D_XBC),
                  vec(HEAD_PAD), vec(D_SSM, HEAD_PAD), vec(D_SSM), vec(D_SSM)],
        out_specs=[rows(D_SSM), pl.BlockSpec((1, D_SSM, D_STATE), lambda bi, c: (bi, 0, 0)),
                   pl.BlockSpec((1, XCARRY, D_XBC), lambda bi, c: (bi, 0, 0))],
        out_shape=[jax.ShapeDtypeStruct((m, D_SSM), BF16),
                   jax.ShapeDtypeStruct((n_batch, D_SSM, D_STATE), F32),
                   jax.ShapeDtypeStruct((n_batch, XCARRY, D_XBC), F32)],
        scratch_shapes=[pltpu.VMEM((D_STATE, D_SSM), F32), pltpu.VMEM((XCARRY + CHUNK, D_XBC), F32),
                        pltpu.VMEM((CHUNK, D_XBC), F32)],
        compiler_params=_params(("arbitrary", "arbitrary")), name="prompt_ssd",
    )(xbc, z, dt, cw, cb, a_pad, _head_expand_matrix(), d_exp, ng)


def _sprep_kernel(v_ref, cst_ref, w_ref, b_ref, lg_ref, lb_ref,
                  xbc_ref, xst_ref, cw_ref, cb_ref, dt_ref, a_ref, dexp_ref, cmix_in_ref,
                  co_ref, ncst_ref, nxst_ref, xdt_ref, bc_ref, dec_ref, skip_ref):
    del cmix_in_ref
    nb = v_ref.shape[0]
    v = v_ref[...]
    acc = jnp.broadcast_to(b_ref[...], (nb, D_CONV)) + w_ref[CONV_W - 1:CONV_W, :] * v
    for k in range(CONV_W - 1):
        acc = acc + w_ref[k:k + 1, :] * cst_ref[k]
    co_ref[...] = _ln_swish(acc, lg_ref[...], lb_ref[...]).astype(BF16)
    for k in range(CONV_W - 2):
        ncst_ref[k] = cst_ref[k + 1]
    ncst_ref[CONV_W - 2] = v

    xn = xbc_ref[...]
    acc = jnp.broadcast_to(cb_ref[...], (nb, D_XBC)) + cw_ref[SSM_CONV_W - 1:SSM_CONV_W, :] * xn
    for k in range(SSM_CONV_W - 1):
        acc = acc + cw_ref[k:k + 1, :] * xst_ref[k]
    xc = _silu(acc)
    for k in range(SSM_CONV_W - 2):
        nxst_ref[k] = xst_ref[k + 1]
    nxst_ref[SSM_CONV_W - 2] = xn

    xs = xc[:, :D_SSM]
    bc_ref[...] = xc[:, D_SSM:]
    dt = dt_ref[...]
    dec_ref[...] = jnp.exp(dt * a_ref[...])
    dt_exp = _dot3_lhs(dt, _head_expand_matrix())
    xdt_ref[...] = xs * dt_exp
    skip_ref[...] = dexp_ref[...] * xs


def _sssd_kernel(dec_ref, h0_ref, xdt_ref, bc_ref, skip_ref, z_ref, ng_ref, ymix_in_ref,
                 hn_ref, y_ref, yt_scr, xdt_t_scr):
    del ymix_in_ref
    step = pl.program_id(0)
    nb = xdt_ref.shape[0]

    @pl.when(step == 0)
    def _():
        yt_scr[...] = jnp.zeros_like(yt_scr)
        xdt_t_scr[...] = xdt_ref[...].T.astype(BF16)

    seq_i = lax.broadcasted_iota(jnp.int32, (nb, D_STATE), 0)
    for bb in range(SSD_BLOCK):
        b = step * SSD_BLOCK + bb
        brow = bc_ref[pl.ds(b, 1), :]
        is_b = seq_i == b
        for g in range(N_GROUPS):
            rows = slice(g * GROUP_W, (g + 1) * GROUP_W)
            b_sel = jnp.where(is_b, brow[:, g * D_STATE:(g + 1) * D_STATE], 0.0).astype(BF16)
            c_sel = jnp.where(is_b, brow[:, (N_GROUPS + g) * D_STATE:(N_GROUPS + g + 1) * D_STATE],
                              0.0).astype(BF16)
            s_new = _dot(xdt_t_scr[rows, :], b_sel)
            parts = []
            for r in range(HEADS_PER_GROUP):
                h = g * HEADS_PER_GROUP + r
                hr = slice(h * HEAD_DIM, (h + 1) * HEAD_DIM)
                parts.append(h0_ref[bb, hr, :] * dec_ref[b * N_HEADS + h]
                             + s_new[r * HEAD_DIM:(r + 1) * HEAD_DIM, :])
            h_new = jnp.concatenate(parts, axis=0)
            hn_ref[bb, rows, :] = h_new
            yt_scr[rows, :] += lax.dot_general(h_new.astype(BF16), c_sel, (((1,), (1,)), ((), ())),
                                               preferred_element_type=F32)

    @pl.when(step == pl.num_programs(0) - 1)
    def _():
        y = yt_scr[...].T + skip_ref[...]
        for g in range(N_GROUPS):
            ch = slice(g * GROUP_W, (g + 1) * GROUP_W)
            y_ref[:, ch] = _gated_norm(y[:, ch], z_ref[:, ch], ng_ref[:, ch]).astype(BF16)


def _sample_mixer(glu, z, xbc, dt, cmix, ymix, n_prompt, n_samp, cst, xst, h0,
                  mw, mb, lg, lb, cw, cb, a_pad, d_exp, ng):
    sb = SAMPLE_BLOCK
    off = n_prompt // sb
    srow = lambda w: pl.BlockSpec((sb, w), lambda i: (off + i, 0))
    orow = lambda w: pl.BlockSpec((sb, w), lambda i: (i, 0))
    taps = lambda k, w: pl.BlockSpec((k, sb, w), lambda i: (0, i, 0))
    const = lambda *s: pl.BlockSpec(s, lambda i: (0,) * len(s))
    anyspec = pl.BlockSpec(memory_space=pl.ANY)
    cst_t = jnp.swapaxes(cst, 0, 1)
    xst_t = jnp.swapaxes(xst, 0, 1)
    cmix, ncst_t, nxst_t, xdt, bc, dec, skip = pl.pallas_call(
        _sprep_kernel, grid=(n_samp // sb,),
        in_specs=[srow(D_CONV), taps(CONV_W - 1, D_CONV), const(CONV_W, D_CONV), const(1, D_CONV),
                  const(1, D_CONV), const(1, D_CONV),
                  srow(D_XBC), taps(SSM_CONV_W - 1, D_XBC), const(SSM_CONV_W, D_XBC),
                  const(1, D_XBC), srow(HEAD_PAD), const(1, HEAD_PAD), const(1, D_SSM), anyspec],
        out_specs=[srow(D_CONV), taps(CONV_W - 1, D_CONV), taps(SSM_CONV_W - 1, D_XBC),
                   orow(D_SSM), orow(D_XBC - D_SSM), orow(HEAD_PAD), orow(D_SSM)],
        out_shape=[jax.ShapeDtypeStruct(cmix.shape, BF16),
                   jax.ShapeDtypeStruct((CONV_W - 1, n_samp, D_CONV), F32),
                   jax.ShapeDtypeStruct((SSM_CONV_W - 1, n_samp, D_XBC), F32),
                   jax.ShapeDtypeStruct((n_samp, D_SSM), F32),
                   jax.ShapeDtypeStruct((n_samp, D_XBC - D_SSM), F32),
                   jax.ShapeDtypeStruct((n_samp, HEAD_PAD), F32),
                   jax.ShapeDtypeStruct((n_samp, D_SSM), F32)],
        input_output_aliases={13: 0},
        compiler_params=_params(("arbitrary",)), name="sample_prep",
    )(glu, cst_t, mw, mb, lg, lb, xbc, xst_t, cw, cb, dt, a_pad, d_exp, cmix)

    dec_flat = dec[:, :N_HEADS].reshape(-1)
    sblock = n_prompt // n_samp
    assert n_samp % SSD_BLOCK == 0
    hn, ymix = pl.pallas_call(
        _sssd_kernel, grid=(n_samp // SSD_BLOCK,),
        in_specs=[pl.BlockSpec(memory_space=pltpu.SMEM),
                  pl.BlockSpec((SSD_BLOCK, D_SSM, D_STATE), lambda b: (b, 0, 0)),
                  const(n_samp, D_SSM), const(n_samp, D_XBC - D_SSM), const(n_samp, D_SSM),
                  pl.BlockSpec((n_samp, D_SSM), lambda b: (sblock, 0)), const(1, D_SSM), anyspec],
        out_specs=[pl.BlockSpec((SSD_BLOCK, D_SSM, D_STATE), lambda b: (b, 0, 0)),
                   pl.BlockSpec((n_samp, D_SSM), lambda b: (sblock, 0))],
        out_shape=[jax.ShapeDtypeStruct((n_samp, D_SSM, D_STATE), F32),
                   jax.ShapeDtypeStruct(ymix.shape, BF16)],
        scratch_shapes=[pltpu.VMEM((D_SSM, n_samp), F32), pltpu.VMEM((D_SSM, n_samp), BF16)],
        input_output_aliases={7: 1},
        compiler_params=_params(("arbitrary",)), name="sample_ssd",
    )(dec_flat, h0.reshape(n_samp, D_SSM, D_STATE), xdt, bc, skip, z, ng, ymix)
    return cmix, ymix, jnp.swapaxes(ncst_t, 0, 1), jnp.swapaxes(nxst_t, 0, 1), hn


def _outproj_kernel(c_ref, y_ref, w_ref, h_ref, g_ref, o_ref):
    k = pl.program_id(1)

    @pl.when(k == 0)
    def _():
        o_ref[...] = _dot(c_ref[...], w_ref[...])

    @pl.when(k > 0)
    def _():
        o_ref[...] += _dot(y_ref[...], w_ref[...])

    @pl.when(k == pl.num_programs(1) - 1)
    def _():
        o_ref[...] = h_ref[...] + _rms(o_ref[...], g_ref[...])


def _out_proj(cmix, ymix, w_out, h, g):
    m = h.shape[0]
    kt = D_CONV
    nk = (D_CONV + D_SSM) // kt
    row = pl.BlockSpec((TM, D_MODEL), lambda i, k: (i, 0))
    return pl.pallas_call(
        _outproj_kernel, grid=(m // TM, nk),
        in_specs=[pl.BlockSpec((TM, kt), lambda i, k: (i, 0)),
                  pl.BlockSpec((TM, kt), lambda i, k: (i, jnp.maximum(k - 1, 0))),
                  pl.BlockSpec((kt, D_MODEL), lambda i, k: (k, 0)),
                  row, pl.BlockSpec((1, D_MODEL), lambda i, k: (0, 0))],
        out_specs=row, out_shape=jax.ShapeDtypeStruct((m, D_MODEL), F32),
        compiler_params=_params(("arbitrary", "arbitrary")), name="out_proj")(cmix, ymix, w_out, h, g)


def _ple_kernel(h_ref, pa_ref, pb_ref, gpre_ref, wg_ref, wp_ref, gpost_ref, oa_ref, ob_ref, emb_scr):
    i = pl.program_id(0)

    def embed(rows, src, srows):
        emb_scr[rows, :] = _dot(src[srows, :].astype(BF16), wp_ref[...])
    _on_tile_rows(i, pa_ref, pb_ref, embed)

    h = h_ref[...]
    gate = jax.nn.sigmoid(_dot(_rms(h, gpre_ref[...]).astype(BF16), wg_ref[...]))
    oa_ref[...] = h + _rms(gate * emb_scr[...], gpost_ref[...])

    @pl.when(i == pl.num_programs(0) - 1)
    def _():
        nb = ob_ref.shape[0]
        ob_ref[...] = oa_ref[TM - nb:TM, :]


def _ple(h, p_prompt, p_samp, gpre, wg, wp, gpost):
    m = h.shape[0]
    n_prompt, n_samp = p_prompt.shape[0], p_samp.shape[0]
    row = pl.BlockSpec((TM, D_MODEL), lambda i: (i, 0))
    vec = pl.BlockSpec((1, D_MODEL), lambda i: (0, 0))
    return pl.pallas_call(
        _ple_kernel, grid=(m // TM,),
        in_specs=[row, pl.BlockSpec((TM, PLE_DIM), lambda i: (i, 0)),
                  pl.BlockSpec((n_samp, PLE_DIM), lambda i: (0, 0)), vec,
                  pl.BlockSpec((D_MODEL, D_MODEL), lambda i: (0, 0)),
                  pl.BlockSpec((PLE_DIM, D_MODEL), lambda i: (0, 0)), vec],
        out_specs=[row, pl.BlockSpec((n_samp, D_MODEL), lambda i: (0, 0))],
        out_shape=[jax.ShapeDtypeStruct((n_prompt, D_MODEL), F32),
                   jax.ShapeDtypeStruct((n_samp, D_MODEL), F32)],
        scratch_shapes=[pltpu.VMEM((TM, D_MODEL), F32)],
        compiler_params=_params(("arbitrary",)), name="ple")(h, p_prompt, p_samp, gpre, wg, wp, gpost)


def _layer(x_prompt, x_samp, p_prompt, p_samp, n_batch, seq, cst, xst, h0, lw):
    (norm_ffn1_pre, w_ffn1_gate, w_ffn1_up, w_ffn1_down, norm_ffn1_post,
     norm_mix_pre, w_in, conv_mod_w, conv_mod_b, conv_mod_ln_g, conv_mod_ln_b,
     ssm_conv_w, ssm_conv_b, dt_bias, a_log, d_skip, ssm_norm_g, w_out, norm_mix_post,
     norm_ffn2_pre, w_ffn2_gate, w_ffn2_up, w_ffn2_down, norm_ffn2_post,
     norm_ple_pre, w_ple_gate, w_ple_proj, norm_ple_post) = lw
    n_prompt, n_samp = x_prompt.shape[0], x_samp.shape[0]
    m = n_prompt + n_samp
    _check_split(n_prompt, n_samp)
    row2 = lambda t: t.reshape(1, -1)
    bf = lambda t: t.astype(BF16)

    later = ((jnp.swapaxes(w_in, 0, 1), 128, True), (w_out, 32, False), (w_ffn2_gate, 16, False),
             (w_ffn2_up, 16, False), (w_ffn2_down, 64, False), (w_ple_gate, 16, False),
             (w_ple_proj, 16, False))
    h1, u, w_in16, w_out16, wg2, wu2, wd2, wpg, wpp = _ffn(
        x_prompt, row2(norm_ffn1_pre), bf(w_ffn1_gate), bf(w_ffn1_up), bf(w_ffn1_down),
        row2(norm_ffn1_post), gnext=row2(norm_mix_pre), x_tail=x_samp, casts=later)

    d_proj = w_in.shape[1]
    w_dt = jnp.pad(w_in16[:, d_proj - N_HEADS:], ((0, 0), (0, HEAD_PAD - N_HEADS)))
    pad_h = lambda t: jnp.pad(t.astype(F32), (0, HEAD_PAD - N_HEADS)).reshape(1, HEAD_PAD)
    glu, z, xbc, dt = _in_proj(u, w_in16, w_dt, pad_h(dt_bias))

    a = -jnp.exp(a_log.astype(F32))
    a_pad = pad_h(a)
    d_exp = row2(jnp.repeat(d_skip.astype(F32), HEAD_DIM))
    cw, cb = ssm_conv_w, row2(ssm_conv_b)
    mw, mb, lg, lb = conv_mod_w, row2(conv_mod_b), row2(conv_mod_ln_g), row2(conv_mod_ln_b)
    ng = row2(ssm_norm_g)

    cmix, glu_tail = _prompt_conv(glu, m, n_batch, seq, mw, mb, lg, lb)
    ymix, hfin_prompt, xbc_tail = _prompt_ssd(xbc, z, dt, m, n_batch, seq, cw, cb, a_pad, d_exp, ng)
    cmix, ymix, new_cst_samp, new_xst_samp, hn = _sample_mixer(
        glu, z, xbc, dt, cmix, ymix, n_prompt, n_samp, cst, xst, h0,
        mw, mb, lg, lb, cw, cb, a_pad, d_exp, ng)

    h2 = _out_proj(cmix, ymix, w_out16, h1, row2(norm_mix_post))
    (h3,) = _ffn(h2, row2(norm_ffn2_pre), wg2, wu2, wd2, row2(norm_ffn2_post))
    y_prompt, y_samp = _ple(h3, p_prompt, p_samp, row2(norm_ple_pre), wpg, wpp, row2(norm_ple_post))

    new_cst_prompt = glu_tail[:, CARRY - (CONV_W - 1):]
    new_xst_prompt = xbc_tail[:, XCARRY - (SSM_CONV_W - 1):]
    new_h_prompt = hfin_prompt.reshape(n_batch, N_HEADS, HEAD_DIM, D_STATE)
    new_h_samp = hn.reshape(n_samp, N_HEADS, HEAD_DIM, D_STATE)
    return (y_prompt, y_samp, new_cst_prompt, new_xst_prompt, new_h_prompt,
            new_cst_samp, new_xst_samp, new_h_samp)


def kernel(x_prompt, x_sample, state_conv_mod, state_ssm_conv, state_ssm, p_prompt, p_sample,
           norm_ffn1_pre, w_ffn1_gate, w_ffn1_up, w_ffn1_down, norm_ffn1_post,
           norm_mix_pre, w_in, conv_mod_w, conv_mod_b, conv_mod_ln_g, conv_mod_ln_b,
           ssm_conv_w, ssm_conv_b, dt_bias, a_log, d_skip, ssm_norm_g, w_out, norm_mix_post,
           norm_ffn2_pre, w_ffn2_gate, w_ffn2_up, w_ffn2_down, norm_ffn2_post,
           norm_ple_pre, w_ple_gate, w_ple_proj, norm_ple_post):
    weights = (norm_ffn1_pre, w_ffn1_gate, w_ffn1_up, w_ffn1_down, norm_ffn1_post,
               norm_mix_pre, w_in, conv_mod_w, conv_mod_b, conv_mod_ln_g, conv_mod_ln_b,
               ssm_conv_w, ssm_conv_b, dt_bias, a_log, d_skip, ssm_norm_g, w_out, norm_mix_post,
               norm_ffn2_pre, w_ffn2_gate, w_ffn2_up, w_ffn2_down, norm_ffn2_post,
               norm_ple_pre, w_ple_gate, w_ple_proj, norm_ple_post)
    n_batch, seq, _ = x_prompt.shape
    n_samp = x_sample.shape[0]
    n_prompt = n_batch * seq
    depth = norm_ffn1_pre.shape[0]
    xp = x_prompt.reshape(n_prompt, D_MODEL)
    xs = x_sample.reshape(n_samp, D_MODEL)
    outs = [[] for _ in range(6)]
    for i in range(depth):
        res = _layer(xp, xs, p_prompt[i].reshape(n_prompt, PLE_DIM), p_sample[i].reshape(n_samp, PLE_DIM),
                     n_batch, seq, state_conv_mod[i], state_ssm_conv[i], state_ssm[i],
                     tuple(w[i] for w in weights))
        xp, xs = res[0], res[1]
        for lst, r in zip(outs, res[2:]):
            lst.append(r)
    return ((xp.reshape(n_batch, seq, D_MODEL), xs.reshape(n_samp, 1, D_MODEL))
            + tuple(jnp.stack(lst, axis=0) for lst in outs))
```

```python
import functools

import jax
import jax.numpy as jnp
from jax import lax
from jax.experimental import pallas as pl
from jax.experimental.pallas import tpu as pltpu

F32 = jnp.float32
BF16 = jnp.bfloat16

D_MODEL = 2048
D_FF = 5632
D_CONV = 1024
D_SSM = 3072
N_HEADS = 48
HEAD_DIM = 64
N_GROUPS = 8
HEADS_PER_GROUP = 6
GROUP_W = HEADS_PER_GROUP * HEAD_DIM
D_STATE = 128
D_XBC = D_SSM + 2 * N_GROUPS * D_STATE
CONV_W = 31
SSM_CONV_W = 4
CHUNK = 128
PLE_DIM = 256
EPS = 1e-6
NEG_BIG = -1e30
HEAD_PAD = 128

TM = 640
TF = 512
TF_FIRST = 256
TMP = 1040
TN = 512
TL = 256
CARRY = 32
CONV_ROWS = 128
CONV_SPAN = TL + CARRY - 8
XCARRY = 8
SAMPLE_BLOCK = 32
SSD_BLOCK = 4
VMEM_LIMIT = 56 * 1024 * 1024


def _params(dims, vmem=VMEM_LIMIT):
    return pltpu.CompilerParams(dimension_semantics=dims, vmem_limit_bytes=vmem)


def _rms(x, g):
    return x * lax.rsqrt(jnp.mean(x * x, axis=-1, keepdims=True) + EPS) * g


def _silu(x):
    return x * jax.nn.sigmoid(x)


def _dot(a, b):
    return jnp.dot(a, b, preferred_element_type=F32)


def _split3(x):
    hi = x.astype(BF16)
    r = x - hi.astype(F32)
    mid = r.astype(BF16)
    lo = (r - mid.astype(F32)).astype(BF16)
    return hi, mid, lo


def _dot3_rhs(a_bf16, x):
    hi, mid, lo = _split3(x)
    return _dot(a_bf16, hi) + _dot(a_bf16, mid) + _dot(a_bf16, lo)


def _dot3_lhs(x, b_bf16):
    hi, mid, lo = _split3(x)
    return _dot(hi, b_bf16) + _dot(mid, b_bf16) + _dot(lo, b_bf16)


def _head_expand_matrix():
    head = lax.broadcasted_iota(jnp.int32, (HEAD_PAD, D_SSM), 0)
    chan = lax.broadcasted_iota(jnp.int32, (HEAD_PAD, D_SSM), 1)
    return jnp.where((chan >> 6) == head, 1.0, 0.0).astype(BF16)


def _on_tile_rows(i, a_ref, b_ref, fn):
    if b_ref is None:
        fn(slice(0, TM), a_ref, slice(0, TM))
        return
    nb = b_ref.shape[0]
    na = TM - nb
    last = pl.num_programs(0) - 1

    @pl.when(i != last)
    def _():
        fn(slice(0, TM), a_ref, slice(0, TM))

    @pl.when(i == last)
    def _():
        fn(slice(0, na), a_ref, slice(0, na))
        fn(slice(na, TM), b_ref, slice(0, nb))


def _check_split(n_prompt, n_samp):
    assert (n_prompt + n_samp) % TM == 0 and n_samp < TM and n_samp % 16 == 0


def _ffn_kernel(*refs, split, n_next, cast_t, emit_w16, n_carry):
    n_cast = len(cast_t)
    refs = list(refs)
    xa_ref = refs.pop(0)
    xb_ref = refs.pop(0) if split else None
    gpre_ref, wg_ref, wu_ref, wd_ref, gpost_ref = refs[:5]
    refs = refs[5:]
    gnext_ref = refs.pop(0) if n_next else None
    cast_in, refs = refs[:n_cast], refs[n_cast:]
    refs = refs[n_carry:]
    o_ref = refs.pop(0)
    unext_ref = refs.pop(0) if n_next else None
    cast_out, refs = refs[:n_cast], refs[n_cast:]
    w16_refs, refs = (refs[:3], refs[3:]) if emit_w16 else ((), refs)
    (u_scr,) = refs
    i = pl.program_id(0)
    j = pl.program_id(1)

    @pl.when(j == 0)
    def _():
        def pre(rows, src, srows):
            u_scr[rows, :] = _rms(src[srows, :], gpre_ref[...]).astype(BF16)
        _on_tile_rows(i, xa_ref, xb_ref, pre)
        o_ref[...] = jnp.zeros_like(o_ref)

    for ci, co, transpose in zip(cast_in, cast_out, cast_t):
        co[...] = (ci[...].T if transpose else ci[...]).astype(BF16)

    wg, wu, wd = wg_ref[...], wu_ref[...], wd_ref[...]
    if emit_w16:
        wg, wu, wd = wg.astype(BF16), wu.astype(BF16), wd.astype(BF16)
        for ref, w in zip(w16_refs, (wg, wu, wd)):
            ref[...] = w
    u = u_scr[...]
    act = (_silu(_dot(u, wg)) * _dot(u, wu)).astype(BF16)
    o_ref[...] += _dot(act, wd)

    @pl.when(j == pl.num_programs(1) - 1)
    def _():
        def post(rows, src, srows):
            h = src[srows, :] + _rms(o_ref[rows, :], gpost_ref[...])
            o_ref[rows, :] = h
            if n_next:
                unext_ref[rows, :] = _rms(h, gnext_ref[...]).astype(BF16)
        _on_tile_rows(i, xa_ref, xb_ref, post)


def _ffn(x, gpre, wg, wu, wd, gpost, gnext=None, x_tail=None, casts=(), m=None, tiles=None,
         carry=(), emit_w16=False, tf=TF, name="ffn"):
    if m is None:
        m = x.shape[0] + (0 if x_tail is None else x_tail.shape[0])
    first, count = tiles if tiles is not None else (0, m // TM)
    grid = (count, D_FF // tf)
    steps = grid[0] * grid[1]
    row = pl.BlockSpec((TM, D_MODEL), lambda i, j: (i + first, 0))
    vec = pl.BlockSpec((1, D_MODEL), lambda i, j: (0, 0))
    wcol = pl.BlockSpec((D_MODEL, tf), lambda i, j: (0, j))
    wrow = pl.BlockSpec((tf, D_MODEL), lambda i, j: (j, 0))
    in_specs, args = [row], [x]
    if x_tail is not None:
        in_specs.append(pl.BlockSpec(x_tail.shape, lambda i, j: (0, 0)))
        args.append(x_tail)
    in_specs += [vec, wcol, wcol, wrow, vec]
    args += [gpre, wg, wu, wd, 0.5 * gpost]
    out_shape = [jax.ShapeDtypeStruct((m, D_MODEL), F32)]
    out_specs = [row]
    if gnext is not None:
        in_specs.append(vec)
        args.append(gnext)
        out_shape.append(jax.ShapeDtypeStruct((m, D_MODEL), BF16))
        out_specs.append(row)
    assert len(carry) in (0, len(out_shape))
    for w, r, transpose in casts:
        nblk = pl.cdiv(w.shape[0], r)
        assert r % 16 == 0 and nblk <= steps and (w.shape[0] % r == 0 or transpose)
        slab = lambda i, j, nblk=nblk: jnp.minimum(i * grid[1] + j, nblk - 1)
        in_specs.append(pl.BlockSpec((r, w.shape[1]), lambda i, j, slab=slab: (slab(i, j), 0)))
        args.append(w)
        if transpose:
            assert r % 128 == 0
            out_specs.append(pl.BlockSpec((w.shape[1], r), lambda i, j, slab=slab: (0, slab(i, j))))
            out_shape.append(jax.ShapeDtypeStruct(w.shape[::-1], BF16))
        else:
            out_specs.append(pl.BlockSpec((r, w.shape[1]), lambda i, j, slab=slab: (slab(i, j), 0)))
            out_shape.append(jax.ShapeDtypeStruct(w.shape, BF16))
    aliases = {}
    for k, c in enumerate(carry):
        aliases[len(args)] = k
        in_specs.append(pl.BlockSpec(memory_space=pl.ANY))
        args.append(c)
    if emit_w16:
        out_specs += [wcol, wcol, wrow]
        out_shape += [jax.ShapeDtypeStruct(w.shape, BF16) for w in (wg, wu, wd)]
    return pl.pallas_call(
        functools.partial(_ffn_kernel, split=x_tail is not None, n_next=gnext is not None,
                          cast_t=tuple(t for _, _, t in casts), emit_w16=emit_w16,
                          n_carry=len(carry)),
        grid=grid, in_specs=in_specs, out_specs=out_specs, out_shape=out_shape,
        scratch_shapes=[pltpu.VMEM((TM, D_MODEL), BF16)],
        input_output_aliases=aliases,
        compiler_params=_params(("arbitrary", "arbitrary")), name=name,
    )(*args)


N_GLU_STEPS = D_CONV // TN
N_Z_STEPS = D_SSM // TN
N_XBC_STEPS = D_XBC // TN


def _inproj_kernel(u_ref, w_ref, wb_ref, wdt_ref, dtb_ref, glu_ref, z_ref, xbc_ref, dt_ref):
    j = pl.program_id(1)

    @pl.when(j < N_GLU_STEPS)
    def _():
        u = u_ref[...]
        glu_ref[...] = _dot(u, w_ref[...]) * jax.nn.sigmoid(_dot(u, wb_ref[...]))

    @pl.when(jnp.logical_and(j >= N_GLU_STEPS, j < N_GLU_STEPS + N_Z_STEPS))
    def _():
        z_ref[...] = _dot(u_ref[...], w_ref[...])

    @pl.when(j >= N_GLU_STEPS + N_Z_STEPS)
    def _():
        xbc_ref[...] = _dot(u_ref[...], w_ref[...])

    @pl.when(j == pl.num_programs(1) - 1)
    def _():
        x = _dot(u_ref[...], wdt_ref[...]) + dtb_ref[...]
        dt_ref[...] = jnp.maximum(x, 0.0) + jnp.log1p(jnp.exp(-jnp.abs(x)))


def _in_proj(u, w_in, w_dt, dt_bias):
    m = u.shape[0]
    assert m % TMP == 0
    g, nz, nx = N_GLU_STEPS, N_Z_STEPS, N_XBC_STEPS
    return pl.pallas_call(
        _inproj_kernel, grid=(m // TMP, g + nz + nx),
        in_specs=[pl.BlockSpec((TMP, D_MODEL), lambda i, j: (i, 0)),
                  pl.BlockSpec((D_MODEL, TN), lambda i, j: (0, jnp.where(j < g, j, j + g))),
                  pl.BlockSpec((D_MODEL, TN), lambda i, j: (0, g + jnp.minimum(j, g - 1))),
                  pl.BlockSpec((D_MODEL, HEAD_PAD), lambda i, j: (0, 0)),
                  pl.BlockSpec((1, HEAD_PAD), lambda i, j: (0, 0))],
        out_specs=[pl.BlockSpec((TMP, TN), lambda i, j: (i, jnp.minimum(j, g - 1))),
                   pl.BlockSpec((TMP, TN), lambda i, j: (i, jnp.clip(j - g, 0, nz - 1))),
                   pl.BlockSpec((TMP, TN), lambda i, j: (i, jnp.clip(j - g - nz, 0, nx - 1))),
                   pl.BlockSpec((TMP, HEAD_PAD), lambda i, j: (i, 0))],
        out_shape=[jax.ShapeDtypeStruct((m, D_CONV), F32), jax.ShapeDtypeStruct((m, D_SSM), F32),
                   jax.ShapeDtypeStruct((m, D_XBC), F32), jax.ShapeDtypeStruct((m, HEAD_PAD), F32)],
        compiler_params=_params(("arbitrary", "arbitrary")), name="in_proj",
    )(u, w_in, w_in, w_dt, dt_bias)


def _ln_swish(y, g, b):
    mu = jnp.mean(y, axis=-1, keepdims=True)
    yc = y - mu
    yn = yc * lax.rsqrt(jnp.mean(yc * yc, axis=-1, keepdims=True) + EPS) * g + b
    return _silu(yn)


def _pconv_kernel(v_ref, w_ref, b_ref, lg_ref, lb_ref, o_ref, tail_ref, xpad_scr, conv_scr, shift_scr):
    t = pl.program_id(1)

    @pl.when(t == 0)
    def _():
        xpad_scr[0:CARRY, :] = jnp.zeros((CARRY, D_CONV), F32)

    xpad_scr[CARRY:CARRY + TL, :] = v_ref[...]
    first = CARRY - (CONV_W - 1)
    hb = CONV_ROWS
    bases = range(0, TL, hb)
    for cb in range(D_CONV // 128):
        lanes = slice(cb * 128, (cb + 1) * 128)
        for phase in range(1, 8):
            shift_scr[phase - 1] = xpad_scr[phase:phase + CONV_SPAN, lanes]
        acc = [jnp.broadcast_to(b_ref[:, lanes], (hb, 128)) for _ in bases]
        for k in range(CONV_W):
            phase, off = (first + k) % 8, 8 * ((first + k) // 8)
            w = w_ref[k:k + 1, lanes]
            for n, base in enumerate(bases):
                if phase:
                    x = shift_scr[phase - 1, base + off:base + off + hb, :]
                else:
                    x = xpad_scr[base + off:base + off + hb, lanes]
                acc[n] = acc[n] + w * x
        for n, base in enumerate(bases):
            conv_scr[base:base + hb, lanes] = acc[n]
    xpad_scr[0:CARRY, :] = xpad_scr[TL:TL + CARRY, :]
    o_ref[...] = _ln_swish(conv_scr[...], lg_ref[...], lb_ref[...]).astype(BF16)

    @pl.when(t == pl.num_programs(1) - 1)
    def _():
        tail_ref[0] = xpad_scr[0:CARRY, :]


def _prompt_conv(v, m, n_batch, seq, w, b, lg, lb):
    vec = pl.BlockSpec((1, D_CONV), lambda bi, t: (0, 0))
    steps = seq // TL
    return pl.pallas_call(
        _pconv_kernel, grid=(n_batch, steps),
        in_specs=[pl.BlockSpec((TL, D_CONV), lambda bi, t: (bi * steps + t, 0)),
                  pl.BlockSpec((CONV_W, D_CONV), lambda bi, t: (0, 0)), vec, vec, vec],
        out_specs=[pl.BlockSpec((TL, D_CONV), lambda bi, t: (bi * steps + t, 0)),
                   pl.BlockSpec((1, CARRY, D_CONV), lambda bi, t: (bi, 0, 0))],
        out_shape=[jax.ShapeDtypeStruct((m, D_CONV), BF16),
                   jax.ShapeDtypeStruct((n_batch, CARRY, D_CONV), F32)],
        scratch_shapes=[pltpu.VMEM((CARRY + TL, D_CONV), F32), pltpu.VMEM((TL, D_CONV), F32),
                        pltpu.VMEM((7, CONV_SPAN, 128), F32)],
        compiler_params=_params(("arbitrary", "arbitrary")), name="prompt_conv")(v, w, b, lg, lb)


def _gated_norm(y, z, g):
    yg = y * _silu(z)
    return yg * lax.rsqrt(jnp.mean(yg * yg, axis=-1, keepdims=True) + EPS) * g


def _pssd_kernel(xbc_ref, z_ref, dt_ref, cw_ref, cb_ref, a_ref, expand_ref, dexp_ref, ng_ref,
                 y_ref, hfin_ref, tail_ref, state_scr, xpad_scr, xc_scr):
    c = pl.program_id(1)
    q = CHUNK

    @pl.when(c == 0)
    def _():
        state_scr[...] = jnp.zeros_like(state_scr)
        xpad_scr[0:XCARRY, :] = jnp.zeros((XCARRY, D_XBC), F32)

    xpad_scr[XCARRY:XCARRY + q, :] = xbc_ref[...]
    for cb in range(D_XBC // 512):
        lanes = slice(cb * 512, (cb + 1) * 512)
        acc = jnp.broadcast_to(cb_ref[:, lanes], (q, 512))
        for j in range(SSM_CONV_W):
            k = SSM_CONV_W - 1 - j
            acc = acc + cw_ref[k:k + 1, lanes] * xpad_scr[XCARRY - j:XCARRY - j + q, lanes]
        xc_scr[:, lanes] = _silu(acc)
    xpad_scr[0:XCARRY, :] = xpad_scr[q:q + XCARRY, :]

    row_i = lax.broadcasted_iota(jnp.int32, (q, q), 0)
    col_i = lax.broadcasted_iota(jnp.int32, (q, q), 1)
    tril = row_i >= col_i
    tri = jnp.where(tril, 1.0, 0.0).astype(BF16)
    expand = expand_ref[...]

    dt = dt_ref[...]
    a_cs = _dot3_rhs(tri, dt * a_ref[...])
    a_cs_t = a_cs.T
    dt_exp = _dot3_lhs(dt, expand)
    acs_exp = _dot3_lhs(a_cs, expand)
    last = acs_exp[q - 1:q, :]
    lane_lo = lax.broadcasted_iota(jnp.int32, (q, 128), 1) < HEAD_DIM

    for g in range(N_GROUPS):
        ch = slice(g * GROUP_W, (g + 1) * GROUP_W)
        xs = xc_scr[:, ch]
        bg = xc_scr[:, D_SSM + g * D_STATE:D_SSM + (g + 1) * D_STATE]
        cg = xc_scr[:, D_SSM + (N_GROUPS + g) * D_STATE:D_SSM + (N_GROUPS + g + 1) * D_STATE]
        bg16 = bg.astype(BF16)
        cg16 = cg.astype(BF16)
        xdt = xs * dt_exp[:, ch]
        acs_g = acs_exp[:, ch]
        cb = lax.dot_general(cg16, bg16, (((1,), (1,)), ((), ())), preferred_element_type=F32)
        st = state_scr[:, ch]
        y = _dot(cg16, st.astype(BF16)) * jnp.exp(acs_g)
        pieces = []
        for pr in range(HEADS_PER_GROUP // 2):
            xpair = xdt[:, pr * 128:(pr + 1) * 128]
            acc = None
            for half in range(2):
                h = g * HEADS_PER_GROUP + 2 * pr + half
                seg = a_cs[:, h:h + 1] - a_cs_t[h:h + 1, :]
                decay = jnp.exp(jnp.where(tril, seg, NEG_BIG))
                m = (cb * decay).astype(BF16)
                keep = lane_lo if half == 0 else jnp.logical_not(lane_lo)
                term = _dot(m, jnp.where(keep, xpair, 0.0).astype(BF16))
                acc = term if acc is None else acc + term
            pieces.append(acc)
        y = y + jnp.concatenate(pieces, axis=1) + dexp_ref[:, ch] * xs
        xdec = (xdt * jnp.exp(last[:, ch] - acs_g)).astype(BF16)
        s_new = lax.dot_general(bg16, xdec, (((0,), (0,)), ((), ())), preferred_element_type=F32)
        state_scr[:, ch] = st * jnp.exp(last[:, ch]) + s_new
        y_ref[:, ch] = _gated_norm(y, z_ref[:, ch], ng_ref[:, ch]).astype(BF16)

    @pl.when(c == pl.num_programs(1) - 1)
    def _():
        hfin_ref[0] = state_scr[...].T
        tail_ref[0] = xpad_scr[0:XCARRY, :]


def _prompt_ssd(xbc, z, dt, m, n_batch, seq, cw, cb, a_pad, d_exp, ng):
    nc = seq // CHUNK

    def rows(w):
        return pl.BlockSpec((CHUNK, w), lambda bi, c: (bi * nc + c, 0))

    def vec(w, r=1):
        return pl.BlockSpec((r, w), lambda bi, c: (0, 0))

    return pl.pallas_call(
        _pssd_kernel, grid=(n_batch, nc),
        in_specs=[rows(D_XBC), rows(D_SSM), rows(HEAD_PAD), vec(D_XBC, SSM_CONV_W), vec(D_XBC),
                  vec(HEAD_PAD), vec(D_SSM, HEAD_PAD), vec(D_SSM), vec(D_SSM)],
        out_specs=[rows(D_SSM), pl.BlockSpec((1, D_SSM, D_STATE), lambda bi, c: (bi, 0, 0)),
                   pl.BlockSpec((1, XCARRY, D_XBC), lambda bi, c: (bi, 0, 0))],
        out_shape=[jax.ShapeDtypeStruct((m, D_SSM), BF16),
                   jax.ShapeDtypeStruct((n_batch, D_SSM, D_STATE), F32),
                   jax.ShapeDtypeStruct((n_batch, XCARRY, D_XBC), F32)],
        scratch_shapes=[pltpu.VMEM((D_STATE, D_SSM), F32), pltpu.VMEM((XCARRY + CHUNK, D_XBC), F32),
                        pltpu.VMEM((CHUNK, D_XBC), F32)],
        compiler_params=_params(("arbitrary", "arbitrary")), name="prompt_ssd",
    )(xbc, z, dt, cw, cb, a_pad, _head_expand_matrix(), d_exp, ng)


def _sprep_kernel(v_ref, cst_ref, w_ref, b_ref, lg_ref, lb_ref,
                  xbc_ref, xst_ref, cw_ref, cb_ref, dt_ref, a_ref, dexp_ref, cmix_in_ref,
                  co_ref, ncst_ref, nxst_ref, xdt_ref, bc_ref, dec_ref, skip_ref):
    del cmix_in_ref
    nb = v_ref.shape[0]
    v = v_ref[...]
    acc = jnp.broadcast_to(b_ref[...], (nb, D_CONV)) + w_ref[CONV_W - 1:CONV_W, :] * v
    for k in range(CONV_W - 1):
        acc = acc + w_ref[k:k + 1, :] * cst_ref[k]
    co_ref[...] = _ln_swish(acc, lg_ref[...], lb_ref[...]).astype(BF16)
    for k in range(CONV_W - 2):
        ncst_ref[k] = cst_ref[k + 1]
    ncst_ref[CONV_W - 2] = v

    xn = xbc_ref[...]
    acc = jnp.broadcast_to(cb_ref[...], (nb, D_XBC)) + cw_ref[SSM_CONV_W - 1:SSM_CONV_W, :] * xn
    for k in range(SSM_CONV_W - 1):
        acc = acc + cw_ref[k:k + 1, :] * xst_ref[k]
    xc = _silu(acc)
    for k in range(SSM_CONV_W - 2):
        nxst_ref[k] = xst_ref[k + 1]
    nxst_ref[SSM_CONV_W - 2] = xn

    xs = xc[:, :D_SSM]
    bc_ref[...] = xc[:, D_SSM:]
    dt = dt_ref[...]
    dec_ref[...] = jnp.exp(dt * a_ref[...])
    dt_exp = _dot3_lhs(dt, _head_expand_matrix())
    xdt_ref[...] = xs * dt_exp
    skip_ref[...] = dexp_ref[...] * xs


def _sssd_kernel(dec_ref, h0_ref, xdt_ref, bc_ref, skip_ref, z_ref, ng_ref, ymix_in_ref,
                 hn_ref, y_ref, yt_scr, xdt_t_scr):
    del ymix_in_ref
    step = pl.program_id(0)
    nb = xdt_ref.shape[0]

    @pl.when(step == 0)
    def _():
        yt_scr[...] = jnp.zeros_like(yt_scr)
        xdt_t_scr[...] = xdt_ref[...].T.astype(BF16)

    seq_i = lax.broadcasted_iota(jnp.int32, (nb, D_STATE), 0)
    for bb in range(SSD_BLOCK):
        b = step * SSD_BLOCK + bb
        brow = bc_ref[pl.ds(b, 1), :]
        is_b = seq_i == b
        for g in range(N_GROUPS):
            rows = slice(g * GROUP_W, (g + 1) * GROUP_W)
            b_sel = jnp.where(is_b, brow[:, g * D_STATE:(g + 1) * D_STATE], 0.0).astype(BF16)
            c_sel = jnp.where(is_b, brow[:, (N_GROUPS + g) * D_STATE:(N_GROUPS + g + 1) * D_STATE],
                              0.0).astype(BF16)
            s_new = _dot(xdt_t_scr[rows, :], b_sel)
            parts = []
            for r in range(HEADS_PER_GROUP):
                h = g * HEADS_PER_GROUP + r
                hr = slice(h * HEAD_DIM, (h + 1) * HEAD_DIM)
                parts.append(h0_ref[bb, hr, :] * dec_ref[b * N_HEADS + h]
                             + s_new[r * HEAD_DIM:(r + 1) * HEAD_DIM, :])
            h_new = jnp.concatenate(parts, axis=0)
            hn_ref[bb, rows, :] = h_new
            yt_scr[rows, :] += lax.dot_general(h_new.astype(BF16), c_sel, (((1,), (1,)), ((), ())),
                                               preferred_element_type=F32)

    @pl.when(step == pl.num_programs(0) - 1)
    def _():
        y = yt_scr[...].T + skip_ref[...]
        for g in range(N_GROUPS):
            ch = slice(g * GROUP_W, (g + 1) * GROUP_W)
            y_ref[:, ch] = _gated_norm(y[:, ch], z_ref[:, ch], ng_ref[:, ch]).astype(BF16)


def _sample_mixer(glu, z, xbc, dt, cmix, ymix, n_prompt, n_samp, cst, xst, h0,
                  mw, mb, lg, lb, cw, cb, a_pad, d_exp, ng):
    sb = SAMPLE_BLOCK
    off = n_prompt // sb
    srow = lambda w: pl.BlockSpec((sb, w), lambda i: (off + i, 0))
    orow = lambda w: pl.BlockSpec((sb, w), lambda i: (i, 0))
    taps = lambda k, w: pl.BlockSpec((k, sb, w), lambda i: (0, i, 0))
    const = lambda *s: pl.BlockSpec(s, lambda i: (0,) * len(s))
    anyspec = pl.BlockSpec(memory_space=pl.ANY)
    cst_t = jnp.swapaxes(cst, 0, 1)
    xst_t = jnp.swapaxes(xst, 0, 1)
    cmix, ncst_t, nxst_t, xdt, bc, dec, skip = pl.pallas_call(
        _sprep_kernel, grid=(n_samp // sb,),
        in_specs=[srow(D_CONV), taps(CONV_W - 1, D_CONV), const(CONV_W, D_CONV), const(1, D_CONV),
                  const(1, D_CONV), const(1, D_CONV),
                  srow(D_XBC), taps(SSM_CONV_W - 1, D_XBC), const(SSM_CONV_W, D_XBC),
                  const(1, D_XBC), srow(HEAD_PAD), const(1, HEAD_PAD), const(1, D_SSM), anyspec],
        out_specs=[srow(D_CONV), taps(CONV_W - 1, D_CONV), taps(SSM_CONV_W - 1, D_XBC),
                   orow(D_SSM), orow(D_XBC - D_SSM), orow(HEAD_PAD), orow(D_SSM)],
        out_shape=[jax.ShapeDtypeStruct(cmix.shape, BF16),
                   jax.ShapeDtypeStruct((CONV_W - 1, n_samp, D_CONV), F32),
                   jax.ShapeDtypeStruct((SSM_CONV_W - 1, n_samp, D_XBC), F32),
                   jax.ShapeDtypeStruct((n_samp, D_SSM), F32),
                   jax.ShapeDtypeStruct((n_samp, D_XBC - D_SSM), F32),
                   jax.ShapeDtypeStruct((n_samp, HEAD_PAD), F32),
                   jax.ShapeDtypeStruct((n_samp, D_SSM), F32)],
        input_output_aliases={13: 0},
        compiler_params=_params(("arbitrary",)), name="sample_prep",
    )(glu, cst_t, mw, mb, lg, lb, xbc, xst_t, cw, cb, dt, a_pad, d_exp, cmix)

    dec_flat = dec[:, :N_HEADS].reshape(-1)
    sblock = n_prompt // n_samp
    assert n_samp % SSD_BLOCK == 0
    hn, ymix = pl.pallas_call(
        _sssd_kernel, grid=(n_samp // SSD_BLOCK,),
        in_specs=[pl.BlockSpec(memory_space=pltpu.SMEM),
                  pl.BlockSpec((SSD_BLOCK, D_SSM, D_STATE), lambda b: (b, 0, 0)),
                  const(n_samp, D_SSM), const(n_samp, D_XBC - D_SSM), const(n_samp, D_SSM),
                  pl.BlockSpec((n_samp, D_SSM), lambda b: (sblock, 0)), const(1, D_SSM), anyspec],
        out_specs=[pl.BlockSpec((SSD_BLOCK, D_SSM, D_STATE), lambda b: (b, 0, 0)),
                   pl.BlockSpec((n_samp, D_SSM), lambda b: (sblock, 0))],
        out_shape=[jax.ShapeDtypeStruct((n_samp, D_SSM, D_STATE), F32),
                   jax.ShapeDtypeStruct(ymix.shape, BF16)],
        scratch_shapes=[pltpu.VMEM((D_SSM, n_samp), F32), pltpu.VMEM((D_SSM, n_samp), BF16)],
        input_output_aliases={7: 1},
        compiler_params=_params(("arbitrary",)), name="sample_ssd",
    )(dec_flat, h0.reshape(n_samp, D_SSM, D_STATE), xdt, bc, skip, z, ng, ymix)
    return cmix, ymix, jnp.swapaxes(ncst_t, 0, 1), jnp.swapaxes(nxst_t, 0, 1), hn


def _outproj_kernel(c_ref, y_ref, w_ref, h_ref, g_ref, o_ref):
    k = pl.program_id(1)

    @pl.when(k == 0)
    def _():
        o_ref[...] = _dot(c_ref[...], w_ref[...])

    @pl.when(k > 0)
    def _():
        o_ref[...] += _dot(y_ref[...], w_ref[...])

    @pl.when(k == pl.num_programs(1) - 1)
    def _():
        o_ref[...] = h_ref[...] + _rms(o_ref[...], g_ref[...])


def _out_proj(cmix, ymix, w_out, h, g):
    m = h.shape[0]
    kt = D_CONV
    nk = (D_CONV + D_SSM) // kt
    row = pl.BlockSpec((TM, D_MODEL), lambda i, k: (i, 0))
    return pl.pallas_call(
        _outproj_kernel, grid=(m // TM, nk),
        in_specs=[pl.BlockSpec((TM, kt), lambda i, k: (i, 0)),
                  pl.BlockSpec((TM, kt), lambda i, k: (i, jnp.maximum(k - 1, 0))),
                  pl.BlockSpec((kt, D_MODEL), lambda i, k: (k, 0)),
                  row, pl.BlockSpec((1, D_MODEL), lambda i, k: (0, 0))],
        out_specs=row, out_shape=jax.ShapeDtypeStruct((m, D_MODEL), F32),
        compiler_params=_params(("arbitrary", "arbitrary")), name="out_proj")(cmix, ymix, w_out, h, g)


def _ple_kernel(h_ref, pa_ref, pb_ref, gpre_ref, wg_ref, wp_ref, gpost_ref, oa_ref, ob_ref, emb_scr):
    i = pl.program_id(0)

    def embed(rows, src, srows):
        emb_scr[rows, :] = _dot(src[srows, :].astype(BF16), wp_ref[...])
    _on_tile_rows(i, pa_ref, pb_ref, embed)

    h = h_ref[...]
    gate = jax.nn.sigmoid(_dot(_rms(h, gpre_ref[...]).astype(BF16), wg_ref[...]))
    oa_ref[...] = h + _rms(gate * emb_scr[...], gpost_ref[...])

    @pl.when(i == pl.num_programs(0) - 1)
    def _():
        nb = ob_ref.shape[0]
        ob_ref[...] = oa_ref[TM - nb:TM, :]


def _ple(h, p_prompt, p_samp, gpre, wg, wp, gpost):
    m = h.shape[0]
    n_prompt, n_samp = p_prompt.shape[0], p_samp.shape[0]
    row = pl.BlockSpec((TM, D_MODEL), lambda i: (i, 0))
    vec = pl.BlockSpec((1, D_MODEL), lambda i: (0, 0))
    return pl.pallas_call(
        _ple_kernel, grid=(m // TM,),
        in_specs=[row, pl.BlockSpec((TM, PLE_DIM), lambda i: (i, 0)),
                  pl.BlockSpec((n_samp, PLE_DIM), lambda i: (0, 0)), vec,
                  pl.BlockSpec((D_MODEL, D_MODEL), lambda i: (0, 0)),
                  pl.BlockSpec((PLE_DIM, D_MODEL), lambda i: (0, 0)), vec],
        out_specs=[row, pl.BlockSpec((n_samp, D_MODEL), lambda i: (0, 0))],
        out_shape=[jax.ShapeDtypeStruct((n_prompt, D_MODEL), F32),
                   jax.ShapeDtypeStruct((n_samp, D_MODEL), F32)],
        scratch_shapes=[pltpu.VMEM((TM, D_MODEL), F32)],
        compiler_params=_params(("arbitrary",)), name="ple")(h, p_prompt, p_samp, gpre, wg, wp, gpost)


def _layer(x_prompt, x_samp, p_prompt, p_samp, n_batch, seq, cst, xst, h0, lw):
    (norm_ffn1_pre, w_ffn1_gate, w_ffn1_up, w_ffn1_down, norm_ffn1_post,
     norm_mix_pre, w_in, conv_mod_w, conv_mod_b, conv_mod_ln_g, conv_mod_ln_b,
     ssm_conv_w, ssm_conv_b, dt_bias, a_log, d_skip, ssm_norm_g, w_out, norm_mix_post,
     norm_ffn2_pre, w_ffn2_gate, w_ffn2_up, w_ffn2_down, norm_ffn2_post,
     norm_ple_pre, w_ple_gate, w_ple_proj, norm_ple_post) = lw
    n_prompt, n_samp = x_prompt.shape[0], x_samp.shape[0]
    m = n_prompt + n_samp
    _check_split(n_prompt, n_samp)
    row2 = lambda t: t.reshape(1, -1)

    later = ((jnp.swapaxes(w_in, 0, 1), 128, True), (w_out, 32, False), (w_ffn2_gate, 16, False),
             (w_ffn2_up, 16, False), (w_ffn2_down, 64, False), (w_ple_gate, 16, False),
             (w_ple_proj, 16, False))
    ffn1 = (row2(norm_ffn1_pre), row2(norm_ffn1_post), row2(norm_mix_pre))
    h1, u, wg1, wu1, wd1 = _ffn(
        x_prompt, ffn1[0], w_ffn1_gate, w_ffn1_up, w_ffn1_down, ffn1[1], gnext=ffn1[2],
        m=m, tiles=(0, 1), emit_w16=True, tf=TF_FIRST, name="ffn_first")
    h1, u, w_in16, w_out16, wg2, wu2, wd2, wpg, wpp = _ffn(
        x_prompt, ffn1[0], wg1, wu1, wd1, ffn1[1], gnext=ffn1[2], x_tail=x_samp, casts=later,
        m=m, tiles=(1, m // TM - 1), carry=(h1, u), name="ffn_rest")

    d_proj = w_in.shape[1]
    w_dt = jnp.pad(w_in16[:, d_proj - N_HEADS:], ((0, 0), (0, HEAD_PAD - N_HEADS)))
    pad_h = lambda t: jnp.pad(t.astype(F32), (0, HEAD_PAD - N_HEADS)).reshape(1, HEAD_PAD)
    glu, z, xbc, dt = _in_proj(u, w_in16, w_dt, pad_h(dt_bias))

    a = -jnp.exp(a_log.astype(F32))
    a_pad = pad_h(a)
    d_exp = row2(jnp.repeat(d_skip.astype(F32), HEAD_DIM))
    cw, cb = ssm_conv_w, row2(ssm_conv_b)
    mw, mb, lg, lb = conv_mod_w, row2(conv_mod_b), row2(conv_mod_ln_g), row2(conv_mod_ln_b)
    ng = row2(ssm_norm_g)

    cmix, glu_tail = _prompt_conv(glu, m, n_batch, seq, mw, mb, lg, lb)
    ymix, hfin_prompt, xbc_tail = _prompt_ssd(xbc, z, dt, m, n_batch, seq, cw, cb, a_pad, d_exp, ng)
    cmix, ymix, new_cst_samp, new_xst_samp, hn = _sample_mixer(
        glu, z, xbc, dt, cmix, ymix, n_prompt, n_samp, cst, xst, h0,
        mw, mb, lg, lb, cw, cb, a_pad, d_exp, ng)

    h2 = _out_proj(cmix, ymix, w_out16, h1, row2(norm_mix_post))
    (h3,) = _ffn(h2, row2(norm_ffn2_pre), wg2, wu2, wd2, row2(norm_ffn2_post))
    y_prompt, y_samp = _ple(h3, p_prompt, p_samp, row2(norm_ple_pre), wpg, wpp, row2(norm_ple_post))

    new_cst_prompt = glu_tail[:, CARRY - (CONV_W - 1):]
    new_xst_prompt = xbc_tail[:, XCARRY - (SSM_CONV_W - 1):]
    new_h_prompt = hfin_prompt.reshape(n_batch, N_HEADS, HEAD_DIM, D_STATE)
    new_h_samp = hn.reshape(n_samp, N_HEADS, HEAD_DIM, D_STATE)
    return (y_prompt, y_samp, new_cst_prompt, new_xst_prompt, new_h_prompt,
            new_cst_samp, new_xst_samp, new_h_samp)


def kernel(x_prompt, x_sample, state_conv_mod, state_ssm_conv, state_ssm, p_prompt, p_sample,
           norm_ffn1_pre, w_ffn1_gate, w_ffn1_up, w_ffn1_down, norm_ffn1_post,
           norm_mix_pre, w_in, conv_mod_w, conv_mod_b, conv_mod_ln_g, conv_mod_ln_b,
           ssm_conv_w, ssm_conv_b, dt_bias, a_log, d_skip, ssm_norm_g, w_out, norm_mix_post,
           norm_ffn2_pre, w_ffn2_gate, w_ffn2_up, w_ffn2_down, norm_ffn2_post,
           norm_ple_pre, w_ple_gate, w_ple_proj, norm_ple_post):
    weights = (norm_ffn1_pre, w_ffn1_gate, w_ffn1_up, w_ffn1_down, norm_ffn1_post,
               norm_mix_pre, w_in, conv_mod_w, conv_mod_b, conv_mod_ln_g, conv_mod_ln_b,
               ssm_conv_w, ssm_conv_b, dt_bias, a_log, d_skip, ssm_norm_g, w_out, norm_mix_post,
               norm_ffn2_pre, w_ffn2_gate, w_ffn2_up, w_ffn2_down, norm_ffn2_post,
               norm_ple_pre, w_ple_gate, w_ple_proj, norm_ple_post)
    n_batch, seq, _ = x_prompt.shape
    n_samp = x_sample.shape[0]
    n_prompt = n_batch * seq
    depth = norm_ffn1_pre.shape[0]
    xp = x_prompt.reshape(n_prompt, D_MODEL)
    xs = x_sample.reshape(n_samp, D_MODEL)
    outs = [[] for _ in range(6)]
    for i in range(depth):
        res = _layer(xp, xs, p_prompt[i].reshape(n_prompt, PLE_DIM), p_sample[i].reshape(n_samp, PLE_DIM),
                     n_batch, seq, state_conv_mod[i], state_ssm_conv[i], state_ssm[i],
                     tuple(w[i] for w in weights))
        xp, xs = res[0], res[1]
        for lst, r in zip(outs, res[2:]):
            lst.append(r)
    return ((xp.reshape(n_batch, seq, D_MODEL), xs.reshape(n_samp, 1, D_MODEL))
            + tuple(jnp.stack(lst, axis=0) for lst in outs))
```

```python
import functools

import jax
import jax.numpy as jnp
from jax import lax
from jax.experimental import pallas as pl
from jax.experimental.pallas import tpu as pltpu

F32 = jnp.float32
BF16 = jnp.bfloat16

D_MODEL = 2048
D_FF = 5632
D_CONV = 1024
D_SSM = 3072
N_HEADS = 48
HEAD_DIM = 64
N_GROUPS = 8
HEADS_PER_GROUP = 6
GROUP_W = HEADS_PER_GROUP * HEAD_DIM
D_STATE = 128
D_XBC = D_SSM + 2 * N_GROUPS * D_STATE
CONV_W = 31
SSM_CONV_W = 4
CHUNK = 128
PLE_DIM = 256
EPS = 1e-6
NEG_BIG = -1e30
HEAD_PAD = 128

TM = 640
TF = 512
TF_FIRST = 256
TMP = 1040
TN = 512
TL = 256
CARRY = 32
CONV_ROWS = 64
NORM_ROWS = 16
CONV_SPAN = TL + CARRY - 8
XCARRY = 8
SAMPLE_BLOCK = 32
SSD_BLOCK = 4
VMEM_LIMIT = 56 * 1024 * 1024


def _params(dims, vmem=VMEM_LIMIT):
    return pltpu.CompilerParams(dimension_semantics=dims, vmem_limit_bytes=vmem)


def _rms(x, g):
    return x * lax.rsqrt(jnp.mean(x * x, axis=-1, keepdims=True) + EPS) * g


def _silu(x):
    return x * jax.nn.sigmoid(x)


def _dot(a, b):
    return jnp.dot(a, b, preferred_element_type=F32)


def _split3(x):
    hi = x.astype(BF16)
    r = x - hi.astype(F32)
    mid = r.astype(BF16)
    lo = (r - mid.astype(F32)).astype(BF16)
    return hi, mid, lo


def _dot3_rhs(a3_bf16, x):
    return _dot(a3_bf16, jnp.concatenate(_split3(x), axis=0))


def _dot3_lhs(x, b3_bf16):
    return _dot(jnp.concatenate(_split3(x), axis=1), b3_bf16)


def _head_expand_matrix():
    head = lax.broadcasted_iota(jnp.int32, (3 * HEAD_PAD, D_SSM), 0) & (HEAD_PAD - 1)
    chan = lax.broadcasted_iota(jnp.int32, (3 * HEAD_PAD, D_SSM), 1)
    return jnp.where((chan >> 6) == head, 1.0, 0.0).astype(BF16)


def _on_tile_rows(i, a_ref, b_ref, fn):
    if b_ref is None:
        fn(slice(0, TM), a_ref, slice(0, TM))
        return
    nb = b_ref.shape[0]
    na = TM - nb
    last = pl.num_programs(0) - 1

    @pl.when(i != last)
    def _():
        fn(slice(0, TM), a_ref, slice(0, TM))

    @pl.when(i == last)
    def _():
        fn(slice(0, na), a_ref, slice(0, na))
        fn(slice(na, TM), b_ref, slice(0, nb))


def _check_split(n_prompt, n_samp):
    assert (n_prompt + n_samp) % TM == 0 and n_samp < TM and n_samp % 16 == 0


def _ffn_kernel(*refs, split, n_next, cast_t, emit_w16, n_carry):
    n_cast = len(cast_t)
    refs = list(refs)
    xa_ref = refs.pop(0)
    xb_ref = refs.pop(0) if split else None
    gpre_ref, wg_ref, wu_ref, wd_ref, gpost_ref = refs[:5]
    refs = refs[5:]
    gnext_ref = refs.pop(0) if n_next else None
    cast_in, refs = refs[:n_cast], refs[n_cast:]
    refs = refs[n_carry:]
    o_ref = refs.pop(0)
    unext_ref = refs.pop(0) if n_next else None
    cast_out, refs = refs[:n_cast], refs[n_cast:]
    w16_refs, refs = (refs[:3], refs[3:]) if emit_w16 else ((), refs)
    (u_scr,) = refs
    i = pl.program_id(0)
    j = pl.program_id(1)

    @pl.when(j == 0)
    def _():
        def pre(rows, src, srows):
            u_scr[rows, :] = _rms(src[srows, :], gpre_ref[...]).astype(BF16)
        _on_tile_rows(i, xa_ref, xb_ref, pre)
        o_ref[...] = jnp.zeros_like(o_ref)

    for ci, co, transpose in zip(cast_in, cast_out, cast_t):
        co[...] = (ci[...].T if transpose else ci[...]).astype(BF16)

    wg, wu, wd = wg_ref[...], wu_ref[...], wd_ref[...]
    if emit_w16:
        wg, wu, wd = wg.astype(BF16), wu.astype(BF16), wd.astype(BF16)
        for ref, w in zip(w16_refs, (wg, wu, wd)):
            ref[...] = w
    u = u_scr[...]
    act = (_silu(_dot(u, wg)) * _dot(u, wu)).astype(BF16)
    o_ref[...] += _dot(act, wd)

    @pl.when(j == pl.num_programs(1) - 1)
    def _():
        def post(rows, src, srows):
            step = rows.stop - rows.start if n_next else NORM_ROWS
            for r in range(0, rows.stop - rows.start, step):
                dst = slice(rows.start + r, rows.start + r + step)
                h = src[srows.start + r:srows.start + r + step, :] + _rms(o_ref[dst, :], gpost_ref[...])
                o_ref[dst, :] = h
                if n_next:
                    unext_ref[dst, :] = _rms(h, gnext_ref[...]).astype(BF16)
        _on_tile_rows(i, xa_ref, xb_ref, post)


def _ffn(x, gpre, wg, wu, wd, gpost, gnext=None, x_tail=None, casts=(), m=None, tiles=None,
         carry=(), emit_w16=False, tf=TF, name="ffn"):
    if m is None:
        m = x.shape[0] + (0 if x_tail is None else x_tail.shape[0])
    first, count = tiles if tiles is not None else (0, m // TM)
    grid = (count, D_FF // tf)
    steps = grid[0] * grid[1]
    row = pl.BlockSpec((TM, D_MODEL), lambda i, j: (i + first, 0))
    vec = pl.BlockSpec((1, D_MODEL), lambda i, j: (0, 0))
    wcol = pl.BlockSpec((D_MODEL, tf), lambda i, j: (0, j))
    wrow = pl.BlockSpec((tf, D_MODEL), lambda i, j: (j, 0))
    in_specs, args = [row], [x]
    if x_tail is not None:
        in_specs.append(pl.BlockSpec(x_tail.shape, lambda i, j: (0, 0)))
        args.append(x_tail)
    in_specs += [vec, wcol, wcol, wrow, vec]
    args += [gpre, wg, wu, wd, 0.5 * gpost]
    out_shape = [jax.ShapeDtypeStruct((m, D_MODEL), F32)]
    out_specs = [row]
    if gnext is not None:
        in_specs.append(vec)
        args.append(gnext)
        out_shape.append(jax.ShapeDtypeStruct((m, D_MODEL), BF16))
        out_specs.append(row)
    assert len(carry) in (0, len(out_shape))
    for w, r, transpose in casts:
        nblk = pl.cdiv(w.shape[0], r)
        assert r % 16 == 0 and nblk <= steps and (w.shape[0] % r == 0 or transpose)
        slab = lambda i, j, nblk=nblk: jnp.minimum(i * grid[1] + j, nblk - 1)
        in_specs.append(pl.BlockSpec((r, w.shape[1]), lambda i, j, slab=slab: (slab(i, j), 0)))
        args.append(w)
        if transpose:
            assert r % 128 == 0
            out_specs.append(pl.BlockSpec((w.shape[1], r), lambda i, j, slab=slab: (0, slab(i, j))))
            out_shape.append(jax.ShapeDtypeStruct(w.shape[::-1], BF16))
        else:
            out_specs.append(pl.BlockSpec((r, w.shape[1]), lambda i, j, slab=slab: (slab(i, j), 0)))
            out_shape.append(jax.ShapeDtypeStruct(w.shape, BF16))
    aliases = {}
    for k, c in enumerate(carry):
        aliases[len(args)] = k
        in_specs.append(pl.BlockSpec(memory_space=pl.ANY))
        args.append(c)
    if emit_w16:
        out_specs += [wcol, wcol, wrow]
        out_shape += [jax.ShapeDtypeStruct(w.shape, BF16) for w in (wg, wu, wd)]
    return pl.pallas_call(
        functools.partial(_ffn_kernel, split=x_tail is not None, n_next=gnext is not None,
                          cast_t=tuple(t for _, _, t in casts), emit_w16=emit_w16,
                          n_carry=len(carry)),
        grid=grid, in_specs=in_specs, out_specs=out_specs, out_shape=out_shape,
        scratch_shapes=[pltpu.VMEM((TM, D_MODEL), BF16)],
        input_output_aliases=aliases,
        compiler_params=_params(("arbitrary", "arbitrary")), name=name,
    )(*args)


N_GLU_STEPS = D_CONV // TN
N_Z_STEPS = D_SSM // TN
N_XBC_STEPS = D_XBC // TN


def _inproj_kernel(u_ref, w_ref, wb_ref, wdt_ref, dtb_ref, glu_ref, z_ref, xbc_ref, dt_ref):
    j = pl.program_id(1)

    @pl.when(j < N_GLU_STEPS)
    def _():
        u = u_ref[...]
        glu_ref[...] = _dot(u, w_ref[...]) * jax.nn.sigmoid(_dot(u, wb_ref[...]))

    @pl.when(jnp.logical_and(j >= N_GLU_STEPS, j < N_GLU_STEPS + N_Z_STEPS))
    def _():
        z_ref[...] = _dot(u_ref[...], w_ref[...])

    @pl.when(j >= N_GLU_STEPS + N_Z_STEPS)
    def _():
        xbc_ref[...] = _dot(u_ref[...], w_ref[...])

    @pl.when(j == pl.num_programs(1) - 1)
    def _():
        x = _dot(u_ref[...], wdt_ref[...]) + dtb_ref[...]
        dt_ref[...] = jnp.maximum(x, 0.0) + jnp.log1p(jnp.exp(-jnp.abs(x)))


def _in_proj(u, w_in, w_dt, dt_bias):
    m = u.shape[0]
    assert m % TMP == 0
    g, nz, nx = N_GLU_STEPS, N_Z_STEPS, N_XBC_STEPS
    return pl.pallas_call(
        _inproj_kernel, grid=(m // TMP, g + nz + nx),
        in_specs=[pl.BlockSpec((TMP, D_MODEL), lambda i, j: (i, 0)),
                  pl.BlockSpec((D_MODEL, TN), lambda i, j: (0, jnp.where(j < g, j, j + g))),
                  pl.BlockSpec((D_MODEL, TN), lambda i, j: (0, g + jnp.minimum(j, g - 1))),
                  pl.BlockSpec((D_MODEL, HEAD_PAD), lambda i, j: (0, 0)),
                  pl.BlockSpec((1, HEAD_PAD), lambda i, j: (0, 0))],
        out_specs=[pl.BlockSpec((TMP, TN), lambda i, j: (i, jnp.minimum(j, g - 1))),
                   pl.BlockSpec((TMP, TN), lambda i, j: (i, jnp.clip(j - g, 0, nz - 1))),
                   pl.BlockSpec((TMP, TN), lambda i, j: (i, jnp.clip(j - g - nz, 0, nx - 1))),
                   pl.BlockSpec((TMP, HEAD_PAD), lambda i, j: (i, 0))],
        out_shape=[jax.ShapeDtypeStruct((m, D_CONV), F32), jax.ShapeDtypeStruct((m, D_SSM), F32),
                   jax.ShapeDtypeStruct((m, D_XBC), F32), jax.ShapeDtypeStruct((m, HEAD_PAD), F32)],
        compiler_params=_params(("arbitrary", "arbitrary")), name="in_proj",
    )(u, w_in, w_in, w_dt, dt_bias)


def _ln_swish(y, g, b):
    mu = jnp.mean(y, axis=-1, keepdims=True)
    yc = y - mu
    yn = yc * lax.rsqrt(jnp.mean(yc * yc, axis=-1, keepdims=True) + EPS) * g + b
    return _silu(yn)


def _pconv_kernel(v_ref, w_ref, b_ref, lg_ref, lb_ref, o_ref, tail_ref, xpad_scr, conv_scr, shift_scr):
    t = pl.program_id(1)

    @pl.when(t == 0)
    def _():
        xpad_scr[0:CARRY, :] = jnp.zeros((CARRY, D_CONV), F32)

    xpad_scr[CARRY:CARRY + TL, :] = v_ref[...]
    first = CARRY - (CONV_W - 1)
    hb = CONV_ROWS
    n_rows = CARRY + TL
    for cb in range(D_CONV // 128):
        lanes = slice(cb * 128, (cb + 1) * 128)
        xfull = xpad_scr[:, lanes]
        conv_scr[:, lanes] = jnp.broadcast_to(b_ref[:, lanes], (TL, 128))
        for phase in range(8):
            xs = pltpu.roll(xfull, n_rows - phase, axis=0) if phase else xfull
            taps = [k for k in range(CONV_W) if (first + k) % 8 == phase]
            for base in range(0, TL, hb):
                acc = conv_scr[base:base + hb, lanes]
                for k in taps:
                    off = 8 * ((first + k) // 8)
                    acc = acc + w_ref[k:k + 1, lanes] * xs[base + off:base + off + hb]
                conv_scr[base:base + hb, lanes] = acc
    xpad_scr[0:CARRY, :] = xpad_scr[TL:TL + CARRY, :]
    for r in range(0, TL, NORM_ROWS):
        rows = slice(r, r + NORM_ROWS)
        o_ref[rows, :] = _ln_swish(conv_scr[rows, :], lg_ref[...], lb_ref[...]).astype(BF16)

    @pl.when(t == pl.num_programs(1) - 1)
    def _():
        tail_ref[0] = xpad_scr[0:CARRY, :]


def _prompt_conv(v, m, n_batch, seq, w, b, lg, lb):
    vec = pl.BlockSpec((1, D_CONV), lambda bi, t: (0, 0))
    steps = seq // TL
    return pl.pallas_call(
        _pconv_kernel, grid=(n_batch, steps),
        in_specs=[pl.BlockSpec((TL, D_CONV), lambda bi, t: (bi * steps + t, 0)),
                  pl.BlockSpec((CONV_W, D_CONV), lambda bi, t: (0, 0)), vec, vec, vec],
        out_specs=[pl.BlockSpec((TL, D_CONV), lambda bi, t: (bi * steps + t, 0)),
                   pl.BlockSpec((1, CARRY, D_CONV), lambda bi, t: (bi, 0, 0))],
        out_shape=[jax.ShapeDtypeStruct((m, D_CONV), BF16),
                   jax.ShapeDtypeStruct((n_batch, CARRY, D_CONV), F32)],
        scratch_shapes=[pltpu.VMEM((CARRY + TL, D_CONV), F32), pltpu.VMEM((TL, D_CONV), F32),
                        pltpu.VMEM((7, CONV_SPAN, 128), F32)],
        compiler_params=_params(("arbitrary", "arbitrary")), name="prompt_conv")(v, w, b, lg, lb)


def _gated_norm(y, z, g):
    yg = y * _silu(z)
    return yg * lax.rsqrt(jnp.mean(yg * yg, axis=-1, keepdims=True) + EPS) * g


def _pssd_kernel(xbc_ref, z_ref, dt_ref, cw_ref, cb_ref, a_ref, expand_ref, dexp_ref, ng_ref,
                 y_ref, hfin_ref, tail_ref, state_scr, xpad_scr, xc_scr):
    c = pl.program_id(1)
    q = CHUNK

    @pl.when(c == 0)
    def _():
        state_scr[...] = jnp.zeros_like(state_scr)
        xpad_scr[0:XCARRY, :] = jnp.zeros((XCARRY, D_XBC), F32)

    xpad_scr[XCARRY:XCARRY + q, :] = xbc_ref[...]
    for cb in range(D_XBC // 512):
        lanes = slice(cb * 512, (cb + 1) * 512)
        acc = jnp.broadcast_to(cb_ref[:, lanes], (q, 512))
        for j in range(SSM_CONV_W):
            k = SSM_CONV_W - 1 - j
            acc = acc + cw_ref[k:k + 1, lanes] * xpad_scr[XCARRY - j:XCARRY - j + q, lanes]
        xc_scr[:, lanes] = _silu(acc)
    xpad_scr[0:XCARRY, :] = xpad_scr[q:q + XCARRY, :]

    row_i = lax.broadcasted_iota(jnp.int32, (q, q), 0)
    col_i = lax.broadcasted_iota(jnp.int32, (q, q), 1)
    tril = row_i >= col_i
    tri = jnp.where(tril, 1.0, 0.0).astype(BF16)
    expand = expand_ref[...]

    dt = dt_ref[...]
    tri3 = jnp.concatenate([tri, tri, tri], axis=1)
    a_cs = _dot3_rhs(tri3, dt * a_ref[...])
    a_cs_t = a_cs.T
    dt_exp = _dot3_lhs(dt, expand)
    acs_exp = _dot3_lhs(a_cs, expand)
    last = acs_exp[q - 1:q, :]
    lane_lo = lax.broadcasted_iota(jnp.int32, (q, 128), 1) < HEAD_DIM

    for g in range(N_GROUPS):
        ch = slice(g * GROUP_W, (g + 1) * GROUP_W)
        xs = xc_scr[:, ch]
        bg = xc_scr[:, D_SSM + g * D_STATE:D_SSM + (g + 1) * D_STATE]
        cg = xc_scr[:, D_SSM + (N_GROUPS + g) * D_STATE:D_SSM + (N_GROUPS + g + 1) * D_STATE]
        bg16 = bg.astype(BF16)
        cg16 = cg.astype(BF16)
        xdt = xs * dt_exp[:, ch]
        acs_g = acs_exp[:, ch]
        cb = lax.dot_general(cg16, bg16, (((1,), (1,)), ((), ())), preferred_element_type=F32)
        st = state_scr[:, ch]
        y = _dot(cg16, st.astype(BF16)) * jnp.exp(acs_g)
        pieces = []
        for pr in range(HEADS_PER_GROUP // 2):
            xpair = xdt[:, pr * 128:(pr + 1) * 128].astype(BF16)
            both = []
            for half in range(2):
                h = g * HEADS_PER_GROUP + 2 * pr + half
                seg = a_cs[:, h:h + 1] - a_cs_t[h:h + 1, :]
                decay = jnp.exp(jnp.where(tril, seg, NEG_BIG))
                both.append(_dot((cb * decay).astype(BF16), xpair))
            pieces.append(jnp.where(lane_lo, both[0], both[1]))
        y = y + jnp.concatenate(pieces, axis=1) + dexp_ref[:, ch] * xs
        xdec = (xdt * jnp.exp(last[:, ch] - acs_g)).astype(BF16)
        s_new = lax.dot_general(bg16, xdec, (((0,), (0,)), ((), ())), preferred_element_type=F32)
        state_scr[:, ch] = st * jnp.exp(last[:, ch]) + s_new
        y_ref[:, ch] = _gated_norm(y, z_ref[:, ch], ng_ref[:, ch]).astype(BF16)

    @pl.when(c == pl.num_programs(1) - 1)
    def _():
        hfin_ref[0] = state_scr[...].T
        tail_ref[0] = xpad_scr[0:XCARRY, :]


def _prompt_ssd(xbc, z, dt, m, n_batch, seq, cw, cb, a_pad, d_exp, ng):
    nc = seq // CHUNK

    def rows(w):
        return pl.BlockSpec((CHUNK, w), lambda bi, c: (bi * nc + c, 0))

    def vec(w, r=1):
        return pl.BlockSpec((r, w), lambda bi, c: (0, 0))

    return pl.pallas_call(
        _pssd_kernel, grid=(n_batch, nc),
        in_specs=[rows(D_XBC), rows(D_SSM), rows(HEAD_PAD), vec(D_XBC, SSM_CONV_W), vec(D_XBC),
                  vec(HEAD_PAD), vec(D_SSM, 3 * HEAD_PAD), vec(D_SSM), vec(D_SSM)],
        out_specs=[rows(D_SSM), pl.BlockSpec((1, D_SSM, D_STATE), lambda bi, c: (bi, 0, 0)),
                   pl.BlockSpec((1, XCARRY, D_XBC), lambda bi, c: (bi, 0, 0))],
        out_shape=[jax.ShapeDtypeStruct((m, D_SSM), BF16),
                   jax.ShapeDtypeStruct((n_batch, D_SSM, D_STATE), F32),
                   jax.ShapeDtypeStruct((n_batch, XCARRY, D_XBC), F32)],
        scratch_shapes=[pltpu.VMEM((D_STATE, D_SSM), F32), pltpu.VMEM((XCARRY + CHUNK, D_XBC), F32),
                        pltpu.VMEM((CHUNK, D_XBC), F32)],
        compiler_params=_params(("arbitrary", "arbitrary")), name="prompt_ssd",
    )(xbc, z, dt, cw, cb, a_pad, _head_expand_matrix(), d_exp, ng)


def _sprep_kernel(v_ref, cst_ref, w_ref, b_ref, lg_ref, lb_ref,
                  xbc_ref, xst_ref, cw_ref, cb_ref, dt_ref, a_ref, dexp_ref, cmix_in_ref,
                  co_ref, ncst_ref, nxst_ref, xdt_ref, bc_ref, dec_ref, skip_ref):
    del cmix_in_ref
    nb = v_ref.shape[0]
    v = v_ref[...]
    acc = jnp.broadcast_to(b_ref[...], (nb, D_CONV)) + w_ref[CONV_W - 1:CONV_W, :] * v
    for k in range(CONV_W - 1):
        acc = acc + w_ref[k:k + 1, :] * cst_ref[k]
    co_ref[...] = _ln_swish(acc, lg_ref[...], lb_ref[...]).astype(BF16)
    for k in range(CONV_W - 2):
        ncst_ref[k] = cst_ref[k + 1]
    ncst_ref[CONV_W - 2] = v

    xn = xbc_ref[...]
    acc = jnp.broadcast_to(cb_ref[...], (nb, D_XBC)) + cw_ref[SSM_CONV_W - 1:SSM_CONV_W, :] * xn
    for k in range(SSM_CONV_W - 1):
        acc = acc + cw_ref[k:k + 1, :] * xst_ref[k]
    xc = _silu(acc)
    for k in range(SSM_CONV_W - 2):
        nxst_ref[k] = xst_ref[k + 1]
    nxst_ref[SSM_CONV_W - 2] = xn

    xs = xc[:, :D_SSM]
    bc_ref[...] = xc[:, D_SSM:]
    dt = dt_ref[...]
    dec_ref[...] = jnp.exp(dt * a_ref[...])
    dt_exp = _dot3_lhs(dt, _head_expand_matrix())
    xdt_ref[...] = xs * dt_exp
    skip_ref[...] = dexp_ref[...] * xs


def _sssd_kernel(dec_ref, h0_ref, xdt_ref, bc_ref, skip_ref, z_ref, ng_ref, ymix_in_ref,
                 hn_ref, y_ref, yt_scr, xdt_t_scr):
    del ymix_in_ref
    step = pl.program_id(0)
    nb = xdt_ref.shape[0]

    @pl.when(step == 0)
    def _():
        yt_scr[...] = jnp.zeros_like(yt_scr)
        xdt_t_scr[...] = xdt_ref[...].T.astype(BF16)

    seq_i = lax.broadcasted_iota(jnp.int32, (nb, D_STATE), 0)
    for bb in range(SSD_BLOCK):
        b = step * SSD_BLOCK + bb
        brow = bc_ref[pl.ds(b, 1), :]
        is_b = seq_i == b
        for g in range(N_GROUPS):
            rows = slice(g * GROUP_W, (g + 1) * GROUP_W)
            b_sel = jnp.where(is_b, brow[:, g * D_STATE:(g + 1) * D_STATE], 0.0).astype(BF16)
            c_sel = jnp.where(is_b, brow[:, (N_GROUPS + g) * D_STATE:(N_GROUPS + g + 1) * D_STATE],
                              0.0).astype(BF16)
            s_new = _dot(xdt_t_scr[rows, :], b_sel)
            parts = []
            for r in range(HEADS_PER_GROUP):
                h = g * HEADS_PER_GROUP + r
                hr = slice(h * HEAD_DIM, (h + 1) * HEAD_DIM)
                parts.append(h0_ref[bb, hr, :] * dec_ref[b * N_HEADS + h]
                             + s_new[r * HEAD_DIM:(r + 1) * HEAD_DIM, :])
            h_new = jnp.concatenate(parts, axis=0)
            hn_ref[bb, rows, :] = h_new
            yt_scr[rows, :] += lax.dot_general(h_new.astype(BF16), c_sel, (((1,), (1,)), ((), ())),
                                               preferred_element_type=F32)

    @pl.when(step == pl.num_programs(0) - 1)
    def _():
        y = yt_scr[...].T + skip_ref[...]
        for g in range(N_GROUPS):
            ch = slice(g * GROUP_W, (g + 1) * GROUP_W)
            y_ref[:, ch] = _gated_norm(y[:, ch], z_ref[:, ch], ng_ref[:, ch]).astype(BF16)


def _sample_mixer(glu, z, xbc, dt, cmix, ymix, n_prompt, n_samp, cst, xst, h0,
                  mw, mb, lg, lb, cw, cb, a_pad, d_exp, ng):
    sb = SAMPLE_BLOCK
    off = n_prompt // sb
    srow = lambda w: pl.BlockSpec((sb, w), lambda i: (off + i, 0))
    orow = lambda w: pl.BlockSpec((sb, w), lambda i: (i, 0))
    taps = lambda k, w: pl.BlockSpec((k, sb, w), lambda i: (0, i, 0))
    const = lambda *s: pl.BlockSpec(s, lambda i: (0,) * len(s))
    anyspec = pl.BlockSpec(memory_space=pl.ANY)
    cst_t = jnp.swapaxes(cst, 0, 1)
    xst_t = jnp.swapaxes(xst, 0, 1)
    cmix, ncst_t, nxst_t, xdt, bc, dec, skip = pl.pallas_call(
        _sprep_kernel, grid=(n_samp // sb,),
        in_specs=[srow(D_CONV), taps(CONV_W - 1, D_CONV), const(CONV_W, D_CONV), const(1, D_CONV),
                  const(1, D_CONV), const(1, D_CONV),
                  srow(D_XBC), taps(SSM_CONV_W - 1, D_XBC), const(SSM_CONV_W, D_XBC),
                  const(1, D_XBC), srow(HEAD_PAD), const(1, HEAD_PAD), const(1, D_SSM), anyspec],
        out_specs=[srow(D_CONV), taps(CONV_W - 1, D_CONV), taps(SSM_CONV_W - 1, D_XBC),
                   orow(D_SSM), orow(D_XBC - D_SSM), orow(HEAD_PAD), orow(D_SSM)],
        out_shape=[jax.ShapeDtypeStruct(cmix.shape, BF16),
                   jax.ShapeDtypeStruct((CONV_W - 1, n_samp, D_CONV), F32),
                   jax.ShapeDtypeStruct((SSM_CONV_W - 1, n_samp, D_XBC), F32),
                   jax.ShapeDtypeStruct((n_samp, D_SSM), F32),
                   jax.ShapeDtypeStruct((n_samp, D_XBC - D_SSM), F32),
                   jax.ShapeDtypeStruct((n_samp, HEAD_PAD), F32),
                   jax.ShapeDtypeStruct((n_samp, D_SSM), F32)],
        input_output_aliases={13: 0},
        compiler_params=_params(("arbitrary",)), name="sample_prep",
    )(glu, cst_t, mw, mb, lg, lb, xbc, xst_t, cw, cb, dt, a_pad, d_exp, cmix)

    dec_flat = dec[:, :N_HEADS].reshape(-1)
    sblock = n_prompt // n_samp
    assert n_samp % SSD_BLOCK == 0
    hn, ymix = pl.pallas_call(
        _sssd_kernel, grid=(n_samp // SSD_BLOCK,),
        in_specs=[pl.BlockSpec(memory_space=pltpu.SMEM),
                  pl.BlockSpec((SSD_BLOCK, D_SSM, D_STATE), lambda b: (b, 0, 0)),
                  const(n_samp, D_SSM), const(n_samp, D_XBC - D_SSM), const(n_samp, D_SSM),
                  pl.BlockSpec((n_samp, D_SSM), lambda b: (sblock, 0)), const(1, D_SSM), anyspec],
        out_specs=[pl.BlockSpec((SSD_BLOCK, D_SSM, D_STATE), lambda b: (b, 0, 0)),
                   pl.BlockSpec((n_samp, D_SSM), lambda b: (sblock, 0))],
        out_shape=[jax.ShapeDtypeStruct((n_samp, D_SSM, D_STATE), F32),
                   jax.ShapeDtypeStruct(ymix.shape, BF16)],
        scratch_shapes=[pltpu.VMEM((D_SSM, n_samp), F32), pltpu.VMEM((D_SSM, n_samp), BF16)],
        input_output_aliases={7: 1},
        compiler_params=_params(("arbitrary",)), name="sample_ssd",
    )(dec_flat, h0.reshape(n_samp, D_SSM, D_STATE), xdt, bc, skip, z, ng, ymix)
    return cmix, ymix, jnp.swapaxes(ncst_t, 0, 1), jnp.swapaxes(nxst_t, 0, 1), hn


def _outproj_kernel(c_ref, y_ref, w_ref, h_ref, g_ref, o_ref):
    mix = _dot(c_ref[...], w_ref[0:D_CONV, :]) + _dot(y_ref[...], w_ref[D_CONV:, :])
    o_ref[...] = h_ref[...] + _rms(mix, g_ref[...])


def _out_proj(cmix, ymix, w_out, h, g):
    m = h.shape[0]
    row = pl.BlockSpec((TM, D_MODEL), lambda i: (i, 0))
    return pl.pallas_call(
        _outproj_kernel, grid=(m // TM,),
        in_specs=[pl.BlockSpec((TM, D_CONV), lambda i: (i, 0)),
                  pl.BlockSpec((TM, D_SSM), lambda i: (i, 0)),
                  pl.BlockSpec((D_CONV + D_SSM, D_MODEL), lambda i: (0, 0)),
                  row, pl.BlockSpec((1, D_MODEL), lambda i: (0, 0))],
        out_specs=row, out_shape=jax.ShapeDtypeStruct((m, D_MODEL), F32),
        compiler_params=_params(("arbitrary",)), name="out_proj")(cmix, ymix, w_out, h, g)


def _ple_kernel(h_ref, pa_ref, pb_ref, gpre_ref, wg_ref, wp_ref, gpost_ref, oa_ref, ob_ref, emb_scr):
    i = pl.program_id(0)

    def embed(rows, src, srows):
        emb_scr[rows, :] = _dot(src[srows, :].astype(BF16), wp_ref[...])
    _on_tile_rows(i, pa_ref, pb_ref, embed)

    h = h_ref[...]
    gate = jax.nn.sigmoid(_dot(_rms(h, gpre_ref[...]).astype(BF16), wg_ref[...]))
    oa_ref[...] = h + _rms(gate * emb_scr[...], gpost_ref[...])

    @pl.when(i == pl.num_programs(0) - 1)
    def _():
        nb = ob_ref.shape[0]
        ob_ref[...] = oa_ref[TM - nb:TM, :]


def _ple(h, p_prompt, p_samp, gpre, wg, wp, gpost):
    m = h.shape[0]
    n_prompt, n_samp = p_prompt.shape[0], p_samp.shape[0]
    row = pl.BlockSpec((TM, D_MODEL), lambda i: (i, 0))
    vec = pl.BlockSpec((1, D_MODEL), lambda i: (0, 0))
    return pl.pallas_call(
        _ple_kernel, grid=(m // TM,),
        in_specs=[row, pl.BlockSpec((TM, PLE_DIM), lambda i: (i, 0)),
                  pl.BlockSpec((n_samp, PLE_DIM), lambda i: (0, 0)), vec,
                  pl.BlockSpec((D_MODEL, D_MODEL), lambda i: (0, 0)),
                  pl.BlockSpec((PLE_DIM, D_MODEL), lambda i: (0, 0)), vec],
        out_specs=[row, pl.BlockSpec((n_samp, D_MODEL), lambda i: (0, 0))],
        out_shape=[jax.ShapeDtypeStruct((n_prompt, D_MODEL), F32),
                   jax.ShapeDtypeStruct((n_samp, D_MODEL), F32)],
        scratch_shapes=[pltpu.VMEM((TM, D_MODEL), F32)],
        compiler_params=_params(("arbitrary",)), name="ple")(h, p_prompt, p_samp, gpre, wg, wp, gpost)


def _layer(x_prompt, x_samp, p_prompt, p_samp, n_batch, seq, cst, xst, h0, lw):
    (norm_ffn1_pre, w_ffn1_gate, w_ffn1_up, w_ffn1_down, norm_ffn1_post,
     norm_mix_pre, w_in, conv_mod_w, conv_mod_b, conv_mod_ln_g, conv_mod_ln_b,
     ssm_conv_w, ssm_conv_b, dt_bias, a_log, d_skip, ssm_norm_g, w_out, norm_mix_post,
     norm_ffn2_pre, w_ffn2_gate, w_ffn2_up, w_ffn2_down, norm_ffn2_post,
     norm_ple_pre, w_ple_gate, w_ple_proj, norm_ple_post) = lw
    n_prompt, n_samp = x_prompt.shape[0], x_samp.shape[0]
    m = n_prompt + n_samp
    _check_split(n_prompt, n_samp)
    row2 = lambda t: t.reshape(1, -1)

    later = ((jnp.swapaxes(w_in, 0, 1), 128, True), (w_out, 32, False), (w_ffn2_gate, 16, False),
             (w_ffn2_up, 16, False), (w_ffn2_down, 64, False), (w_ple_gate, 16, False),
             (w_ple_proj, 16, False))
    ffn1 = (row2(norm_ffn1_pre), row2(norm_ffn1_post), row2(norm_mix_pre))
    h1, u, wg1, wu1, wd1 = _ffn(
        x_prompt, ffn1[0], w_ffn1_gate, w_ffn1_up, w_ffn1_down, ffn1[1], gnext=ffn1[2],
        m=m, tiles=(0, 1), emit_w16=True, tf=TF_FIRST, name="ffn_first")
    h1, u, w_in16, w_out16, wg2, wu2, wd2, wpg, wpp = _ffn(
        x_prompt, ffn1[0], wg1, wu1, wd1, ffn1[1], gnext=ffn1[2], x_tail=x_samp, casts=later,
        m=m, tiles=(1, m // TM - 1), carry=(h1, u), name="ffn_rest")

    d_proj = w_in.shape[1]
    w_dt = jnp.pad(w_in16[:, d_proj - N_HEADS:], ((0, 0), (0, HEAD_PAD - N_HEADS)))
    pad_h = lambda t: jnp.pad(t.astype(F32), (0, HEAD_PAD - N_HEADS)).reshape(1, HEAD_PAD)
    glu, z, xbc, dt = _in_proj(u, w_in16, w_dt, pad_h(dt_bias))

    a = -jnp.exp(a_log.astype(F32))
    a_pad = pad_h(a)
    d_exp = row2(jnp.repeat(d_skip.astype(F32), HEAD_DIM))
    cw, cb = ssm_conv_w, row2(ssm_conv_b)
    mw, mb, lg, lb = conv_mod_w, row2(conv_mod_b), row2(conv_mod_ln_g), row2(conv_mod_ln_b)
    ng = row2(ssm_norm_g)

    cmix, glu_tail = _prompt_conv(glu, m, n_batch, seq, mw, mb, lg, lb)
    ymix, hfin_prompt, xbc_tail = _prompt_ssd(xbc, z, dt, m, n_batch, seq, cw, cb, a_pad, d_exp, ng)
    cmix, ymix, new_cst_samp, new_xst_samp, hn = _sample_mixer(
        glu, z, xbc, dt, cmix, ymix, n_prompt, n_samp, cst, xst, h0,
        mw, mb, lg, lb, cw, cb, a_pad, d_exp, ng)

    h2 = _out_proj(cmix, ymix, w_out16, h1, row2(norm_mix_post))
    (h3,) = _ffn(h2, row2(norm_ffn2_pre), wg2, wu2, wd2, row2(norm_ffn2_post))
    y_prompt, y_samp = _ple(h3, p_prompt, p_samp, row2(norm_ple_pre), wpg, wpp, row2(norm_ple_post))

    new_cst_prompt = glu_tail[:, CARRY - (CONV_W - 1):]
    new_xst_prompt = xbc_tail[:, XCARRY - (SSM_CONV_W - 1):]
    new_h_prompt = hfin_prompt.reshape(n_batch, N_HEADS, HEAD_DIM, D_STATE)
    new_h_samp = hn.reshape(n_samp, N_HEADS, HEAD_DIM, D_STATE)
    return (y_prompt, y_samp, new_cst_prompt, new_xst_prompt, new_h_prompt,
            new_cst_samp, new_xst_samp, new_h_samp)


def kernel(x_prompt, x_sample, state_conv_mod, state_ssm_conv, state_ssm, p_prompt, p_sample,
           norm_ffn1_pre, w_ffn1_gate, w_ffn1_up, w_ffn1_down, norm_ffn1_post,
           norm_mix_pre, w_in, conv_mod_w, conv_mod_b, conv_mod_ln_g, conv_mod_ln_b,
           ssm_conv_w, ssm_conv_b, dt_bias, a_log, d_skip, ssm_norm_g, w_out, norm_mix_post,
           norm_ffn2_pre, w_ffn2_gate, w_ffn2_up, w_ffn2_down, norm_ffn2_post,
           norm_ple_pre, w_ple_gate, w_ple_proj, norm_ple_post):
    weights = (norm_ffn1_pre, w_ffn1_gate, w_ffn1_up, w_ffn1_down, norm_ffn1_post,
               norm_mix_pre, w_in, conv_mod_w, conv_mod_b, conv_mod_ln_g, conv_mod_ln_b,
               ssm_conv_w, ssm_conv_b, dt_bias, a_log, d_skip, ssm_norm_g, w_out, norm_mix_post,
               norm_ffn2_pre, w_ffn2_gate, w_ffn2_up, w_ffn2_down, norm_ffn2_post,
               norm_ple_pre, w_ple_gate, w_ple_proj, norm_ple_post)
    n_batch, seq, _ = x_prompt.shape
    n_samp = x_sample.shape[0]
    n_prompt = n_batch * seq
    depth = norm_ffn1_pre.shape[0]
    xp = x_prompt.reshape(n_prompt, D_MODEL)
    xs = x_sample.reshape(n_samp, D_MODEL)
    outs = [[] for _ in range(6)]
    for i in range(depth):
        res = _layer(xp, xs, p_prompt[i].reshape(n_prompt, PLE_DIM), p_sample[i].reshape(n_samp, PLE_DIM),
                     n_batch, seq, state_conv_mod[i], state_ssm_conv[i], state_ssm[i],
                     tuple(w[i] for w in weights))
        xp, xs = res[0], res[1]
        for lst, r in zip(outs, res[2:]):
            lst.append(r)
    return ((xp.reshape(n_batch, seq, D_MODEL), xs.reshape(n_samp, 1, D_MODEL))
            + tuple(jnp.stack(lst, axis=0) for lst in outs))
```

```python
import functools

import jax
import jax.numpy as jnp
from jax import lax
from jax.experimental import pallas as pl
from jax.experimental.pallas import tpu as pltpu

F32 = jnp.float32
BF16 = jnp.bfloat16

D_MODEL = 2048
D_FF = 5632
D_CONV = 1024
D_SSM = 3072
N_HEADS = 48
HEAD_DIM = 64
N_GROUPS = 8
HEADS_PER_GROUP = 6
GROUP_W = HEADS_PER_GROUP * HEAD_DIM
D_STATE = 128
D_XBC = D_SSM + 2 * N_GROUPS * D_STATE
CONV_W = 31
SSM_CONV_W = 4
CHUNK = 128
PLE_DIM = 256
EPS = 1e-6
NEG_BIG = -1e30
HEAD_PAD = 128

TM = 640
TF = 512
TF_FIRST = 256
TMP = 1040
TN = 512
TL = 256
CARRY = 32
CONV_ROWS = 64
NORM_ROWS = 16
CONV_SPAN = TL + CARRY - 8
XCARRY = 8
SAMPLE_BLOCK = 32
VMEM_LIMIT = 56 * 1024 * 1024


def _params(dims, vmem=VMEM_LIMIT):
    return pltpu.CompilerParams(dimension_semantics=dims, vmem_limit_bytes=vmem)


def _rms(x, g):
    return x * lax.rsqrt(jnp.mean(x * x, axis=-1, keepdims=True) + EPS) * g


def _silu(x):
    return x * jax.nn.sigmoid(x)


def _dot(a, b):
    return jnp.dot(a, b, preferred_element_type=F32)


def _split3(x):
    hi = x.astype(BF16)
    r = x - hi.astype(F32)
    mid = r.astype(BF16)
    lo = (r - mid.astype(F32)).astype(BF16)
    return hi, mid, lo


def _dot3_rhs(a3_bf16, x):
    return _dot(a3_bf16, jnp.concatenate(_split3(x), axis=0))


def _dot3_lhs(x, b3_bf16):
    return _dot(jnp.concatenate(_split3(x), axis=1), b3_bf16)


def _head_expand_matrix():
    head = lax.broadcasted_iota(jnp.int32, (3 * HEAD_PAD, D_SSM), 0) & (HEAD_PAD - 1)
    chan = lax.broadcasted_iota(jnp.int32, (3 * HEAD_PAD, D_SSM), 1)
    return jnp.where((chan >> 6) == head, 1.0, 0.0).astype(BF16)


def _on_tile_rows(i, a_ref, b_ref, fn):
    if b_ref is None:
        fn(slice(0, TM), a_ref, slice(0, TM))
        return
    nb = b_ref.shape[0]
    na = TM - nb
    last = pl.num_programs(0) - 1

    @pl.when(i != last)
    def _():
        fn(slice(0, TM), a_ref, slice(0, TM))

    @pl.when(i == last)
    def _():
        fn(slice(0, na), a_ref, slice(0, na))
        fn(slice(na, TM), b_ref, slice(0, nb))


def _check_split(n_prompt, n_samp):
    assert (n_prompt + n_samp) % TM == 0 and n_samp < TM and n_samp % 16 == 0


def _ffn_kernel(*refs, split, n_next, cast_t, emit_w16, n_carry):
    n_cast = len(cast_t)
    refs = list(refs)
    xa_ref = refs.pop(0)
    xb_ref = refs.pop(0) if split else None
    gpre_ref, wg_ref, wu_ref, wd_ref, gpost_ref = refs[:5]
    refs = refs[5:]
    gnext_ref = refs.pop(0) if n_next else None
    cast_in, refs = refs[:n_cast], refs[n_cast:]
    refs = refs[n_carry:]
    o_ref = refs.pop(0)
    unext_ref = refs.pop(0) if n_next else None
    cast_out, refs = refs[:n_cast], refs[n_cast:]
    w16_refs, refs = (refs[:3], refs[3:]) if emit_w16 else ((), refs)
    (u_scr,) = refs
    i = pl.program_id(0)
    j = pl.program_id(1)

    @pl.when(j == 0)
    def _():
        def pre(rows, src, srows):
            u_scr[rows, :] = _rms(src[srows, :], gpre_ref[...]).astype(BF16)
        _on_tile_rows(i, xa_ref, xb_ref, pre)
        o_ref[...] = jnp.zeros_like(o_ref)

    for ci, co, transpose in zip(cast_in, cast_out, cast_t):
        co[...] = (ci[...].T if transpose else ci[...]).astype(BF16)

    wg, wu, wd = wg_ref[...], wu_ref[...], wd_ref[...]
    if emit_w16:
        wg, wu, wd = wg.astype(BF16), wu.astype(BF16), wd.astype(BF16)
        for ref, w in zip(w16_refs, (wg, wu, wd)):
            ref[...] = w
    u = u_scr[...]
    act = (_silu(_dot(u, wg)) * _dot(u, wu)).astype(BF16)
    o_ref[...] += _dot(act, wd)

    @pl.when(j == pl.num_programs(1) - 1)
    def _():
        def post(rows, src, srows):
            step = rows.stop - rows.start if n_next else NORM_ROWS
            for r in range(0, rows.stop - rows.start, step):
                dst = slice(rows.start + r, rows.start + r + step)
                h = src[srows.start + r:srows.start + r + step, :] + _rms(o_ref[dst, :], gpost_ref[...])
                o_ref[dst, :] = h
                if n_next:
                    unext_ref[dst, :] = _rms(h, gnext_ref[...]).astype(BF16)
        _on_tile_rows(i, xa_ref, xb_ref, post)


def _ffn(x, gpre, wg, wu, wd, gpost, gnext=None, x_tail=None, casts=(), m=None, tiles=None,
         carry=(), emit_w16=False, tf=TF, name="ffn"):
    if m is None:
        m = x.shape[0] + (0 if x_tail is None else x_tail.shape[0])
    first, count = tiles if tiles is not None else (0, m // TM)
    grid = (count, D_FF // tf)
    steps = grid[0] * grid[1]
    row = pl.BlockSpec((TM, D_MODEL), lambda i, j: (i + first, 0))
    vec = pl.BlockSpec((1, D_MODEL), lambda i, j: (0, 0))
    wcol = pl.BlockSpec((D_MODEL, tf), lambda i, j: (0, j))
    wrow = pl.BlockSpec((tf, D_MODEL), lambda i, j: (j, 0))
    in_specs, args = [row], [x]
    if x_tail is not None:
        in_specs.append(pl.BlockSpec(x_tail.shape, lambda i, j: (0, 0)))
        args.append(x_tail)
    in_specs += [vec, wcol, wcol, wrow, vec]
    args += [gpre, wg, wu, wd, 0.5 * gpost]
    out_shape = [jax.ShapeDtypeStruct((m, D_MODEL), F32)]
    out_specs = [row]
    if gnext is not None:
        in_specs.append(vec)
        args.append(gnext)
        out_shape.append(jax.ShapeDtypeStruct((m, D_MODEL), BF16))
        out_specs.append(row)
    assert len(carry) in (0, len(out_shape))
    for w, r, transpose in casts:
        nblk = pl.cdiv(w.shape[0], r)
        assert r % 16 == 0 and nblk <= steps and (w.shape[0] % r == 0 or transpose)
        slab = lambda i, j, nblk=nblk: jnp.minimum(i * grid[1] + j, nblk - 1)
        in_specs.append(pl.BlockSpec((r, w.shape[1]), lambda i, j, slab=slab: (slab(i, j), 0)))
        args.append(w)
        if transpose:
            assert r % 128 == 0
            out_specs.append(pl.BlockSpec((w.shape[1], r), lambda i, j, slab=slab: (0, slab(i, j))))
            out_shape.append(jax.ShapeDtypeStruct(w.shape[::-1], BF16))
        else:
            out_specs.append(pl.BlockSpec((r, w.shape[1]), lambda i, j, slab=slab: (slab(i, j), 0)))
            out_shape.append(jax.ShapeDtypeStruct(w.shape, BF16))
    aliases = {}
    for k, c in enumerate(carry):
        aliases[len(args)] = k
        in_specs.append(pl.BlockSpec(memory_space=pl.ANY))
        args.append(c)
    if emit_w16:
        out_specs += [wcol, wcol, wrow]
        out_shape += [jax.ShapeDtypeStruct(w.shape, BF16) for w in (wg, wu, wd)]
    return pl.pallas_call(
        functools.partial(_ffn_kernel, split=x_tail is not None, n_next=gnext is not None,
                          cast_t=tuple(t for _, _, t in casts), emit_w16=emit_w16,
                          n_carry=len(carry)),
        grid=grid, in_specs=in_specs, out_specs=out_specs, out_shape=out_shape,
        scratch_shapes=[pltpu.VMEM((TM, D_MODEL), BF16)],
        input_output_aliases=aliases,
        compiler_params=_params(("arbitrary", "arbitrary")), name=name,
    )(*args)


N_GLU_STEPS = D_CONV // TN
N_Z_STEPS = D_SSM // TN
N_XBC_STEPS = D_XBC // TN


def _inproj_kernel(u_ref, w_ref, wb_ref, wdt_ref, dtb_ref, glu_ref, z_ref, xbc_ref, dt_ref):
    j = pl.program_id(1)

    @pl.when(j < N_GLU_STEPS)
    def _():
        u = u_ref[...]
        glu_ref[...] = _dot(u, w_ref[...]) * jax.nn.sigmoid(_dot(u, wb_ref[...]))

    @pl.when(jnp.logical_and(j >= N_GLU_STEPS, j < N_GLU_STEPS + N_Z_STEPS))
    def _():
        z_ref[...] = _dot(u_ref[...], w_ref[...])

    @pl.when(j >= N_GLU_STEPS + N_Z_STEPS)
    def _():
        xbc_ref[...] = _dot(u_ref[...], w_ref[...])

    @pl.when(j == pl.num_programs(1) - 1)
    def _():
        x = _dot(u_ref[...], wdt_ref[...]) + dtb_ref[...]
        dt_ref[...] = jnp.maximum(x, 0.0) + jnp.log1p(jnp.exp(-jnp.abs(x)))


def _in_proj(u, w_in, w_dt, dt_bias):
    m = u.shape[0]
    assert m % TMP == 0
    g, nz, nx = N_GLU_STEPS, N_Z_STEPS, N_XBC_STEPS
    return pl.pallas_call(
        _inproj_kernel, grid=(m // TMP, g + nz + nx),
        in_specs=[pl.BlockSpec((TMP, D_MODEL), lambda i, j: (i, 0)),
                  pl.BlockSpec((D_MODEL, TN), lambda i, j: (0, jnp.where(j < g, j, j + g))),
                  pl.BlockSpec((D_MODEL, TN), lambda i, j: (0, g + jnp.minimum(j, g - 1))),
                  pl.BlockSpec((D_MODEL, HEAD_PAD), lambda i, j: (0, 0)),
                  pl.BlockSpec((1, HEAD_PAD), lambda i, j: (0, 0))],
        out_specs=[pl.BlockSpec((TMP, TN), lambda i, j: (i, jnp.minimum(j, g - 1))),
                   pl.BlockSpec((TMP, TN), lambda i, j: (i, jnp.clip(j - g, 0, nz - 1))),
                   pl.BlockSpec((TMP, TN), lambda i, j: (i, jnp.clip(j - g - nz, 0, nx - 1))),
                   pl.BlockSpec((TMP, HEAD_PAD), lambda i, j: (i, 0))],
        out_shape=[jax.ShapeDtypeStruct((m, D_CONV), F32), jax.ShapeDtypeStruct((m, D_SSM), F32),
                   jax.ShapeDtypeStruct((m, D_XBC), F32), jax.ShapeDtypeStruct((m, HEAD_PAD), F32)],
        compiler_params=_params(("arbitrary", "arbitrary")), name="in_proj",
    )(u, w_in, w_in, w_dt, dt_bias)


def _ln_swish(y, g, b):
    mu = jnp.mean(y, axis=-1, keepdims=True)
    yc = y - mu
    yn = yc * lax.rsqrt(jnp.mean(yc * yc, axis=-1, keepdims=True) + EPS) * g + b
    return _silu(yn)


def _pconv_kernel(v_ref, w_ref, b_ref, lg_ref, lb_ref, o_ref, tail_ref, xpad_scr, conv_scr, shift_scr):
    t = pl.program_id(1)

    @pl.when(t == 0)
    def _():
        xpad_scr[0:CARRY, :] = jnp.zeros((CARRY, D_CONV), F32)

    xpad_scr[CARRY:CARRY + TL, :] = v_ref[...]
    first = CARRY - (CONV_W - 1)
    hb = CONV_ROWS
    n_rows = CARRY + TL
    for cb in range(D_CONV // 128):
        lanes = slice(cb * 128, (cb + 1) * 128)
        xfull = xpad_scr[:, lanes]
        conv_scr[:, lanes] = jnp.broadcast_to(b_ref[:, lanes], (TL, 128))
        for phase in range(8):
            xs = pltpu.roll(xfull, n_rows - phase, axis=0) if phase else xfull
            taps = [k for k in range(CONV_W) if (first + k) % 8 == phase]
            for base in range(0, TL, hb):
                acc = conv_scr[base:base + hb, lanes]
                for k in taps:
                    off = 8 * ((first + k) // 8)
                    acc = acc + w_ref[k:k + 1, lanes] * xs[base + off:base + off + hb]
                conv_scr[base:base + hb, lanes] = acc
    xpad_scr[0:CARRY, :] = xpad_scr[TL:TL + CARRY, :]
    for r in range(0, TL, NORM_ROWS):
        rows = slice(r, r + NORM_ROWS)
        o_ref[rows, :] = _ln_swish(conv_scr[rows, :], lg_ref[...], lb_ref[...]).astype(BF16)

    @pl.when(t == pl.num_programs(1) - 1)
    def _():
        tail_ref[0] = xpad_scr[0:CARRY, :]


def _prompt_conv(v, n_batch, seq, w, b, lg, lb):
    vec = pl.BlockSpec((1, D_CONV), lambda bi, t: (0, 0))
    steps = seq // TL
    return pl.pallas_call(
        _pconv_kernel, grid=(n_batch, steps),
        in_specs=[pl.BlockSpec((TL, D_CONV), lambda bi, t: (bi * steps + t, 0)),
                  pl.BlockSpec((CONV_W, D_CONV), lambda bi, t: (0, 0)), vec, vec, vec],
        out_specs=[pl.BlockSpec((TL, D_CONV), lambda bi, t: (bi * steps + t, 0)),
                   pl.BlockSpec((1, CARRY, D_CONV), lambda bi, t: (bi, 0, 0))],
        out_shape=[jax.ShapeDtypeStruct((n_batch * seq, D_CONV), BF16),
                   jax.ShapeDtypeStruct((n_batch, CARRY, D_CONV), F32)],
        scratch_shapes=[pltpu.VMEM((CARRY + TL, D_CONV), F32), pltpu.VMEM((TL, D_CONV), F32),
                        pltpu.VMEM((7, CONV_SPAN, 128), F32)],
        compiler_params=_params(("arbitrary", "arbitrary")), name="prompt_conv")(v, w, b, lg, lb)


def _gated_norm(y, z, g):
    yg = y * _silu(z)
    return yg * lax.rsqrt(jnp.mean(yg * yg, axis=-1, keepdims=True) + EPS) * g


def _sample_state_update(step, n_steps, dec_ref, h0_ref, xdt_ref, bc_ref, skip_ref, z_ref, ng_ref,
                         hn_ref, y_ref, yt_scr, xdt_t_scr):
    per_step = h0_ref.shape[0]
    nb = xdt_ref.shape[0]

    @pl.when(step == 0)
    def _():
        yt_scr[...] = jnp.zeros_like(yt_scr)
        xdt_t_scr[...] = xdt_ref[...].T.astype(BF16)

    seq_i = lax.broadcasted_iota(jnp.int32, (nb, D_STATE), 0)
    seqs = [step * per_step + bb for bb in range(per_step)]
    brows = [bc_ref[pl.ds(b, 1), :] for b in seqs]

    def one_hot_rows(g):
        cols = slice(g * D_STATE, (g + 1) * D_STATE)
        return jnp.concatenate([jnp.where(seq_i == b, brow[:, cols], 0.0).astype(BF16)
                                for b, brow in zip(seqs, brows)], axis=1)

    for g in range(N_GROUPS):
        rows = slice(g * GROUP_W, (g + 1) * GROUP_W)
        s_new = _dot(xdt_t_scr[rows, :], one_hot_rows(g))
        h_all = []
        for bb, b in enumerate(seqs):
            parts = []
            for r in range(HEADS_PER_GROUP):
                h = g * HEADS_PER_GROUP + r
                hr = slice(h * HEAD_DIM, (h + 1) * HEAD_DIM)
                parts.append(h0_ref[bb, hr, :] * dec_ref[b * N_HEADS + h]
                             + s_new[r * HEAD_DIM:(r + 1) * HEAD_DIM, bb * D_STATE:(bb + 1) * D_STATE])
            h_new = jnp.concatenate(parts, axis=0)
            hn_ref[bb, rows, :] = h_new
            h_all.append(h_new.astype(BF16))
        yt_scr[rows, :] += lax.dot_general(jnp.concatenate(h_all, axis=1), one_hot_rows(N_GROUPS + g),
                                           (((1,), (1,)), ((), ())), preferred_element_type=F32)

    @pl.when(step == n_steps - 1)
    def _():
        y = yt_scr[...].T + skip_ref[...]
        for g in range(N_GROUPS):
            ch = slice(g * GROUP_W, (g + 1) * GROUP_W)
            y_ref[:, ch] = _gated_norm(y[:, ch], z_ref[:, ch], ng_ref[:, ch]).astype(BF16)


def _pssd_kernel(xbc_ref, z_ref, dt_ref, cw_ref, cb_ref, a_ref, expand_ref, dexp_ref, ng_ref,
                 dec_ref, h0_ref, xdt_ref, bc_ref, skip_ref, zs_ref,
                 y_ref, hfin_ref, tail_ref, hn_ref, ys_ref,
                 state_scr, xpad_scr, xc_scr, yt_scr, xdt_t_scr):
    c = pl.program_id(1)
    q = CHUNK
    _sample_state_update(pl.program_id(0) * pl.num_programs(1) + c,
                         pl.num_programs(0) * pl.num_programs(1),
                         dec_ref, h0_ref, xdt_ref, bc_ref, skip_ref, zs_ref, ng_ref,
                         hn_ref, ys_ref, yt_scr, xdt_t_scr)

    @pl.when(c == 0)
    def _():
        state_scr[...] = jnp.zeros_like(state_scr)
        xpad_scr[0:XCARRY, :] = jnp.zeros((XCARRY, D_XBC), F32)

    xpad_scr[XCARRY:XCARRY + q, :] = xbc_ref[...]
    for cb in range(D_XBC // 512):
        lanes = slice(cb * 512, (cb + 1) * 512)
        acc = jnp.broadcast_to(cb_ref[:, lanes], (q, 512))
        for j in range(SSM_CONV_W):
            k = SSM_CONV_W - 1 - j
            acc = acc + cw_ref[k:k + 1, lanes] * xpad_scr[XCARRY - j:XCARRY - j + q, lanes]
        xc_scr[:, lanes] = _silu(acc)
    xpad_scr[0:XCARRY, :] = xpad_scr[q:q + XCARRY, :]

    row_i = lax.broadcasted_iota(jnp.int32, (q, q), 0)
    col_i = lax.broadcasted_iota(jnp.int32, (q, q), 1)
    tril = row_i >= col_i
    tri = jnp.where(tril, 1.0, 0.0).astype(BF16)
    expand = expand_ref[...]

    dt = dt_ref[...]
    tri3 = jnp.concatenate([tri, tri, tri], axis=1)
    a_cs = _dot3_rhs(tri3, dt * a_ref[...])
    a_cs_t = a_cs.T
    dt_exp = _dot3_lhs(dt, expand)
    acs_exp = _dot3_lhs(a_cs, expand)
    last = acs_exp[q - 1:q, :]
    lane_lo = lax.broadcasted_iota(jnp.int32, (q, 128), 1) < HEAD_DIM

    for g in range(N_GROUPS):
        ch = slice(g * GROUP_W, (g + 1) * GROUP_W)
        xs = xc_scr[:, ch]
        bg = xc_scr[:, D_SSM + g * D_STATE:D_SSM + (g + 1) * D_STATE]
        cg = xc_scr[:, D_SSM + (N_GROUPS + g) * D_STATE:D_SSM + (N_GROUPS + g + 1) * D_STATE]
        bg16 = bg.astype(BF16)
        cg16 = cg.astype(BF16)
        xdt = xs * dt_exp[:, ch]
        acs_g = acs_exp[:, ch]
        cb = lax.dot_general(cg16, bg16, (((1,), (1,)), ((), ())), preferred_element_type=F32)
        st = state_scr[:, ch]
        y = _dot(cg16, st.astype(BF16)) * jnp.exp(acs_g)
        pieces = []
        for pr in range(HEADS_PER_GROUP // 2):
            xpair = xdt[:, pr * 128:(pr + 1) * 128].astype(BF16)
            both = []
            for half in range(2):
                h = g * HEADS_PER_GROUP + 2 * pr + half
                seg = a_cs[:, h:h + 1] - a_cs_t[h:h + 1, :]
                decay = jnp.exp(jnp.where(tril, seg, NEG_BIG))
                both.append(_dot((cb * decay).astype(BF16), xpair))
            pieces.append(jnp.where(lane_lo, both[0], both[1]))
        y = y + jnp.concatenate(pieces, axis=1) + dexp_ref[:, ch] * xs
        xdec = (xdt * jnp.exp(last[:, ch] - acs_g)).astype(BF16)
        s_new = lax.dot_general(bg16, xdec, (((0,), (0,)), ((), ())), preferred_element_type=F32)
        state_scr[:, ch] = st * jnp.exp(last[:, ch]) + s_new
        y_ref[:, ch] = _gated_norm(y, z_ref[:, ch], ng_ref[:, ch]).astype(BF16)

    @pl.when(c == pl.num_programs(1) - 1)
    def _():
        hfin_ref[0] = state_scr[...].T
        tail_ref[0] = xpad_scr[0:XCARRY, :]


def _ssd(xbc, z, dt, n_batch, seq, cw, cb, a_pad, d_exp, ng, dec, h0, xdt, bc, skip):
    nc = seq // CHUNK
    n_prompt = n_batch * seq
    n_samp = h0.shape[0]
    per_step = n_samp // (n_batch * nc)
    assert per_step * n_batch * nc == n_samp and n_prompt % n_samp == 0

    def rows(w):
        return pl.BlockSpec((CHUNK, w), lambda bi, c: (bi * nc + c, 0))

    def vec(*s):
        return pl.BlockSpec(s, lambda bi, c: (0,) * len(s))

    states = pl.BlockSpec((per_step, D_SSM, D_STATE), lambda bi, c: (bi * nc + c, 0, 0))
    return pl.pallas_call(
        _pssd_kernel, grid=(n_batch, nc),
        in_specs=[rows(D_XBC), rows(D_SSM), rows(HEAD_PAD), vec(SSM_CONV_W, D_XBC), vec(1, D_XBC),
                  vec(1, HEAD_PAD), vec(3 * HEAD_PAD, D_SSM), vec(1, D_SSM), vec(1, D_SSM),
                  pl.BlockSpec(memory_space=pltpu.SMEM), states, vec(n_samp, D_SSM),
                  vec(n_samp, D_XBC - D_SSM), vec(n_samp, D_SSM),
                  pl.BlockSpec((n_samp, D_SSM), lambda bi, c: (n_prompt // n_samp, 0))],
        out_specs=[rows(D_SSM), pl.BlockSpec((1, D_SSM, D_STATE), lambda bi, c: (bi, 0, 0)),
                   pl.BlockSpec((1, XCARRY, D_XBC), lambda bi, c: (bi, 0, 0)),
                   states, vec(n_samp, D_SSM)],
        out_shape=[jax.ShapeDtypeStruct((n_prompt, D_SSM), BF16),
                   jax.ShapeDtypeStruct((n_batch, D_SSM, D_STATE), F32),
                   jax.ShapeDtypeStruct((n_batch, XCARRY, D_XBC), F32),
                   jax.ShapeDtypeStruct((n_samp, D_SSM, D_STATE), F32),
                   jax.ShapeDtypeStruct((n_samp, D_SSM), BF16)],
        scratch_shapes=[pltpu.VMEM((D_STATE, D_SSM), F32), pltpu.VMEM((XCARRY + CHUNK, D_XBC), F32),
                        pltpu.VMEM((CHUNK, D_XBC), F32),
                        pltpu.VMEM((D_SSM, n_samp), F32), pltpu.VMEM((D_SSM, n_samp), BF16)],
        compiler_params=_params(("arbitrary", "arbitrary")), name="ssd",
    )(xbc, z, dt, cw, cb, a_pad, _head_expand_matrix(), d_exp, ng, dec, h0, xdt, bc, skip, z)


def _sprep_kernel(v_ref, cst_ref, w_ref, b_ref, lg_ref, lb_ref,
                  xbc_ref, xst_ref, cw_ref, cb_ref, dt_ref, a_ref, dexp_ref,
                  co_ref, ncst_ref, nxst_ref, xdt_ref, bc_ref, dec_ref, skip_ref):
    nb = v_ref.shape[0]
    v = v_ref[...]
    acc = jnp.broadcast_to(b_ref[...], (nb, D_CONV)) + w_ref[CONV_W - 1:CONV_W, :] * v
    for k in range(CONV_W - 1):
        acc = acc + w_ref[k:k + 1, :] * cst_ref[k]
    co_ref[...] = _ln_swish(acc, lg_ref[...], lb_ref[...]).astype(BF16)
    for k in range(CONV_W - 2):
        ncst_ref[k] = cst_ref[k + 1]
    ncst_ref[CONV_W - 2] = v

    xn = xbc_ref[...]
    acc = jnp.broadcast_to(cb_ref[...], (nb, D_XBC)) + cw_ref[SSM_CONV_W - 1:SSM_CONV_W, :] * xn
    for k in range(SSM_CONV_W - 1):
        acc = acc + cw_ref[k:k + 1, :] * xst_ref[k]
    xc = _silu(acc)
    for k in range(SSM_CONV_W - 2):
        nxst_ref[k] = xst_ref[k + 1]
    nxst_ref[SSM_CONV_W - 2] = xn

    xs = xc[:, :D_SSM]
    bc_ref[...] = xc[:, D_SSM:]
    dt = dt_ref[...]
    dec_ref[...] = jnp.exp(dt * a_ref[...])
    dt_exp = _dot3_lhs(dt, _head_expand_matrix())
    xdt_ref[...] = xs * dt_exp
    skip_ref[...] = dexp_ref[...] * xs


def _sample_prep(glu, xbc, dt, n_prompt, n_samp, cst, xst, mw, mb, lg, lb, cw, cb, a_pad, d_exp):
    sb = SAMPLE_BLOCK
    off = n_prompt // sb
    srow = lambda w: pl.BlockSpec((sb, w), lambda i: (off + i, 0))
    orow = lambda w: pl.BlockSpec((sb, w), lambda i: (i, 0))
    taps = lambda k, w: pl.BlockSpec((k, sb, w), lambda i: (0, i, 0))
    const = lambda *s: pl.BlockSpec(s, lambda i: (0,) * len(s))
    cst_t = jnp.swapaxes(cst, 0, 1)
    xst_t = jnp.swapaxes(xst, 0, 1)
    c_samp, ncst_t, nxst_t, xdt, bc, dec, skip = pl.pallas_call(
        _sprep_kernel, grid=(n_samp // sb,),
        in_specs=[srow(D_CONV), taps(CONV_W - 1, D_CONV), const(CONV_W, D_CONV), const(1, D_CONV),
                  const(1, D_CONV), const(1, D_CONV),
                  srow(D_XBC), taps(SSM_CONV_W - 1, D_XBC), const(SSM_CONV_W, D_XBC),
                  const(1, D_XBC), srow(HEAD_PAD), const(1, HEAD_PAD), const(1, D_SSM)],
        out_specs=[orow(D_CONV), taps(CONV_W - 1, D_CONV), taps(SSM_CONV_W - 1, D_XBC),
                   orow(D_SSM), orow(D_XBC - D_SSM), orow(HEAD_PAD), orow(D_SSM)],
        out_shape=[jax.ShapeDtypeStruct((n_samp, D_CONV), BF16),
                   jax.ShapeDtypeStruct((CONV_W - 1, n_samp, D_CONV), F32),
                   jax.ShapeDtypeStruct((SSM_CONV_W - 1, n_samp, D_XBC), F32),
                   jax.ShapeDtypeStruct((n_samp, D_SSM), F32),
                   jax.ShapeDtypeStruct((n_samp, D_XBC - D_SSM), F32),
                   jax.ShapeDtypeStruct((n_samp, HEAD_PAD), F32),
                   jax.ShapeDtypeStruct((n_samp, D_SSM), F32)],
        compiler_params=_params(("arbitrary",)), name="sample_prep",
    )(glu, cst_t, mw, mb, lg, lb, xbc, xst_t, cw, cb, dt, a_pad, d_exp)
    dec_flat = dec[:, :N_HEADS].reshape(-1)
    return (c_samp, jnp.swapaxes(ncst_t, 0, 1), jnp.swapaxes(nxst_t, 0, 1), xdt, bc, dec_flat, skip)


def _outproj_kernel(c_ref, cs_ref, y_ref, ys_ref, w_ref, h_ref, g_ref, o_ref):
    i = pl.program_id(0)
    last = pl.num_programs(0) - 1
    na = TM - cs_ref.shape[0]

    def run(c, y):
        mix = _dot(c, w_ref[0:D_CONV, :]) + _dot(y, w_ref[D_CONV:, :])
        o_ref[...] = h_ref[...] + _rms(mix, g_ref[...])

    @pl.when(i != last)
    def _():
        run(c_ref[...], y_ref[...])

    @pl.when(i == last)
    def _():
        run(jnp.concatenate([c_ref[0:na, :], cs_ref[...]], axis=0),
            jnp.concatenate([y_ref[0:na, :], ys_ref[...]], axis=0))


def _out_proj(c_prompt, c_samp, y_prompt, y_samp, w_out, h, g):
    m = h.shape[0]
    n_samp = c_samp.shape[0]
    row = pl.BlockSpec((TM, D_MODEL), lambda i: (i, 0))
    return pl.pallas_call(
        _outproj_kernel, grid=(m // TM,),
        in_specs=[pl.BlockSpec((TM, D_CONV), lambda i: (i, 0)),
                  pl.BlockSpec((n_samp, D_CONV), lambda i: (0, 0)),
                  pl.BlockSpec((TM, D_SSM), lambda i: (i, 0)),
                  pl.BlockSpec((n_samp, D_SSM), lambda i: (0, 0)),
                  pl.BlockSpec((D_CONV + D_SSM, D_MODEL), lambda i: (0, 0)),
                  row, pl.BlockSpec((1, D_MODEL), lambda i: (0, 0))],
        out_specs=row, out_shape=jax.ShapeDtypeStruct((m, D_MODEL), F32),
        compiler_params=_params(("arbitrary",)), name="out_proj",
    )(c_prompt, c_samp, y_prompt, y_samp, w_out, h, g)


def _ple_kernel(h_ref, pa_ref, pb_ref, gpre_ref, wg_ref, wp_ref, gpost_ref, oa_ref, ob_ref, emb_scr):
    i = pl.program_id(0)

    def embed(rows, src, srows):
        emb_scr[rows, :] = _dot(src[srows, :].astype(BF16), wp_ref[...])
    _on_tile_rows(i, pa_ref, pb_ref, embed)

    h = h_ref[...]
    gate = jax.nn.sigmoid(_dot(_rms(h, gpre_ref[...]).astype(BF16), wg_ref[...]))
    oa_ref[...] = h + _rms(gate * emb_scr[...], gpost_ref[...])

    @pl.when(i == pl.num_programs(0) - 1)
    def _():
        nb = ob_ref.shape[0]
        ob_ref[...] = oa_ref[TM - nb:TM, :]


def _ple(h, p_prompt, p_samp, gpre, wg, wp, gpost):
    m = h.shape[0]
    n_prompt, n_samp = p_prompt.shape[0], p_samp.shape[0]
    row = pl.BlockSpec((TM, D_MODEL), lambda i: (i, 0))
    vec = pl.BlockSpec((1, D_MODEL), lambda i: (0, 0))
    return pl.pallas_call(
        _ple_kernel, grid=(m // TM,),
        in_specs=[row, pl.BlockSpec((TM, PLE_DIM), lambda i: (i, 0)),
                  pl.BlockSpec((n_samp, PLE_DIM), lambda i: (0, 0)), vec,
                  pl.BlockSpec((D_MODEL, D_MODEL), lambda i: (0, 0)),
                  pl.BlockSpec((PLE_DIM, D_MODEL), lambda i: (0, 0)), vec],
        out_specs=[row, pl.BlockSpec((n_samp, D_MODEL), lambda i: (0, 0))],
        out_shape=[jax.ShapeDtypeStruct((n_prompt, D_MODEL), F32),
                   jax.ShapeDtypeStruct((n_samp, D_MODEL), F32)],
        scratch_shapes=[pltpu.VMEM((TM, D_MODEL), F32)],
        compiler_params=_params(("arbitrary",)), name="ple")(h, p_prompt, p_samp, gpre, wg, wp, gpost)


def _layer(x_prompt, x_samp, p_prompt, p_samp, n_batch, seq, cst, xst, h0, lw):
    (norm_ffn1_pre, w_ffn1_gate, w_ffn1_up, w_ffn1_down, norm_ffn1_post,
     norm_mix_pre, w_in, conv_mod_w, conv_mod_b, conv_mod_ln_g, conv_mod_ln_b,
     ssm_conv_w, ssm_conv_b, dt_bias, a_log, d_skip, ssm_norm_g, w_out, norm_mix_post,
     norm_ffn2_pre, w_ffn2_gate, w_ffn2_up, w_ffn2_down, norm_ffn2_post,
     norm_ple_pre, w_ple_gate, w_ple_proj, norm_ple_post) = lw
    n_prompt, n_samp = x_prompt.shape[0], x_samp.shape[0]
    m = n_prompt + n_samp
    _check_split(n_prompt, n_samp)
    row2 = lambda t: t.reshape(1, -1)

    later = ((jnp.swapaxes(w_in, 0, 1), 128, True), (w_out, 32, False), (w_ffn2_gate, 16, False),
             (w_ffn2_up, 16, False), (w_ffn2_down, 64, False), (w_ple_gate, 16, False),
             (w_ple_proj, 16, False))
    ffn1 = (row2(norm_ffn1_pre), row2(norm_ffn1_post), row2(norm_mix_pre))
    h1, u, wg1, wu1, wd1 = _ffn(
        x_prompt, ffn1[0], w_ffn1_gate, w_ffn1_up, w_ffn1_down, ffn1[1], gnext=ffn1[2],
        m=m, tiles=(0, 1), emit_w16=True, tf=TF_FIRST, name="ffn_first")
    h1, u, w_in16, w_out16, wg2, wu2, wd2, wpg, wpp = _ffn(
        x_prompt, ffn1[0], wg1, wu1, wd1, ffn1[1], gnext=ffn1[2], x_tail=x_samp, casts=later,
        m=m, tiles=(1, m // TM - 1), carry=(h1, u), name="ffn_rest")

    d_proj = w_in.shape[1]
    w_dt = jnp.pad(w_in16[:, d_proj - N_HEADS:], ((0, 0), (0, HEAD_PAD - N_HEADS)))
    pad_h = lambda t: jnp.pad(t.astype(F32), (0, HEAD_PAD - N_HEADS)).reshape(1, HEAD_PAD)
    glu, z, xbc, dt = _in_proj(u, w_in16, w_dt, pad_h(dt_bias))

    a = -jnp.exp(a_log.astype(F32))
    a_pad = pad_h(a)
    d_exp = row2(jnp.repeat(d_skip.astype(F32), HEAD_DIM))
    cw, cb = ssm_conv_w, row2(ssm_conv_b)
    mw, mb, lg, lb = conv_mod_w, row2(conv_mod_b), row2(conv_mod_ln_g), row2(conv_mod_ln_b)
    ng = row2(ssm_norm_g)

    c_prompt, glu_tail = _prompt_conv(glu, n_batch, seq, mw, mb, lg, lb)
    c_samp, new_cst_samp, new_xst_samp, xdt, bc, dec, skip = _sample_prep(
        glu, xbc, dt, n_prompt, n_samp, cst, xst, mw, mb, lg, lb, cw, cb, a_pad, d_exp)
    y_mix_prompt, hfin_prompt, xbc_tail, hn, y_mix_samp = _ssd(
        xbc, z, dt, n_batch, seq, cw, cb, a_pad, d_exp, ng,
        dec, h0.reshape(n_samp, D_SSM, D_STATE), xdt, bc, skip)

    h2 = _out_proj(c_prompt, c_samp, y_mix_prompt, y_mix_samp, w_out16, h1, row2(norm_mix_post))
    (h3,) = _ffn(h2, row2(norm_ffn2_pre), wg2, wu2, wd2, row2(norm_ffn2_post))
    y_prompt, y_samp = _ple(h3, p_prompt, p_samp, row2(norm_ple_pre), wpg, wpp, row2(norm_ple_post))

    new_cst_prompt = glu_tail[:, CARRY - (CONV_W - 1):]
    new_xst_prompt = xbc_tail[:, XCARRY - (SSM_CONV_W - 1):]
    new_h_prompt = hfin_prompt.reshape(n_batch, N_HEADS, HEAD_DIM, D_STATE)
    new_h_samp = hn.reshape(n_samp, N_HEADS, HEAD_DIM, D_STATE)
    return (y_prompt, y_samp, new_cst_prompt, new_xst_prompt, new_h_prompt,
            new_cst_samp, new_xst_samp, new_h_samp)


def kernel(x_prompt, x_sample, state_conv_mod, state_ssm_conv, state_ssm, p_prompt, p_sample,
           norm_ffn1_pre, w_ffn1_gate, w_ffn1_up, w_ffn1_down, norm_ffn1_post,
           norm_mix_pre, w_in, conv_mod_w, conv_mod_b, conv_mod_ln_g, conv_mod_ln_b,
           ssm_conv_w, ssm_conv_b, dt_bias, a_log, d_skip, ssm_norm_g, w_out, norm_mix_post,
           norm_ffn2_pre, w_ffn2_gate, w_ffn2_up, w_ffn2_down, norm_ffn2_post,
           norm_ple_pre, w_ple_gate, w_ple_proj, norm_ple_post):
    weights = (norm_ffn1_pre, w_ffn1_gate, w_ffn1_up, w_ffn1_down, norm_ffn1_post,
               norm_mix_pre, w_in, conv_mod_w, conv_mod_b, conv_mod_ln_g, conv_mod_ln_b,
               ssm_conv_w, ssm_conv_b, dt_bias, a_log, d_skip, ssm_norm_g, w_out, norm_mix_post,
               norm_ffn2_pre, w_ffn2_gate, w_ffn2_up, w_ffn2_down, norm_ffn2_post,
               norm_ple_pre, w_ple_gate, w_ple_proj, norm_ple_post)
    n_batch, seq, _ = x_prompt.shape
    n_samp = x_sample.shape[0]
    n_prompt = n_batch * seq
    depth = norm_ffn1_pre.shape[0]
    xp = x_prompt.reshape(n_prompt, D_MODEL)
    xs = x_sample.reshape(n_samp, D_MODEL)
    outs = [[] for _ in range(6)]
    for i in range(depth):
        res = _layer(xp, xs, p_prompt[i].reshape(n_prompt, PLE_DIM), p_sample[i].reshape(n_samp, PLE_DIM),
                     n_batch, seq, state_conv_mod[i], state_ssm_conv[i], state_ssm[i],
                     tuple(w[i] for w in weights))
        xp, xs = res[0], res[1]
        for lst, r in zip(outs, res[2:]):
            lst.append(r)
    return ((xp.reshape(n_batch, seq, D_MODEL), xs.reshape(n_samp, 1, D_MODEL))
            + tuple(jnp.stack(lst, axis=0) for lst in outs))
```

```python
import functools

import jax
import jax.numpy as jnp
from jax import lax
from jax.experimental import pallas as pl
from jax.experimental.pallas import tpu as pltpu

F32 = jnp.float32
BF16 = jnp.bfloat16

D_MODEL = 2048
D_FF = 5632
D_CONV = 1024
D_SSM = 3072
N_HEADS = 48
HEAD_DIM = 64
N_GROUPS = 8
HEADS_PER_GROUP = 6
GROUP_W = HEADS_PER_GROUP * HEAD_DIM
D_STATE = 128
D_XBC = D_SSM + 2 * N_GROUPS * D_STATE
CONV_W = 31
SSM_CONV_W = 4
CHUNK = 128
PLE_DIM = 256
EPS = 1e-6
NEG_BIG = -1e30
HEAD_PAD = 128

TM = 640
TF = 512
TF_FIRST = 256
FIRST_TILES = 2
TMP = 832
TN = 1024
TL = 256
CARRY = 32
CONV_ROWS = 64
NORM_ROWS = 16
CONV_SPAN = TL + CARRY - 8
XCARRY = 8
SAMPLE_BLOCK = 32
VMEM_LIMIT = 56 * 1024 * 1024


def _params(dims, vmem=VMEM_LIMIT):
    return pltpu.CompilerParams(dimension_semantics=dims, vmem_limit_bytes=vmem)


def _rms(x, g):
    return x * lax.rsqrt(jnp.mean(x * x, axis=-1, keepdims=True) + EPS) * g


def _silu(x):
    return x * jax.nn.sigmoid(x)


def _dot(a, b):
    return jnp.dot(a, b, preferred_element_type=F32)


def _split3(x):
    hi = x.astype(BF16)
    r = x - hi.astype(F32)
    mid = r.astype(BF16)
    lo = (r - mid.astype(F32)).astype(BF16)
    return hi, mid, lo


def _dot3_rhs(a3_bf16, x):
    return _dot(a3_bf16, jnp.concatenate(_split3(x), axis=0))


def _dot3_lhs(x, b3_bf16):
    return _dot(jnp.concatenate(_split3(x), axis=1), b3_bf16)


def _head_expand_matrix():
    head = lax.broadcasted_iota(jnp.int32, (3 * HEAD_PAD, D_SSM), 0) & (HEAD_PAD - 1)
    chan = lax.broadcasted_iota(jnp.int32, (3 * HEAD_PAD, D_SSM), 1)
    return jnp.where((chan >> 6) == head, 1.0, 0.0).astype(BF16)


def _on_tile_rows(i, a_ref, b_ref, fn):
    tm = a_ref.shape[0]
    if b_ref is None:
        fn(slice(0, tm), a_ref, slice(0, tm))
        return
    nb = b_ref.shape[0]
    na = tm - nb
    last = pl.num_programs(0) - 1

    @pl.when(i != last)
    def _():
        fn(slice(0, tm), a_ref, slice(0, tm))

    @pl.when(i == last)
    def _():
        fn(slice(0, na), a_ref, slice(0, na))
        fn(slice(na, tm), b_ref, slice(0, nb))


def _check_split(n_prompt, n_samp):
    assert (n_prompt + n_samp) % TM == 0 and n_samp < TM and n_samp % 16 == 0


def _ffn_kernel(*refs, split, n_next, cast_t, emit_w16, n_carry):
    n_cast = len(cast_t)
    refs = list(refs)
    xa_ref = refs.pop(0)
    xb_ref = refs.pop(0) if split else None
    gpre_ref, wg_ref, wu_ref, wd_ref, gpost_ref = refs[:5]
    refs = refs[5:]
    gnext_ref = refs.pop(0) if n_next else None
    cast_in, refs = refs[:n_cast], refs[n_cast:]
    refs = refs[n_carry:]
    o_ref = refs.pop(0)
    unext_ref = refs.pop(0) if n_next else None
    cast_out, refs = refs[:n_cast], refs[n_cast:]
    w16_refs, refs = (refs[:3], refs[3:]) if emit_w16 else ((), refs)
    (u_scr,) = refs
    i = pl.program_id(0)
    j = pl.program_id(1)

    @pl.when(j == 0)
    def _():
        def pre(rows, src, srows):
            u_scr[rows, :] = _rms(src[srows, :], gpre_ref[...]).astype(BF16)
        _on_tile_rows(i, xa_ref, xb_ref, pre)
        o_ref[...] = jnp.zeros_like(o_ref)

    for ci, co, transpose in zip(cast_in, cast_out, cast_t):
        co[...] = (ci[...].T if transpose else ci[...]).astype(BF16)

    wg, wu, wd = wg_ref[...], wu_ref[...], wd_ref[...]
    if emit_w16:
        wg, wu, wd = wg.astype(BF16), wu.astype(BF16), wd.astype(BF16)
        for ref, w in zip(w16_refs, (wg, wu, wd)):
            ref[...] = w
    u = u_scr[...]
    act = (_silu(_dot(u, wg)) * _dot(u, wu)).astype(BF16)
    o_ref[...] += _dot(act, wd)

    @pl.when(j == pl.num_programs(1) - 1)
    def _():
        def post(rows, src, srows):
            step = rows.stop - rows.start if n_next else NORM_ROWS
            for r in range(0, rows.stop - rows.start, step):
                dst = slice(rows.start + r, rows.start + r + step)
                h = src[srows.start + r:srows.start + r + step, :] + _rms(o_ref[dst, :], gpost_ref[...])
                o_ref[dst, :] = h
                if n_next:
                    unext_ref[dst, :] = _rms(h, gnext_ref[...]).astype(BF16)
        _on_tile_rows(i, xa_ref, xb_ref, post)


def _ffn(x, gpre, wg, wu, wd, gpost, gnext=None, x_tail=None, casts=(), m=None, tiles=None,
         carry=(), emit_w16=False, tf=TF, tm=TM, name="ffn"):
    if m is None:
        m = x.shape[0] + (0 if x_tail is None else x_tail.shape[0])
    first, count = tiles if tiles is not None else (0, m // tm)
    grid = (count, D_FF // tf)
    steps = grid[0] * grid[1]
    row = pl.BlockSpec((tm, D_MODEL), lambda i, j: (i + first, 0))
    vec = pl.BlockSpec((1, D_MODEL), lambda i, j: (0, 0))
    wcol = pl.BlockSpec((D_MODEL, tf), lambda i, j: (0, j))
    wrow = pl.BlockSpec((tf, D_MODEL), lambda i, j: (j, 0))
    in_specs, args = [row], [x]
    if x_tail is not None:
        in_specs.append(pl.BlockSpec(x_tail.shape, lambda i, j: (0, 0)))
        args.append(x_tail)
    in_specs += [vec, wcol, wcol, wrow, vec]
    args += [gpre, wg, wu, wd, 0.5 * gpost]
    out_shape = [jax.ShapeDtypeStruct((m, D_MODEL), F32)]
    out_specs = [row]
    if gnext is not None:
        in_specs.append(vec)
        args.append(gnext)
        out_shape.append(jax.ShapeDtypeStruct((m, D_MODEL), BF16))
        out_specs.append(row)
    assert len(carry) in (0, len(out_shape))
    for w, r, transpose in casts:
        nblk = pl.cdiv(w.shape[0], r)
        assert r % 16 == 0 and nblk <= steps and (w.shape[0] % r == 0 or transpose)
        slab = lambda i, j, nblk=nblk: jnp.minimum(i * grid[1] + j, nblk - 1)
        in_specs.append(pl.BlockSpec((r, w.shape[1]), lambda i, j, slab=slab: (slab(i, j), 0)))
        args.append(w)
        if transpose:
            assert r % 128 == 0
            out_specs.append(pl.BlockSpec((w.shape[1], r), lambda i, j, slab=slab: (0, slab(i, j))))
            out_shape.append(jax.ShapeDtypeStruct(w.shape[::-1], BF16))
        else:
            out_specs.append(pl.BlockSpec((r, w.shape[1]), lambda i, j, slab=slab: (slab(i, j), 0)))
            out_shape.append(jax.ShapeDtypeStruct(w.shape, BF16))
    aliases = {}
    for k, c in enumerate(carry):
        aliases[len(args)] = k
        in_specs.append(pl.BlockSpec(memory_space=pl.ANY))
        args.append(c)
    if emit_w16:
        out_specs += [wcol, wcol, wrow]
        out_shape += [jax.ShapeDtypeStruct(w.shape, BF16) for w in (wg, wu, wd)]
    return pl.pallas_call(
        functools.partial(_ffn_kernel, split=x_tail is not None, n_next=gnext is not None,
                          cast_t=tuple(t for _, _, t in casts), emit_w16=emit_w16,
                          n_carry=len(carry)),
        grid=grid, in_specs=in_specs, out_specs=out_specs, out_shape=out_shape,
        scratch_shapes=[pltpu.VMEM((tm, D_MODEL), BF16)],
        input_output_aliases=aliases,
        compiler_params=_params(("arbitrary", "arbitrary")), name=name,
    )(*args)


N_GLU_STEPS = D_CONV // TN
N_Z_STEPS = D_SSM // TN
N_XBC_STEPS = D_XBC // TN


def _inproj_kernel(u_ref, w_ref, wb_ref, wdt_ref, dtb_ref, glu_ref, z_ref, xbc_ref, dt_ref):
    j = pl.program_id(1)

    @pl.when(j < N_GLU_STEPS)
    def _():
        u = u_ref[...]
        glu_ref[...] = _dot(u, w_ref[...]) * jax.nn.sigmoid(_dot(u, wb_ref[...]))

    @pl.when(jnp.logical_and(j >= N_GLU_STEPS, j < N_GLU_STEPS + N_Z_STEPS))
    def _():
        z_ref[...] = _dot(u_ref[...], w_ref[...])

    @pl.when(j >= N_GLU_STEPS + N_Z_STEPS)
    def _():
        xbc_ref[...] = _dot(u_ref[...], w_ref[...])

    @pl.when(j == pl.num_programs(1) - 1)
    def _():
        x = _dot(u_ref[...], wdt_ref[...]) + dtb_ref[...]
        dt_ref[...] = jnp.maximum(x, 0.0) + jnp.log1p(jnp.exp(-jnp.abs(x)))


def _in_proj(u, w_in, w_dt, dt_bias):
    m = u.shape[0]
    assert m % TMP == 0
    g, nz, nx = N_GLU_STEPS, N_Z_STEPS, N_XBC_STEPS
    return pl.pallas_call(
        _inproj_kernel, grid=(m // TMP, g + nz + nx),
        in_specs=[pl.BlockSpec((TMP, D_MODEL), lambda i, j: (i, 0)),
                  pl.BlockSpec((D_MODEL, TN), lambda i, j: (0, jnp.where(j < g, j, j + g))),
                  pl.BlockSpec((D_MODEL, TN), lambda i, j: (0, g + jnp.minimum(j, g - 1))),
                  pl.BlockSpec((D_MODEL, HEAD_PAD), lambda i, j: (0, 0)),
                  pl.BlockSpec((1, HEAD_PAD), lambda i, j: (0, 0))],
        out_specs=[pl.BlockSpec((TMP, TN), lambda i, j: (i, jnp.minimum(j, g - 1))),
                   pl.BlockSpec((TMP, TN), lambda i, j: (i, jnp.clip(j - g, 0, nz - 1))),
                   pl.BlockSpec((TMP, TN), lambda i, j: (i, jnp.clip(j - g - nz, 0, nx - 1))),
                   pl.BlockSpec((TMP, HEAD_PAD), lambda i, j: (i, 0))],
        out_shape=[jax.ShapeDtypeStruct((m, D_CONV), F32), jax.ShapeDtypeStruct((m, D_SSM), F32),
                   jax.ShapeDtypeStruct((m, D_XBC), F32), jax.ShapeDtypeStruct((m, HEAD_PAD), F32)],
        compiler_params=_params(("arbitrary", "arbitrary")), name="in_proj",
    )(u, w_in, w_in, w_dt, dt_bias)


def _ln_swish(y, g, b):
    mu = jnp.mean(y, axis=-1, keepdims=True)
    yc = y - mu
    yn = yc * lax.rsqrt(jnp.mean(yc * yc, axis=-1, keepdims=True) + EPS) * g + b
    return _silu(yn)


def _pconv_kernel(v_ref, w_ref, b_ref, lg_ref, lb_ref, o_ref, tail_ref, xpad_scr, conv_scr, shift_scr):
    t = pl.program_id(1)

    @pl.when(t == 0)
    def _():
        xpad_scr[0:CARRY, :] = jnp.zeros((CARRY, D_CONV), F32)

    xpad_scr[CARRY:CARRY + TL, :] = v_ref[...]
    first = CARRY - (CONV_W - 1)
    hb = CONV_ROWS
    n_rows = CARRY + TL
    for cb in range(D_CONV // 128):
        lanes = slice(cb * 128, (cb + 1) * 128)
        xfull = xpad_scr[:, lanes]
        conv_scr[:, lanes] = jnp.broadcast_to(b_ref[:, lanes], (TL, 128))
        for phase in range(8):
            xs = pltpu.roll(xfull, n_rows - phase, axis=0) if phase else xfull
            taps = [k for k in range(CONV_W) if (first + k) % 8 == phase]
            for base in range(0, TL, hb):
                acc = conv_scr[base:base + hb, lanes]
                for k in taps:
                    off = 8 * ((first + k) // 8)
                    acc = acc + w_ref[k:k + 1, lanes] * xs[base + off:base + off + hb]
                conv_scr[base:base + hb, lanes] = acc
    xpad_scr[0:CARRY, :] = xpad_scr[TL:TL + CARRY, :]
    for r in range(0, TL, NORM_ROWS):
        rows = slice(r, r + NORM_ROWS)
        o_ref[rows, :] = _ln_swish(conv_scr[rows, :], lg_ref[...], lb_ref[...]).astype(BF16)

    @pl.when(t == pl.num_programs(1) - 1)
    def _():
        tail_ref[0] = xpad_scr[0:CARRY, :]


def _prompt_conv(v, n_batch, seq, w, b, lg, lb):
    vec = pl.BlockSpec((1, D_CONV), lambda bi, t: (0, 0))
    steps = seq // TL
    return pl.pallas_call(
        _pconv_kernel, grid=(n_batch, steps),
        in_specs=[pl.BlockSpec((TL, D_CONV), lambda bi, t: (bi * steps + t, 0)),
                  pl.BlockSpec((CONV_W, D_CONV), lambda bi, t: (0, 0)), vec, vec, vec],
        out_specs=[pl.BlockSpec((TL, D_CONV), lambda bi, t: (bi * steps + t, 0)),
                   pl.BlockSpec((1, CARRY, D_CONV), lambda bi, t: (bi, 0, 0))],
        out_shape=[jax.ShapeDtypeStruct((n_batch * seq, D_CONV), BF16),
                   jax.ShapeDtypeStruct((n_batch, CARRY, D_CONV), F32)],
        scratch_shapes=[pltpu.VMEM((CARRY + TL, D_CONV), F32), pltpu.VMEM((TL, D_CONV), F32),
                        pltpu.VMEM((7, CONV_SPAN, 128), F32)],
        compiler_params=_params(("arbitrary", "arbitrary")), name="prompt_conv")(v, w, b, lg, lb)


def _gated_norm(y, z, g):
    yg = y * _silu(z)
    return yg * lax.rsqrt(jnp.mean(yg * yg, axis=-1, keepdims=True) + EPS) * g


def _sample_state_update(step, n_steps, dec_ref, h0_ref, xdt_ref, bc_ref, skip_ref, z_ref, ng_ref,
                         hn_ref, y_ref, yt_scr, xdt_t_scr):
    per_step = h0_ref.shape[0]
    nb = xdt_ref.shape[0]

    @pl.when(step == 0)
    def _():
        yt_scr[...] = jnp.zeros_like(yt_scr)
        xdt_t_scr[...] = xdt_ref[...].T.astype(BF16)

    seq_i = lax.broadcasted_iota(jnp.int32, (nb, D_STATE), 0)
    seqs = [step * per_step + bb for bb in range(per_step)]
    brows = [bc_ref[pl.ds(b, 1), :] for b in seqs]

    def one_hot_rows(g):
        cols = slice(g * D_STATE, (g + 1) * D_STATE)
        return jnp.concatenate([jnp.where(seq_i == b, brow[:, cols], 0.0).astype(BF16)
                                for b, brow in zip(seqs, brows)], axis=1)

    for g in range(N_GROUPS):
        rows = slice(g * GROUP_W, (g + 1) * GROUP_W)
        s_new = _dot(xdt_t_scr[rows, :], one_hot_rows(g))
        h_all = []
        for bb, b in enumerate(seqs):
            parts = []
            for r in range(HEADS_PER_GROUP):
                h = g * HEADS_PER_GROUP + r
                hr = slice(h * HEAD_DIM, (h + 1) * HEAD_DIM)
                parts.append(h0_ref[bb, hr, :] * dec_ref[b * N_HEADS + h]
                             + s_new[r * HEAD_DIM:(r + 1) * HEAD_DIM, bb * D_STATE:(bb + 1) * D_STATE])
            h_new = jnp.concatenate(parts, axis=0)
            hn_ref[bb, rows, :] = h_new
            h_all.append(h_new.astype(BF16))
        yt_scr[rows, :] += lax.dot_general(jnp.concatenate(h_all, axis=1), one_hot_rows(N_GROUPS + g),
                                           (((1,), (1,)), ((), ())), preferred_element_type=F32)

    @pl.when(step == n_steps - 1)
    def _():
        y = yt_scr[...].T + skip_ref[...]
        for g in range(N_GROUPS):
            ch = slice(g * GROUP_W, (g + 1) * GROUP_W)
            y_ref[:, ch] = _gated_norm(y[:, ch], z_ref[:, ch], ng_ref[:, ch]).astype(BF16)


def _pssd_kernel(xbc_ref, z_ref, dt_ref, cw_ref, cb_ref, a_ref, expand_ref, dexp_ref, ng_ref,
                 dec_ref, h0_ref, xdt_ref, bc_ref, skip_ref, zs_ref,
                 y_ref, hfin_ref, tail_ref, hn_ref, ys_ref,
                 state_scr, xpad_scr, xc_scr, yt_scr, xdt_t_scr):
    c = pl.program_id(1)
    q = CHUNK
    _sample_state_update(pl.program_id(0) * pl.num_programs(1) + c,
                         pl.num_programs(0) * pl.num_programs(1),
                         dec_ref, h0_ref, xdt_ref, bc_ref, skip_ref, zs_ref, ng_ref,
                         hn_ref, ys_ref, yt_scr, xdt_t_scr)

    @pl.when(c == 0)
    def _():
        state_scr[...] = jnp.zeros_like(state_scr)
        xpad_scr[0:XCARRY, :] = jnp.zeros((XCARRY, D_XBC), F32)

    xpad_scr[XCARRY:XCARRY + q, :] = xbc_ref[...]
    for cb in range(D_XBC // 512):
        lanes = slice(cb * 512, (cb + 1) * 512)
        acc = jnp.broadcast_to(cb_ref[:, lanes], (q, 512))
        for j in range(SSM_CONV_W):
            k = SSM_CONV_W - 1 - j
            acc = acc + cw_ref[k:k + 1, lanes] * xpad_scr[XCARRY - j:XCARRY - j + q, lanes]
        xc_scr[:, lanes] = _silu(acc)
    xpad_scr[0:XCARRY, :] = xpad_scr[q:q + XCARRY, :]

    row_i = lax.broadcasted_iota(jnp.int32, (q, q), 0)
    col_i = lax.broadcasted_iota(jnp.int32, (q, q), 1)
    tril = row_i >= col_i
    tri = jnp.where(tril, 1.0, 0.0).astype(BF16)
    expand = expand_ref[...]

    dt = dt_ref[...]
    tri3 = jnp.concatenate([tri, tri, tri], axis=1)
    a_cs = _dot3_rhs(tri3, dt * a_ref[...])
    a_cs_t = a_cs.T
    dt_exp = _dot3_lhs(dt, expand)
    acs_exp = _dot3_lhs(a_cs, expand)
    last = acs_exp[q - 1:q, :]
    lane_lo = lax.broadcasted_iota(jnp.int32, (q, 128), 1) < HEAD_DIM

    for g in range(N_GROUPS):
        ch = slice(g * GROUP_W, (g + 1) * GROUP_W)
        xs = xc_scr[:, ch]
        bg = xc_scr[:, D_SSM + g * D_STATE:D_SSM + (g + 1) * D_STATE]
        cg = xc_scr[:, D_SSM + (N_GROUPS + g) * D_STATE:D_SSM + (N_GROUPS + g + 1) * D_STATE]
        bg16 = bg.astype(BF16)
        cg16 = cg.astype(BF16)
        xdt = xs * dt_exp[:, ch]
        acs_g = acs_exp[:, ch]
        cb = lax.dot_general(cg16, bg16, (((1,), (1,)), ((), ())), preferred_element_type=F32)
        st = state_scr[:, ch]
        y = _dot(cg16, st.astype(BF16)) * jnp.exp(acs_g)
        pieces = []
        for pr in range(HEADS_PER_GROUP // 2):
            xpair = xdt[:, pr * 128:(pr + 1) * 128].astype(BF16)
            both = []
            for half in range(2):
                h = g * HEADS_PER_GROUP + 2 * pr + half
                seg = a_cs[:, h:h + 1] - a_cs_t[h:h + 1, :]
                decay = jnp.exp(jnp.where(tril, seg, NEG_BIG))
                both.append(_dot((cb * decay).astype(BF16), xpair))
            pieces.append(jnp.where(lane_lo, both[0], both[1]))
        y = y + jnp.concatenate(pieces, axis=1) + dexp_ref[:, ch] * xs
        xdec = (xdt * jnp.exp(last[:, ch] - acs_g)).astype(BF16)
        s_new = lax.dot_general(bg16, xdec, (((0,), (0,)), ((), ())), preferred_element_type=F32)
        state_scr[:, ch] = st * jnp.exp(last[:, ch]) + s_new
        y_ref[:, ch] = _gated_norm(y, z_ref[:, ch], ng_ref[:, ch]).astype(BF16)

    @pl.when(c == pl.num_programs(1) - 1)
    def _():
        hfin_ref[0] = state_scr[...].T
        tail_ref[0] = xpad_scr[0:XCARRY, :]


def _ssd(xbc, z, dt, n_batch, seq, cw, cb, a_pad, d_exp, ng, dec, h0, xdt, bc, skip):
    nc = seq // CHUNK
    n_prompt = n_batch * seq
    n_samp = h0.shape[0]
    per_step = n_samp // (n_batch * nc)
    assert per_step * n_batch * nc == n_samp and n_prompt % n_samp == 0

    def rows(w):
        return pl.BlockSpec((CHUNK, w), lambda bi, c: (bi * nc + c, 0))

    def vec(*s):
        return pl.BlockSpec(s, lambda bi, c: (0,) * len(s))

    states = pl.BlockSpec((per_step, D_SSM, D_STATE), lambda bi, c: (bi * nc + c, 0, 0))
    return pl.pallas_call(
        _pssd_kernel, grid=(n_batch, nc),
        in_specs=[rows(D_XBC), rows(D_SSM), rows(HEAD_PAD), vec(SSM_CONV_W, D_XBC), vec(1, D_XBC),
                  vec(1, HEAD_PAD), vec(3 * HEAD_PAD, D_SSM), vec(1, D_SSM), vec(1, D_SSM),
                  pl.BlockSpec(memory_space=pltpu.SMEM), states, vec(n_samp, D_SSM),
                  vec(n_samp, D_XBC - D_SSM), vec(n_samp, D_SSM),
                  pl.BlockSpec((n_samp, D_SSM), lambda bi, c: (n_prompt // n_samp, 0))],
        out_specs=[rows(D_SSM), pl.BlockSpec((1, D_SSM, D_STATE), lambda bi, c: (bi, 0, 0)),
                   pl.BlockSpec((1, XCARRY, D_XBC), lambda bi, c: (bi, 0, 0)),
                   states, vec(n_samp, D_SSM)],
        out_shape=[jax.ShapeDtypeStruct((n_prompt, D_SSM), BF16),
                   jax.ShapeDtypeStruct((n_batch, D_SSM, D_STATE), F32),
                   jax.ShapeDtypeStruct((n_batch, XCARRY, D_XBC), F32),
                   jax.ShapeDtypeStruct((n_samp, D_SSM, D_STATE), F32),
                   jax.ShapeDtypeStruct((n_samp, D_SSM), BF16)],
        scratch_shapes=[pltpu.VMEM((D_STATE, D_SSM), F32), pltpu.VMEM((XCARRY + CHUNK, D_XBC), F32),
                        pltpu.VMEM((CHUNK, D_XBC), F32),
                        pltpu.VMEM((D_SSM, n_samp), F32), pltpu.VMEM((D_SSM, n_samp), BF16)],
        compiler_params=_params(("arbitrary", "arbitrary")), name="ssd",
    )(xbc, z, dt, cw, cb, a_pad, _head_expand_matrix(), d_exp, ng, dec, h0, xdt, bc, skip, z)


def _sprep_kernel(v_ref, cst_ref, w_ref, b_ref, lg_ref, lb_ref,
                  xbc_ref, xst_ref, cw_ref, cb_ref, dt_ref, a_ref, dexp_ref,
                  co_ref, ncst_ref, nxst_ref, xdt_ref, bc_ref, dec_ref, skip_ref):
    nb = v_ref.shape[0]
    v = v_ref[...]
    acc = jnp.broadcast_to(b_ref[...], (nb, D_CONV)) + w_ref[CONV_W - 1:CONV_W, :] * v
    for k in range(CONV_W - 1):
        acc = acc + w_ref[k:k + 1, :] * cst_ref[k]
    co_ref[...] = _ln_swish(acc, lg_ref[...], lb_ref[...]).astype(BF16)
    for k in range(CONV_W - 2):
        ncst_ref[k] = cst_ref[k + 1]
    ncst_ref[CONV_W - 2] = v

    xn = xbc_ref[...]
    acc = jnp.broadcast_to(cb_ref[...], (nb, D_XBC)) + cw_ref[SSM_CONV_W - 1:SSM_CONV_W, :] * xn
    for k in range(SSM_CONV_W - 1):
        acc = acc + cw_ref[k:k + 1, :] * xst_ref[k]
    xc = _silu(acc)
    for k in range(SSM_CONV_W - 2):
        nxst_ref[k] = xst_ref[k + 1]
    nxst_ref[SSM_CONV_W - 2] = xn

    xs = xc[:, :D_SSM]
    bc_ref[...] = xc[:, D_SSM:]
    dt = dt_ref[...]
    dec_ref[...] = jnp.exp(dt * a_ref[...])
    dt_exp = _dot3_lhs(dt, _head_expand_matrix())
    xdt_ref[...] = xs * dt_exp
    skip_ref[...] = dexp_ref[...] * xs


def _sample_prep(glu, xbc, dt, n_prompt, n_samp, cst, xst, mw, mb, lg, lb, cw, cb, a_pad, d_exp):
    sb = SAMPLE_BLOCK
    off = n_prompt // sb
    srow = lambda w: pl.BlockSpec((sb, w), lambda i: (off + i, 0))
    orow = lambda w: pl.BlockSpec((sb, w), lambda i: (i, 0))
    taps = lambda k, w: pl.BlockSpec((k, sb, w), lambda i: (0, i, 0))
    const = lambda *s: pl.BlockSpec(s, lambda i: (0,) * len(s))
    cst_t = jnp.swapaxes(cst, 0, 1)
    xst_t = jnp.swapaxes(xst, 0, 1)
    c_samp, ncst_t, nxst_t, xdt, bc, dec, skip = pl.pallas_call(
        _sprep_kernel, grid=(n_samp // sb,),
        in_specs=[srow(D_CONV), taps(CONV_W - 1, D_CONV), const(CONV_W, D_CONV), const(1, D_CONV),
                  const(1, D_CONV), const(1, D_CONV),
                  srow(D_XBC), taps(SSM_CONV_W - 1, D_XBC), const(SSM_CONV_W, D_XBC),
                  const(1, D_XBC), srow(HEAD_PAD), const(1, HEAD_PAD), const(1, D_SSM)],
        out_specs=[orow(D_CONV), taps(CONV_W - 1, D_CONV), taps(SSM_CONV_W - 1, D_XBC),
                   orow(D_SSM), orow(D_XBC - D_SSM), orow(HEAD_PAD), orow(D_SSM)],
        out_shape=[jax.ShapeDtypeStruct((n_samp, D_CONV), BF16),
                   jax.ShapeDtypeStruct((CONV_W - 1, n_samp, D_CONV), F32),
                   jax.ShapeDtypeStruct((SSM_CONV_W - 1, n_samp, D_XBC), F32),
                   jax.ShapeDtypeStruct((n_samp, D_SSM), F32),
                   jax.ShapeDtypeStruct((n_samp, D_XBC - D_SSM), F32),
                   jax.ShapeDtypeStruct((n_samp, HEAD_PAD), F32),
                   jax.ShapeDtypeStruct((n_samp, D_SSM), F32)],
        compiler_params=_params(("arbitrary",)), name="sample_prep",
    )(glu, cst_t, mw, mb, lg, lb, xbc, xst_t, cw, cb, dt, a_pad, d_exp)
    dec_flat = dec[:, :N_HEADS].reshape(-1)
    return (c_samp, jnp.swapaxes(ncst_t, 0, 1), jnp.swapaxes(nxst_t, 0, 1), xdt, bc, dec_flat, skip)


def _outproj_kernel(c_ref, cs_ref, y_ref, ys_ref, w_ref, h_ref, g_ref, o_ref):
    i = pl.program_id(0)
    last = pl.num_programs(0) - 1
    na = TM - cs_ref.shape[0]

    def run(c, y):
        mix = _dot(c, w_ref[0:D_CONV, :]) + _dot(y, w_ref[D_CONV:, :])
        o_ref[...] = h_ref[...] + _rms(mix, g_ref[...])

    @pl.when(i != last)
    def _():
        run(c_ref[...], y_ref[...])

    @pl.when(i == last)
    def _():
        run(jnp.concatenate([c_ref[0:na, :], cs_ref[...]], axis=0),
            jnp.concatenate([y_ref[0:na, :], ys_ref[...]], axis=0))


def _out_proj(c_prompt, c_samp, y_prompt, y_samp, w_out, h, g):
    m = h.shape[0]
    n_samp = c_samp.shape[0]
    row = pl.BlockSpec((TM, D_MODEL), lambda i: (i, 0))
    return pl.pallas_call(
        _outproj_kernel, grid=(m // TM,),
        in_specs=[pl.BlockSpec((TM, D_CONV), lambda i: (i, 0)),
                  pl.BlockSpec((n_samp, D_CONV), lambda i: (0, 0)),
                  pl.BlockSpec((TM, D_SSM), lambda i: (i, 0)),
                  pl.BlockSpec((n_samp, D_SSM), lambda i: (0, 0)),
                  pl.BlockSpec((D_CONV + D_SSM, D_MODEL), lambda i: (0, 0)),
                  row, pl.BlockSpec((1, D_MODEL), lambda i: (0, 0))],
        out_specs=row, out_shape=jax.ShapeDtypeStruct((m, D_MODEL), F32),
        compiler_params=_params(("arbitrary",)), name="out_proj",
    )(c_prompt, c_samp, y_prompt, y_samp, w_out, h, g)


def _ple_kernel(h_ref, pa_ref, pb_ref, gpre_ref, wg_ref, wp_ref, gpost_ref, oa_ref, ob_ref, emb_scr):
    i = pl.program_id(0)

    def embed(rows, src, srows):
        emb_scr[rows, :] = _dot(src[srows, :].astype(BF16), wp_ref[...])
    _on_tile_rows(i, pa_ref, pb_ref, embed)

    h = h_ref[...]
    gate = jax.nn.sigmoid(_dot(_rms(h, gpre_ref[...]).astype(BF16), wg_ref[...]))
    oa_ref[...] = h + _rms(gate * emb_scr[...], gpost_ref[...])

    @pl.when(i == pl.num_programs(0) - 1)
    def _():
        nb = ob_ref.shape[0]
        ob_ref[...] = oa_ref[TM - nb:TM, :]


def _ple(h, p_prompt, p_samp, gpre, wg, wp, gpost):
    m = h.shape[0]
    n_prompt, n_samp = p_prompt.shape[0], p_samp.shape[0]
    row = pl.BlockSpec((TM, D_MODEL), lambda i: (i, 0))
    vec = pl.BlockSpec((1, D_MODEL), lambda i: (0, 0))
    return pl.pallas_call(
        _ple_kernel, grid=(m // TM,),
        in_specs=[row, pl.BlockSpec((TM, PLE_DIM), lambda i: (i, 0)),
                  pl.BlockSpec((n_samp, PLE_DIM), lambda i: (0, 0)), vec,
                  pl.BlockSpec((D_MODEL, D_MODEL), lambda i: (0, 0)),
                  pl.BlockSpec((PLE_DIM, D_MODEL), lambda i: (0, 0)), vec],
        out_specs=[row, pl.BlockSpec((n_samp, D_MODEL), lambda i: (0, 0))],
        out_shape=[jax.ShapeDtypeStruct((n_prompt, D_MODEL), F32),
                   jax.ShapeDtypeStruct((n_samp, D_MODEL), F32)],
        scratch_shapes=[pltpu.VMEM((TM, D_MODEL), F32)],
        compiler_params=_params(("arbitrary",)), name="ple")(h, p_prompt, p_samp, gpre, wg, wp, gpost)


def _layer(x_prompt, x_samp, p_prompt, p_samp, n_batch, seq, cst, xst, h0, lw):
    (norm_ffn1_pre, w_ffn1_gate, w_ffn1_up, w_ffn1_down, norm_ffn1_post,
     norm_mix_pre, w_in, conv_mod_w, conv_mod_b, conv_mod_ln_g, conv_mod_ln_b,
     ssm_conv_w, ssm_conv_b, dt_bias, a_log, d_skip, ssm_norm_g, w_out, norm_mix_post,
     norm_ffn2_pre, w_ffn2_gate, w_ffn2_up, w_ffn2_down, norm_ffn2_post,
     norm_ple_pre, w_ple_gate, w_ple_proj, norm_ple_post) = lw
    n_prompt, n_samp = x_prompt.shape[0], x_samp.shape[0]
    m = n_prompt + n_samp
    _check_split(n_prompt, n_samp)
    row2 = lambda t: t.reshape(1, -1)

    later = ((jnp.swapaxes(w_in, 0, 1), 128, True), (w_out, 64, False), (w_ffn2_gate, 32, False),
             (w_ffn2_up, 32, False), (w_ffn2_down, 64, False), (w_ple_gate, 32, False),
             (w_ple_proj, 16, False))
    ffn1 = (row2(norm_ffn1_pre), row2(norm_ffn1_post), row2(norm_mix_pre))
    h1, u, wg1, wu1, wd1 = _ffn(
        x_prompt, ffn1[0], w_ffn1_gate, w_ffn1_up, w_ffn1_down, ffn1[1], gnext=ffn1[2],
        m=m, tiles=(0, 1), emit_w16=True, tf=TF_FIRST, tm=FIRST_TILES * TM, name="ffn_first")
    h1, u, w_in16, w_out16, wg2, wu2, wd2, wpg, wpp = _ffn(
        x_prompt, ffn1[0], wg1, wu1, wd1, ffn1[1], gnext=ffn1[2], x_tail=x_samp, casts=later,
        m=m, tiles=(FIRST_TILES, m // TM - FIRST_TILES), carry=(h1, u), name="ffn_rest")

    d_proj = w_in.shape[1]
    w_dt = jnp.pad(w_in16[:, d_proj - N_HEADS:], ((0, 0), (0, HEAD_PAD - N_HEADS)))
    pad_h = lambda t: jnp.pad(t.astype(F32), (0, HEAD_PAD - N_HEADS)).reshape(1, HEAD_PAD)
    glu, z, xbc, dt = _in_proj(u, w_in16, w_dt, pad_h(dt_bias))

    a = -jnp.exp(a_log.astype(F32))
    a_pad = pad_h(a)
    d_exp = row2(jnp.repeat(d_skip.astype(F32), HEAD_DIM))
    cw, cb = ssm_conv_w, row2(ssm_conv_b)
    mw, mb, lg, lb = conv_mod_w, row2(conv_mod_b), row2(conv_mod_ln_g), row2(conv_mod_ln_b)
    ng = row2(ssm_norm_g)

    c_prompt, glu_tail = _prompt_conv(glu, n_batch, seq, mw, mb, lg, lb)
    c_samp, new_cst_samp, new_xst_samp, xdt, bc, dec, skip = _sample_prep(
        glu, xbc, dt, n_prompt, n_samp, cst, xst, mw, mb, lg, lb, cw, cb, a_pad, d_exp)
    y_mix_prompt, hfin_prompt, xbc_tail, hn, y_mix_samp = _ssd(
        xbc, z, dt, n_batch, seq, cw, cb, a_pad, d_exp, ng,
        dec, h0.reshape(n_samp, D_SSM, D_STATE), xdt, bc, skip)

    h2 = _out_proj(c_prompt, c_samp, y_mix_prompt, y_mix_samp, w_out16, h1, row2(norm_mix_post))
    (h3,) = _ffn(h2, row2(norm_ffn2_pre), wg2, wu2, wd2, row2(norm_ffn2_post))
    y_prompt, y_samp = _ple(h3, p_prompt, p_samp, row2(norm_ple_pre), wpg, wpp, row2(norm_ple_post))

    new_cst_prompt = glu_tail[:, CARRY - (CONV_W - 1):]
    new_xst_prompt = xbc_tail[:, XCARRY - (SSM_CONV_W - 1):]
    new_h_prompt = hfin_prompt.reshape(n_batch, N_HEADS, HEAD_DIM, D_STATE)
    new_h_samp = hn.reshape(n_samp, N_HEADS, HEAD_DIM, D_STATE)
    return (y_prompt, y_samp, new_cst_prompt, new_xst_prompt, new_h_prompt,
            new_cst_samp, new_xst_samp, new_h_samp)


def kernel(x_prompt, x_sample, state_conv_mod, state_ssm_conv, state_ssm, p_prompt, p_sample,
           norm_ffn1_pre, w_ffn1_gate, w_ffn1_up, w_ffn1_down, norm_ffn1_post,
           norm_mix_pre, w_in, conv_mod_w, conv_mod_b, conv_mod_ln_g, conv_mod_ln_b,
           ssm_conv_w, ssm_conv_b, dt_bias, a_log, d_skip, ssm_norm_g, w_out, norm_mix_post,
           norm_ffn2_pre, w_ffn2_gate, w_ffn2_up, w_ffn2_down, norm_ffn2_post,
           norm_ple_pre, w_ple_gate, w_ple_proj, norm_ple_post):
    weights = (norm_ffn1_pre, w_ffn1_gate, w_ffn1_up, w_ffn1_down, norm_ffn1_post,
               norm_mix_pre, w_in, conv_mod_w, conv_mod_b, conv_mod_ln_g, conv_mod_ln_b,
               ssm_conv_w, ssm_conv_b, dt_bias, a_log, d_skip, ssm_norm_g, w_out, norm_mix_post,
               norm_ffn2_pre, w_ffn2_gate, w_ffn2_up, w_ffn2_down, norm_ffn2_post,
               norm_ple_pre, w_ple_gate, w_ple_proj, norm_ple_post)
    n_batch, seq, _ = x_prompt.shape
    n_samp = x_sample.shape[0]
    n_prompt = n_batch * seq
    depth = norm_ffn1_pre.shape[0]
    xp = x_prompt.reshape(n_prompt, D_MODEL)
    xs = x_sample.reshape(n_samp, D_MODEL)
    outs = [[] for _ in range(6)]
    for i in range(depth):
        res = _layer(xp, xs, p_prompt[i].reshape(n_prompt, PLE_DIM), p_sample[i].reshape(n_samp, PLE_DIM),
                     n_batch, seq, state_conv_mod[i], state_ssm_conv[i], state_ssm[i],
                     tuple(w[i] for w in weights))
        xp, xs = res[0], res[1]
        for lst, r in zip(outs, res[2:]):
            lst.append(r)
    return ((xp.reshape(n_batch, seq, D_MODEL), xs.reshape(n_samp, 1, D_MODEL))
            + tuple(jnp.stack(lst, axis=0) for lst in outs))
```

```python
import functools

import jax
import jax.numpy as jnp
from jax import lax
from jax.experimental import pallas as pl
from jax.experimental.pallas import tpu as pltpu

F32 = jnp.float32
BF16 = jnp.bfloat16

D_MODEL = 2048
D_FF = 5632
D_CONV = 1024
D_SSM = 3072
N_HEADS = 48
HEAD_DIM = 64
N_GROUPS = 8
HEADS_PER_GROUP = 6
GROUP_W = HEADS_PER_GROUP * HEAD_DIM
D_STATE = 128
D_XBC = D_SSM + 2 * N_GROUPS * D_STATE
CONV_W = 31
SSM_CONV_W = 4
CHUNK = 128
PLE_DIM = 256
EPS = 1e-6
NEG_BIG = -1e30
HEAD_PAD = 128

TM = 640
TM_FFN2 = 1040
TF = 512
TF_FIRST = 256
FIRST_TILES = 2
TMP = 832
TN = 1024
TL = 256
CARRY = 32
CONV_ROWS = 64
NORM_ROWS = 16
XCARRY = 8
SAMPLE_BLOCK = 32
VMEM_LIMIT = 56 * 1024 * 1024


def _params(dims, vmem=VMEM_LIMIT):
    return pltpu.CompilerParams(dimension_semantics=dims, vmem_limit_bytes=vmem)


def _rms(x, g):
    return x * lax.rsqrt(jnp.mean(x * x, axis=-1, keepdims=True) + EPS) * g


def _silu(x):
    return x * jax.nn.sigmoid(x)


def _dot(a, b):
    return jnp.dot(a, b, preferred_element_type=F32)


def _split3(x):
    hi = x.astype(BF16)
    r = x - hi.astype(F32)
    mid = r.astype(BF16)
    lo = (r - mid.astype(F32)).astype(BF16)
    return hi, mid, lo


def _dot3_rhs(a3_bf16, x):
    return _dot(a3_bf16, jnp.concatenate(_split3(x), axis=0))


def _dot3_lhs(x, b3_bf16):
    return _dot(jnp.concatenate(_split3(x), axis=1), b3_bf16)


def _head_expand_matrix():
    head = lax.broadcasted_iota(jnp.int32, (3 * HEAD_PAD, D_SSM), 0) & (HEAD_PAD - 1)
    chan = lax.broadcasted_iota(jnp.int32, (3 * HEAD_PAD, D_SSM), 1)
    return jnp.where((chan >> 6) == head, 1.0, 0.0).astype(BF16)


def _on_tile_rows(i, a_ref, b_ref, fn):
    tm = a_ref.shape[0]
    if b_ref is None:
        fn(slice(0, tm), a_ref, slice(0, tm))
        return
    nb = b_ref.shape[0]
    na = tm - nb
    last = pl.num_programs(0) - 1

    @pl.when(i != last)
    def _():
        fn(slice(0, tm), a_ref, slice(0, tm))

    @pl.when(i == last)
    def _():
        fn(slice(0, na), a_ref, slice(0, na))
        fn(slice(na, tm), b_ref, slice(0, nb))


def _check_split(n_prompt, n_samp):
    assert (n_prompt + n_samp) % TM == 0 and n_samp < TM and n_samp % 16 == 0


def _ffn_kernel(*refs, split, n_next, cast_t, emit_w16, n_carry):
    n_cast = len(cast_t)
    refs = list(refs)
    xa_ref = refs.pop(0)
    xb_ref = refs.pop(0) if split else None
    gpre_ref, wg_ref, wu_ref, wd_ref, gpost_ref = refs[:5]
    refs = refs[5:]
    gnext_ref = refs.pop(0) if n_next else None
    cast_in, refs = refs[:n_cast], refs[n_cast:]
    refs = refs[n_carry:]
    o_ref = refs.pop(0)
    unext_ref = refs.pop(0) if n_next else None
    cast_out, refs = refs[:n_cast], refs[n_cast:]
    w16_refs, refs = (refs[:3], refs[3:]) if emit_w16 else ((), refs)
    (u_scr,) = refs
    i = pl.program_id(0)
    j = pl.program_id(1)

    @pl.when(j == 0)
    def _():
        def pre(rows, src, srows):
            u_scr[rows, :] = _rms(src[srows, :], gpre_ref[...]).astype(BF16)
        _on_tile_rows(i, xa_ref, xb_ref, pre)
        o_ref[...] = jnp.zeros_like(o_ref)

    for ci, co, transpose in zip(cast_in, cast_out, cast_t):
        co[...] = (ci[...].T if transpose else ci[...]).astype(BF16)

    wg, wu, wd = wg_ref[...], wu_ref[...], wd_ref[...]
    if emit_w16:
        wg, wu, wd = wg.astype(BF16), wu.astype(BF16), wd.astype(BF16)
        for ref, w in zip(w16_refs, (wg, wu, wd)):
            ref[...] = w
    u = u_scr[...]
    act = (_silu(_dot(u, wg)) * _dot(u, wu)).astype(BF16)
    o_ref[...] += _dot(act, wd)

    @pl.when(j == pl.num_programs(1) - 1)
    def _():
        def post(rows, src, srows):
            step = rows.stop - rows.start if n_next else NORM_ROWS
            for r in range(0, rows.stop - rows.start, step):
                dst = slice(rows.start + r, rows.start + r + step)
                h = src[srows.start + r:srows.start + r + step, :] + _rms(o_ref[dst, :], gpost_ref[...])
                o_ref[dst, :] = h
                if n_next:
                    unext_ref[dst, :] = _rms(h, gnext_ref[...]).astype(BF16)
        _on_tile_rows(i, xa_ref, xb_ref, post)


def _ffn(x, gpre, wg, wu, wd, gpost, gnext=None, x_tail=None, casts=(), m=None, tiles=None,
         carry=(), emit_w16=False, tf=TF, tm=TM, name="ffn"):
    if m is None:
        m = x.shape[0] + (0 if x_tail is None else x_tail.shape[0])
    first, count = tiles if tiles is not None else (0, m // tm)
    grid = (count, D_FF // tf)
    steps = grid[0] * grid[1]
    row = pl.BlockSpec((tm, D_MODEL), lambda i, j: (i + first, 0))
    vec = pl.BlockSpec((1, D_MODEL), lambda i, j: (0, 0))
    wcol = pl.BlockSpec((D_MODEL, tf), lambda i, j: (0, j))
    wrow = pl.BlockSpec((tf, D_MODEL), lambda i, j: (j, 0))
    in_specs, args = [row], [x]
    if x_tail is not None:
        in_specs.append(pl.BlockSpec(x_tail.shape, lambda i, j: (0, 0)))
        args.append(x_tail)
    in_specs += [vec, wcol, wcol, wrow, vec]
    args += [gpre, wg, wu, wd, 0.5 * gpost]
    out_shape = [jax.ShapeDtypeStruct((m, D_MODEL), F32)]
    out_specs = [row]
    if gnext is not None:
        in_specs.append(vec)
        args.append(gnext)
        out_shape.append(jax.ShapeDtypeStruct((m, D_MODEL), BF16))
        out_specs.append(row)
    assert len(carry) in (0, len(out_shape))
    for w, r, transpose in casts:
        nblk = pl.cdiv(w.shape[0], r)
        assert r % 16 == 0 and nblk <= steps and (w.shape[0] % r == 0 or transpose)
        slab = lambda i, j, nblk=nblk: jnp.minimum(i * grid[1] + j, nblk - 1)
        in_specs.append(pl.BlockSpec((r, w.shape[1]), lambda i, j, slab=slab: (slab(i, j), 0)))
        args.append(w)
        if transpose:
            assert r % 128 == 0
            out_specs.append(pl.BlockSpec((w.shape[1], r), lambda i, j, slab=slab: (0, slab(i, j))))
            out_shape.append(jax.ShapeDtypeStruct(w.shape[::-1], BF16))
        else:
            out_specs.append(pl.BlockSpec((r, w.shape[1]), lambda i, j, slab=slab: (slab(i, j), 0)))
            out_shape.append(jax.ShapeDtypeStruct(w.shape, BF16))
    aliases = {}
    for k, c in enumerate(carry):
        aliases[len(args)] = k
        in_specs.append(pl.BlockSpec(memory_space=pl.ANY))
        args.append(c)
    if emit_w16:
        out_specs += [wcol, wcol, wrow]
        out_shape += [jax.ShapeDtypeStruct(w.shape, BF16) for w in (wg, wu, wd)]
    return pl.pallas_call(
        functools.partial(_ffn_kernel, split=x_tail is not None, n_next=gnext is not None,
                          cast_t=tuple(t for _, _, t in casts), emit_w16=emit_w16,
                          n_carry=len(carry)),
        grid=grid, in_specs=in_specs, out_specs=out_specs, out_shape=out_shape,
        scratch_shapes=[pltpu.VMEM((tm, D_MODEL), BF16)],
        input_output_aliases=aliases,
        compiler_params=_params(("arbitrary", "arbitrary")), name=name,
    )(*args)


N_GLU_STEPS = D_CONV // TN
N_Z_STEPS = D_SSM // TN
N_XBC_STEPS = D_XBC // TN


def _inproj_kernel(u_ref, w_ref, wb_ref, wdt_ref, dtb_ref, glu_ref, z_ref, xbc_ref, dt_ref):
    j = pl.program_id(1)

    @pl.when(j < N_GLU_STEPS)
    def _():
        u = u_ref[...]
        glu_ref[...] = _dot(u, w_ref[...]) * jax.nn.sigmoid(_dot(u, wb_ref[...]))

    @pl.when(jnp.logical_and(j >= N_GLU_STEPS, j < N_GLU_STEPS + N_Z_STEPS))
    def _():
        z_ref[...] = _dot(u_ref[...], w_ref[...])

    @pl.when(j >= N_GLU_STEPS + N_Z_STEPS)
    def _():
        xbc_ref[...] = _dot(u_ref[...], w_ref[...])

    @pl.when(j == pl.num_programs(1) - 1)
    def _():
        x = _dot(u_ref[...], wdt_ref[...]) + dtb_ref[...]
        dt_ref[...] = jnp.maximum(x, 0.0) + jnp.log1p(jnp.exp(-jnp.abs(x)))


def _in_proj(u, w_in, w_dt, dt_bias):
    m = u.shape[0]
    assert m % TMP == 0
    g, nz, nx = N_GLU_STEPS, N_Z_STEPS, N_XBC_STEPS
    return pl.pallas_call(
        _inproj_kernel, grid=(m // TMP, g + nz + nx),
        in_specs=[pl.BlockSpec((TMP, D_MODEL), lambda i, j: (i, 0)),
                  pl.BlockSpec((D_MODEL, TN), lambda i, j: (0, jnp.where(j < g, j, j + g))),
                  pl.BlockSpec((D_MODEL, TN), lambda i, j: (0, g + jnp.minimum(j, g - 1))),
                  pl.BlockSpec((D_MODEL, HEAD_PAD), lambda i, j: (0, 0)),
                  pl.BlockSpec((1, HEAD_PAD), lambda i, j: (0, 0))],
        out_specs=[pl.BlockSpec((TMP, TN), lambda i, j: (i, jnp.minimum(j, g - 1))),
                   pl.BlockSpec((TMP, TN), lambda i, j: (i, jnp.clip(j - g, 0, nz - 1))),
                   pl.BlockSpec((TMP, TN), lambda i, j: (i, jnp.clip(j - g - nz, 0, nx - 1))),
                   pl.BlockSpec((TMP, HEAD_PAD), lambda i, j: (i, 0))],
        out_shape=[jax.ShapeDtypeStruct((m, D_CONV), F32), jax.ShapeDtypeStruct((m, D_SSM), F32),
                   jax.ShapeDtypeStruct((m, D_XBC), F32), jax.ShapeDtypeStruct((m, HEAD_PAD), F32)],
        compiler_params=_params(("arbitrary", "arbitrary")), name="in_proj",
    )(u, w_in, w_in, w_dt, dt_bias)


def _ln_swish(y, g, b):
    mu = jnp.mean(y, axis=-1, keepdims=True)
    yc = y - mu
    yn = yc * lax.rsqrt(jnp.mean(yc * yc, axis=-1, keepdims=True) + EPS) * g + b
    return _silu(yn)


def _pconv_kernel(v_ref, w_ref, b_ref, lg_ref, lb_ref, o_ref, tail_ref, xpad_scr, conv_scr):
    t = pl.program_id(1)

    @pl.when(t == 0)
    def _():
        xpad_scr[0:CARRY, :] = jnp.zeros((CARRY, D_CONV), F32)

    xpad_scr[CARRY:CARRY + TL, :] = v_ref[...]
    first = CARRY - (CONV_W - 1)
    hb = CONV_ROWS
    n_rows = CARRY + TL
    for cb in range(D_CONV // 128):
        lanes = slice(cb * 128, (cb + 1) * 128)
        xfull = xpad_scr[:, lanes]
        conv_scr[:, lanes] = jnp.broadcast_to(b_ref[:, lanes], (TL, 128))
        for phase in range(8):
            xs = pltpu.roll(xfull, n_rows - phase, axis=0) if phase else xfull
            taps = [k for k in range(CONV_W) if (first + k) % 8 == phase]
            for base in range(0, TL, hb):
                acc = conv_scr[base:base + hb, lanes]
                for k in taps:
                    off = 8 * ((first + k) // 8)
                    acc = acc + w_ref[k:k + 1, lanes] * xs[base + off:base + off + hb]
                conv_scr[base:base + hb, lanes] = acc
    xpad_scr[0:CARRY, :] = xpad_scr[TL:TL + CARRY, :]
    for r in range(0, TL, NORM_ROWS):
        rows = slice(r, r + NORM_ROWS)
        o_ref[rows, :] = _ln_swish(conv_scr[rows, :], lg_ref[...], lb_ref[...]).astype(BF16)

    @pl.when(t == pl.num_programs(1) - 1)
    def _():
        tail_ref[0] = xpad_scr[0:CARRY, :]


def _prompt_conv(v, n_batch, seq, w, b, lg, lb):
    vec = pl.BlockSpec((1, D_CONV), lambda bi, t: (0, 0))
    steps = seq // TL
    return pl.pallas_call(
        _pconv_kernel, grid=(n_batch, steps),
        in_specs=[pl.BlockSpec((TL, D_CONV), lambda bi, t: (bi * steps + t, 0)),
                  pl.BlockSpec((CONV_W, D_CONV), lambda bi, t: (0, 0)), vec, vec, vec],
        out_specs=[pl.BlockSpec((TL, D_CONV), lambda bi, t: (bi * steps + t, 0)),
                   pl.BlockSpec((1, CARRY, D_CONV), lambda bi, t: (bi, 0, 0))],
        out_shape=[jax.ShapeDtypeStruct((n_batch * seq, D_CONV), BF16),
                   jax.ShapeDtypeStruct((n_batch, CARRY, D_CONV), F32)],
        scratch_shapes=[pltpu.VMEM((CARRY + TL, D_CONV), F32), pltpu.VMEM((TL, D_CONV), F32)],
        compiler_params=_params(("arbitrary", "arbitrary")), name="prompt_conv")(v, w, b, lg, lb)


def _gated_norm(y, z, g):
    yg = y * _silu(z)
    return yg * lax.rsqrt(jnp.mean(yg * yg, axis=-1, keepdims=True) + EPS) * g


def _sample_state_update(step, n_steps, dec_ref, h0_ref, xdt_ref, bc_ref, skip_ref, z_ref, ng_ref,
                         hn_ref, y_ref, yt_scr, xdt_t_scr):
    per_step = h0_ref.shape[0]
    nb = xdt_ref.shape[0]

    @pl.when(step == 0)
    def _():
        yt_scr[...] = jnp.zeros_like(yt_scr)
        xdt_t_scr[...] = xdt_ref[...].T.astype(BF16)

    seq_i = lax.broadcasted_iota(jnp.int32, (nb, D_STATE), 0)
    seqs = [step * per_step + bb for bb in range(per_step)]
    brows = [bc_ref[pl.ds(b, 1), :] for b in seqs]

    def one_hot_rows(g):
        cols = slice(g * D_STATE, (g + 1) * D_STATE)
        return jnp.concatenate([jnp.where(seq_i == b, brow[:, cols], 0.0).astype(BF16)
                                for b, brow in zip(seqs, brows)], axis=1)

    for g in range(N_GROUPS):
        rows = slice(g * GROUP_W, (g + 1) * GROUP_W)
        s_new = _dot(xdt_t_scr[rows, :], one_hot_rows(g))
        h_all = []
        for bb, b in enumerate(seqs):
            parts = []
            for r in range(HEADS_PER_GROUP):
                h = g * HEADS_PER_GROUP + r
                hr = slice(h * HEAD_DIM, (h + 1) * HEAD_DIM)
                parts.append(h0_ref[bb, hr, :] * dec_ref[b * N_HEADS + h]
                             + s_new[r * HEAD_DIM:(r + 1) * HEAD_DIM, bb * D_STATE:(bb + 1) * D_STATE])
            h_new = jnp.concatenate(parts, axis=0)
            hn_ref[bb, rows, :] = h_new
            h_all.append(h_new.astype(BF16))
        yt_scr[rows, :] += lax.dot_general(jnp.concatenate(h_all, axis=1), one_hot_rows(N_GROUPS + g),
                                           (((1,), (1,)), ((), ())), preferred_element_type=F32)

    @pl.when(step == n_steps - 1)
    def _():
        y = yt_scr[...].T + skip_ref[...]
        for g in range(N_GROUPS):
            ch = slice(g * GROUP_W, (g + 1) * GROUP_W)
            y_ref[:, ch] = _gated_norm(y[:, ch], z_ref[:, ch], ng_ref[:, ch]).astype(BF16)


def _pssd_kernel(xbc_ref, z_ref, dt_ref, cw_ref, cb_ref, a_ref, expand_ref, dexp_ref, ng_ref,
                 dec_ref, h0_ref, xdt_ref, bc_ref, skip_ref, zs_ref,
                 y_ref, hfin_ref, tail_ref, hn_ref, ys_ref,
                 state_scr, xpad_scr, xc_scr, yt_scr, xdt_t_scr):
    c = pl.program_id(1)
    q = CHUNK
    _sample_state_update(pl.program_id(0) * pl.num_programs(1) + c,
                         pl.num_programs(0) * pl.num_programs(1),
                         dec_ref, h0_ref, xdt_ref, bc_ref, skip_ref, zs_ref, ng_ref,
                         hn_ref, ys_ref, yt_scr, xdt_t_scr)

    @pl.when(c == 0)
    def _():
        state_scr[...] = jnp.zeros_like(state_scr)
        xpad_scr[0:XCARRY, :] = jnp.zeros((XCARRY, D_XBC), F32)

    xpad_scr[XCARRY:XCARRY + q, :] = xbc_ref[...]
    for cb in range(D_XBC // 512):
        lanes = slice(cb * 512, (cb + 1) * 512)
        acc = jnp.broadcast_to(cb_ref[:, lanes], (q, 512))
        for j in range(SSM_CONV_W):
            k = SSM_CONV_W - 1 - j
            acc = acc + cw_ref[k:k + 1, lanes] * xpad_scr[XCARRY - j:XCARRY - j + q, lanes]
        xc_scr[:, lanes] = _silu(acc)
    xpad_scr[0:XCARRY, :] = xpad_scr[q:q + XCARRY, :]

    row_i = lax.broadcasted_iota(jnp.int32, (q, q), 0)
    col_i = lax.broadcasted_iota(jnp.int32, (q, q), 1)
    tril = row_i >= col_i
    tri = jnp.where(tril, 1.0, 0.0).astype(BF16)
    expand = expand_ref[...]

    dt = dt_ref[...]
    tri3 = jnp.concatenate([tri, tri, tri], axis=1)
    a_cs = _dot3_rhs(tri3, dt * a_ref[...])
    a_cs_t = a_cs.T
    dt_exp = _dot3_lhs(dt, expand)
    acs_exp = _dot3_lhs(a_cs, expand)
    last = acs_exp[q - 1:q, :]
    lane_lo = lax.broadcasted_iota(jnp.int32, (q, 128), 1) < HEAD_DIM

    for g in range(N_GROUPS):
        ch = slice(g * GROUP_W, (g + 1) * GROUP_W)
        xs = xc_scr[:, ch]
        bg = xc_scr[:, D_SSM + g * D_STATE:D_SSM + (g + 1) * D_STATE]
        cg = xc_scr[:, D_SSM + (N_GROUPS + g) * D_STATE:D_SSM + (N_GROUPS + g + 1) * D_STATE]
        bg16 = bg.astype(BF16)
        cg16 = cg.astype(BF16)
        xdt = xs * dt_exp[:, ch]
        acs_g = acs_exp[:, ch]
        cb = lax.dot_general(cg16, bg16, (((1,), (1,)), ((), ())), preferred_element_type=F32)
        st = state_scr[:, ch]
        y = _dot(cg16, st.astype(BF16)) * jnp.exp(acs_g)
        pieces = []
        for pr in range(HEADS_PER_GROUP // 2):
            xpair = xdt[:, pr * 128:(pr + 1) * 128].astype(BF16)
            both = []
            for half in range(2):
                h = g * HEADS_PER_GROUP + 2 * pr + half
                seg = a_cs[:, h:h + 1] - a_cs_t[h:h + 1, :]
                decay = jnp.exp(jnp.where(tril, seg, NEG_BIG))
                both.append(_dot((cb * decay).astype(BF16), xpair))
            pieces.append(jnp.where(lane_lo, both[0], both[1]))
        y = y + jnp.concatenate(pieces, axis=1) + dexp_ref[:, ch] * xs
        xdec = (xdt * jnp.exp(last[:, ch] - acs_g)).astype(BF16)
        s_new = lax.dot_general(bg16, xdec, (((0,), (0,)), ((), ())), preferred_element_type=F32)
        state_scr[:, ch] = st * jnp.exp(last[:, ch]) + s_new
        y_ref[:, ch] = _gated_norm(y, z_ref[:, ch], ng_ref[:, ch]).astype(BF16)

    @pl.when(c == pl.num_programs(1) - 1)
    def _():
        hfin_ref[0] = state_scr[...].T
        tail_ref[0] = xpad_scr[0:XCARRY, :]


def _ssd(xbc, z, dt, n_batch, seq, cw, cb, a_pad, d_exp, ng, dec, h0, xdt, bc, skip):
    nc = seq // CHUNK
    n_prompt = n_batch * seq
    n_samp = h0.shape[0]
    per_step = n_samp // (n_batch * nc)
    assert per_step * n_batch * nc == n_samp and n_prompt % n_samp == 0

    def rows(w):
        return pl.BlockSpec((CHUNK, w), lambda bi, c: (bi * nc + c, 0))

    def vec(*s):
        return pl.BlockSpec(s, lambda bi, c: (0,) * len(s))

    states = pl.BlockSpec((per_step, D_SSM, D_STATE), lambda bi, c: (bi * nc + c, 0, 0))
    return pl.pallas_call(
        _pssd_kernel, grid=(n_batch, nc),
        in_specs=[rows(D_XBC), rows(D_SSM), rows(HEAD_PAD), vec(SSM_CONV_W, D_XBC), vec(1, D_XBC),
                  vec(1, HEAD_PAD), vec(3 * HEAD_PAD, D_SSM), vec(1, D_SSM), vec(1, D_SSM),
                  pl.BlockSpec(memory_space=pltpu.SMEM), states, vec(n_samp, D_SSM),
                  vec(n_samp, D_XBC - D_SSM), vec(n_samp, D_SSM),
                  pl.BlockSpec((n_samp, D_SSM), lambda bi, c: (n_prompt // n_samp, 0))],
        out_specs=[rows(D_SSM), pl.BlockSpec((1, D_SSM, D_STATE), lambda bi, c: (bi, 0, 0)),
                   pl.BlockSpec((1, XCARRY, D_XBC), lambda bi, c: (bi, 0, 0)),
                   states, vec(n_samp, D_SSM)],
        out_shape=[jax.ShapeDtypeStruct((n_prompt, D_SSM), BF16),
                   jax.ShapeDtypeStruct((n_batch, D_SSM, D_STATE), F32),
                   jax.ShapeDtypeStruct((n_batch, XCARRY, D_XBC), F32),
                   jax.ShapeDtypeStruct((n_samp, D_SSM, D_STATE), F32),
                   jax.ShapeDtypeStruct((n_samp, D_SSM), BF16)],
        scratch_shapes=[pltpu.VMEM((D_STATE, D_SSM), F32), pltpu.VMEM((XCARRY + CHUNK, D_XBC), F32),
                        pltpu.VMEM((CHUNK, D_XBC), F32),
                        pltpu.VMEM((D_SSM, n_samp), F32), pltpu.VMEM((D_SSM, n_samp), BF16)],
        compiler_params=_params(("arbitrary", "arbitrary")), name="ssd",
    )(xbc, z, dt, cw, cb, a_pad, _head_expand_matrix(), d_exp, ng, dec, h0, xdt, bc, skip, z)


def _sprep_kernel(v_ref, cst_ref, w_ref, b_ref, lg_ref, lb_ref,
                  xbc_ref, xst_ref, cw_ref, cb_ref, dt_ref, a_ref, dexp_ref,
                  co_ref, ncst_ref, nxst_ref, xdt_ref, bc_ref, dec_ref, skip_ref):
    nb = v_ref.shape[0]
    v = v_ref[...]
    acc = jnp.broadcast_to(b_ref[...], (nb, D_CONV)) + w_ref[CONV_W - 1:CONV_W, :] * v
    for k in range(CONV_W - 1):
        acc = acc + w_ref[k:k + 1, :] * cst_ref[k]
    co_ref[...] = _ln_swish(acc, lg_ref[...], lb_ref[...]).astype(BF16)
    for k in range(CONV_W - 2):
        ncst_ref[k] = cst_ref[k + 1]
    ncst_ref[CONV_W - 2] = v

    xn = xbc_ref[...]
    acc = jnp.broadcast_to(cb_ref[...], (nb, D_XBC)) + cw_ref[SSM_CONV_W - 1:SSM_CONV_W, :] * xn
    for k in range(SSM_CONV_W - 1):
        acc = acc + cw_ref[k:k + 1, :] * xst_ref[k]
    xc = _silu(acc)
    for k in range(SSM_CONV_W - 2):
        nxst_ref[k] = xst_ref[k + 1]
    nxst_ref[SSM_CONV_W - 2] = xn

    xs = xc[:, :D_SSM]
    bc_ref[...] = xc[:, D_SSM:]
    dt = dt_ref[...]
    dec_ref[...] = jnp.exp(dt * a_ref[...])
    dt_exp = _dot3_lhs(dt, _head_expand_matrix())
    xdt_ref[...] = xs * dt_exp
    skip_ref[...] = dexp_ref[...] * xs


def _sample_prep(glu, xbc, dt, n_prompt, n_samp, cst, xst, mw, mb, lg, lb, cw, cb, a_pad, d_exp):
    sb = SAMPLE_BLOCK
    off = n_prompt // sb
    srow = lambda w: pl.BlockSpec((sb, w), lambda i: (off + i, 0))
    orow = lambda w: pl.BlockSpec((sb, w), lambda i: (i, 0))
    taps = lambda k, w: pl.BlockSpec((k, sb, w), lambda i: (0, i, 0))
    const = lambda *s: pl.BlockSpec(s, lambda i: (0,) * len(s))
    cst_t = jnp.swapaxes(cst, 0, 1)
    xst_t = jnp.swapaxes(xst, 0, 1)
    c_samp, ncst_t, nxst_t, xdt, bc, dec, skip = pl.pallas_call(
        _sprep_kernel, grid=(n_samp // sb,),
        in_specs=[srow(D_CONV), taps(CONV_W - 1, D_CONV), const(CONV_W, D_CONV), const(1, D_CONV),
                  const(1, D_CONV), const(1, D_CONV),
                  srow(D_XBC), taps(SSM_CONV_W - 1, D_XBC), const(SSM_CONV_W, D_XBC),
                  const(1, D_XBC), srow(HEAD_PAD), const(1, HEAD_PAD), const(1, D_SSM)],
        out_specs=[orow(D_CONV), taps(CONV_W - 1, D_CONV), taps(SSM_CONV_W - 1, D_XBC),
                   orow(D_SSM), orow(D_XBC - D_SSM), orow(HEAD_PAD), orow(D_SSM)],
        out_shape=[jax.ShapeDtypeStruct((n_samp, D_CONV), BF16),
                   jax.ShapeDtypeStruct((CONV_W - 1, n_samp, D_CONV), F32),
                   jax.ShapeDtypeStruct((SSM_CONV_W - 1, n_samp, D_XBC), F32),
                   jax.ShapeDtypeStruct((n_samp, D_SSM), F32),
                   jax.ShapeDtypeStruct((n_samp, D_XBC - D_SSM), F32),
                   jax.ShapeDtypeStruct((n_samp, HEAD_PAD), F32),
                   jax.ShapeDtypeStruct((n_samp, D_SSM), F32)],
        compiler_params=_params(("arbitrary",)), name="sample_prep",
    )(glu, cst_t, mw, mb, lg, lb, xbc, xst_t, cw, cb, dt, a_pad, d_exp)
    dec_flat = dec[:, :N_HEADS].reshape(-1)
    return (c_samp, jnp.swapaxes(ncst_t, 0, 1), jnp.swapaxes(nxst_t, 0, 1), xdt, bc, dec_flat, skip)


def _outproj_kernel(c_ref, cs_ref, y_ref, ys_ref, w_ref, h_ref, g_ref, o_ref):
    i = pl.program_id(0)
    last = pl.num_programs(0) - 1
    na = TM - cs_ref.shape[0]

    def run(c, y):
        mix = _dot(c, w_ref[0:D_CONV, :]) + _dot(y, w_ref[D_CONV:, :])
        o_ref[...] = h_ref[...] + _rms(mix, g_ref[...])

    @pl.when(i != last)
    def _():
        run(c_ref[...], y_ref[...])

    @pl.when(i == last)
    def _():
        run(jnp.concatenate([c_ref[0:na, :], cs_ref[...]], axis=0),
            jnp.concatenate([y_ref[0:na, :], ys_ref[...]], axis=0))


def _out_proj(c_prompt, c_samp, y_prompt, y_samp, w_out, h, g):
    m = h.shape[0]
    n_samp = c_samp.shape[0]
    row = pl.BlockSpec((TM, D_MODEL), lambda i: (i, 0))
    return pl.pallas_call(
        _outproj_kernel, grid=(m // TM,),
        in_specs=[pl.BlockSpec((TM, D_CONV), lambda i: (i, 0)),
                  pl.BlockSpec((n_samp, D_CONV), lambda i: (0, 0)),
                  pl.BlockSpec((TM, D_SSM), lambda i: (i, 0)),
                  pl.BlockSpec((n_samp, D_SSM), lambda i: (0, 0)),
                  pl.BlockSpec((D_CONV + D_SSM, D_MODEL), lambda i: (0, 0)),
                  row, pl.BlockSpec((1, D_MODEL), lambda i: (0, 0))],
        out_specs=row, out_shape=jax.ShapeDtypeStruct((m, D_MODEL), F32),
        compiler_params=_params(("arbitrary",)), name="out_proj",
    )(c_prompt, c_samp, y_prompt, y_samp, w_out, h, g)


def _ple_kernel(h_ref, pa_ref, pb_ref, gpre_ref, wg_ref, wp_ref, gpost_ref, oa_ref, ob_ref, emb_scr):
    i = pl.program_id(0)

    def embed(rows, src, srows):
        emb_scr[rows, :] = _dot(src[srows, :].astype(BF16), wp_ref[...])
    _on_tile_rows(i, pa_ref, pb_ref, embed)

    h = h_ref[...]
    gate = jax.nn.sigmoid(_dot(_rms(h, gpre_ref[...]).astype(BF16), wg_ref[...]))
    oa_ref[...] = h + _rms(gate * emb_scr[...], gpost_ref[...])

    @pl.when(i == pl.num_programs(0) - 1)
    def _():
        nb = ob_ref.shape[0]
        ob_ref[...] = oa_ref[TM - nb:TM, :]


def _ple(h, p_prompt, p_samp, gpre, wg, wp, gpost):
    m = h.shape[0]
    n_prompt, n_samp = p_prompt.shape[0], p_samp.shape[0]
    row = pl.BlockSpec((TM, D_MODEL), lambda i: (i, 0))
    vec = pl.BlockSpec((1, D_MODEL), lambda i: (0, 0))
    return pl.pallas_call(
        _ple_kernel, grid=(m // TM,),
        in_specs=[row, pl.BlockSpec((TM, PLE_DIM), lambda i: (i, 0)),
                  pl.BlockSpec((n_samp, PLE_DIM), lambda i: (0, 0)), vec,
                  pl.BlockSpec((D_MODEL, D_MODEL), lambda i: (0, 0)),
                  pl.BlockSpec((PLE_DIM, D_MODEL), lambda i: (0, 0)), vec],
        out_specs=[row, pl.BlockSpec((n_samp, D_MODEL), lambda i: (0, 0))],
        out_shape=[jax.ShapeDtypeStruct((n_prompt, D_MODEL), F32),
                   jax.ShapeDtypeStruct((n_samp, D_MODEL), F32)],
        scratch_shapes=[pltpu.VMEM((TM, D_MODEL), F32)],
        compiler_params=_params(("arbitrary",)), name="ple")(h, p_prompt, p_samp, gpre, wg, wp, gpost)


def _layer(x_prompt, x_samp, p_prompt, p_samp, n_batch, seq, cst, xst, h0, lw):
    (norm_ffn1_pre, w_ffn1_gate, w_ffn1_up, w_ffn1_down, norm_ffn1_post,
     norm_mix_pre, w_in, conv_mod_w, conv_mod_b, conv_mod_ln_g, conv_mod_ln_b,
     ssm_conv_w, ssm_conv_b, dt_bias, a_log, d_skip, ssm_norm_g, w_out, norm_mix_post,
     norm_ffn2_pre, w_ffn2_gate, w_ffn2_up, w_ffn2_down, norm_ffn2_post,
     norm_ple_pre, w_ple_gate, w_ple_proj, norm_ple_post) = lw
    n_prompt, n_samp = x_prompt.shape[0], x_samp.shape[0]
    m = n_prompt + n_samp
    _check_split(n_prompt, n_samp)
    row2 = lambda t: t.reshape(1, -1)

    later = ((jnp.swapaxes(w_in, 0, 1), 128, True), (w_out, 64, False), (w_ffn2_gate, 32, False),
             (w_ffn2_up, 32, False), (w_ffn2_down, 64, False), (w_ple_gate, 32, False),
             (w_ple_proj, 16, False))
    ffn1 = (row2(norm_ffn1_pre), row2(norm_ffn1_post), row2(norm_mix_pre))
    h1, u, wg1, wu1, wd1 = _ffn(
        x_prompt, ffn1[0], w_ffn1_gate, w_ffn1_up, w_ffn1_down, ffn1[1], gnext=ffn1[2],
        m=m, tiles=(0, 1), emit_w16=True, tf=TF_FIRST, tm=FIRST_TILES * TM, name="ffn_first")
    h1, u, w_in16, w_out16, wg2, wu2, wd2, wpg, wpp = _ffn(
        x_prompt, ffn1[0], wg1, wu1, wd1, ffn1[1], gnext=ffn1[2], x_tail=x_samp, casts=later,
        m=m, tiles=(FIRST_TILES, m // TM - FIRST_TILES), carry=(h1, u), name="ffn_rest")

    d_proj = w_in.shape[1]
    w_dt = jnp.pad(w_in16[:, d_proj - N_HEADS:], ((0, 0), (0, HEAD_PAD - N_HEADS)))
    pad_h = lambda t: jnp.pad(t.astype(F32), (0, HEAD_PAD - N_HEADS)).reshape(1, HEAD_PAD)
    glu, z, xbc, dt = _in_proj(u, w_in16, w_dt, pad_h(dt_bias))

    a = -jnp.exp(a_log.astype(F32))
    a_pad = pad_h(a)
    d_exp = row2(jnp.repeat(d_skip.astype(F32), HEAD_DIM))
    cw, cb = ssm_conv_w, row2(ssm_conv_b)
    mw, mb, lg, lb = conv_mod_w, row2(conv_mod_b), row2(conv_mod_ln_g), row2(conv_mod_ln_b)
    ng = row2(ssm_norm_g)

    c_prompt, glu_tail = _prompt_conv(glu, n_batch, seq, mw, mb, lg, lb)
    c_samp, new_cst_samp, new_xst_samp, xdt, bc, dec, skip = _sample_prep(
        glu, xbc, dt, n_prompt, n_samp, cst, xst, mw, mb, lg, lb, cw, cb, a_pad, d_exp)
    y_mix_prompt, hfin_prompt, xbc_tail, hn, y_mix_samp = _ssd(
        xbc, z, dt, n_batch, seq, cw, cb, a_pad, d_exp, ng,
        dec, h0.reshape(n_samp, D_SSM, D_STATE), xdt, bc, skip)

    h2 = _out_proj(c_prompt, c_samp, y_mix_prompt, y_mix_samp, w_out16, h1, row2(norm_mix_post))
    (h3,) = _ffn(h2, row2(norm_ffn2_pre), wg2, wu2, wd2, row2(norm_ffn2_post), tm=TM_FFN2)
    y_prompt, y_samp = _ple(h3, p_prompt, p_samp, row2(norm_ple_pre), wpg, wpp, row2(norm_ple_post))

    new_cst_prompt = glu_tail[:, CARRY - (CONV_W - 1):]
    new_xst_prompt = xbc_tail[:, XCARRY - (SSM_CONV_W - 1):]
    new_h_prompt = hfin_prompt.reshape(n_batch, N_HEADS, HEAD_DIM, D_STATE)
    new_h_samp = hn.reshape(n_samp, N_HEADS, HEAD_DIM, D_STATE)
    return (y_prompt, y_samp, new_cst_prompt, new_xst_prompt, new_h_prompt,
            new_cst_samp, new_xst_samp, new_h_samp)


def kernel(x_prompt, x_sample, state_conv_mod, state_ssm_conv, state_ssm, p_prompt, p_sample,
           norm_ffn1_pre, w_ffn1_gate, w_ffn1_up, w_ffn1_down, norm_ffn1_post,
           norm_mix_pre, w_in, conv_mod_w, conv_mod_b, conv_mod_ln_g, conv_mod_ln_b,
           ssm_conv_w, ssm_conv_b, dt_bias, a_log, d_skip, ssm_norm_g, w_out, norm_mix_post,
           norm_ffn2_pre, w_ffn2_gate, w_ffn2_up, w_ffn2_down, norm_ffn2_post,
           norm_ple_pre, w_ple_gate, w_ple_proj, norm_ple_post):
    weights = (norm_ffn1_pre, w_ffn1_gate, w_ffn1_up, w_ffn1_down, norm_ffn1_post,
               norm_mix_pre, w_in, conv_mod_w, conv_mod_b, conv_mod_ln_g, conv_mod_ln_b,
               ssm_conv_w, ssm_conv_b, dt_bias, a_log, d_skip, ssm_norm_g, w_out, norm_mix_post,
               norm_ffn2_pre, w_ffn2_gate, w_ffn2_up, w_ffn2_down, norm_ffn2_post,
               norm_ple_pre, w_ple_gate, w_ple_proj, norm_ple_post)
    n_batch, seq, _ = x_prompt.shape
    n_samp = x_sample.shape[0]
    n_prompt = n_batch * seq
    depth = norm_ffn1_pre.shape[0]
    xp = x_prompt.reshape(n_prompt, D_MODEL)
    xs = x_sample.reshape(n_samp, D_MODEL)
    outs = [[] for _ in range(6)]
    for i in range(depth):
        res = _layer(xp, xs, p_prompt[i].reshape(n_prompt, PLE_DIM), p_sample[i].reshape(n_samp, PLE_DIM),
                     n_batch, seq, state_conv_mod[i], state_ssm_conv[i], state_ssm[i],
                     tuple(w[i] for w in weights))
        xp, xs = res[0], res[1]
        for lst, r in zip(outs, res[2:]):
            lst.append(r)
    return ((xp.reshape(n_batch, seq, D_MODEL), xs.reshape(n_samp, 1, D_MODEL))
            + tuple(jnp.stack(lst, axis=0) for lst in outs))
```

```python
import functools

import jax
import jax.numpy as jnp
from jax import lax
from jax.experimental import pallas as pl
from jax.experimental.pallas import tpu as pltpu

F32 = jnp.float32
BF16 = jnp.bfloat16

D_MODEL = 2048
D_FF = 5632
D_CONV = 1024
D_SSM = 3072
N_HEADS = 48
HEAD_DIM = 64
N_GROUPS = 8
HEADS_PER_GROUP = 6
GROUP_W = HEADS_PER_GROUP * HEAD_DIM
D_STATE = 128
D_XBC = D_SSM + 2 * N_GROUPS * D_STATE
CONV_W = 31
SSM_CONV_W = 4
CHUNK = 128
PLE_DIM = 256
EPS = 1e-6
NEG_BIG = -1e30
HEAD_PAD = 128

TM = 640
TF = 512
TF_FIRST = 256
FIRST_TILES = 2
TMP = 832
TN = 1024
TL = 256
CARRY = 32
CONV_ROWS = 64
NORM_ROWS = 16
XCARRY = 8
SAMPLE_BLOCK = 32
VMEM_LIMIT = 56 * 1024 * 1024


def _params(dims, vmem=VMEM_LIMIT):
    return pltpu.CompilerParams(dimension_semantics=dims, vmem_limit_bytes=vmem)


def _rms(x, g):
    return x * lax.rsqrt(jnp.mean(x * x, axis=-1, keepdims=True) + EPS) * g


def _silu(x):
    return x * jax.nn.sigmoid(x)


def _dot(a, b):
    return jnp.dot(a, b, preferred_element_type=F32)


def _split3(x):
    hi = x.astype(BF16)
    r = x - hi.astype(F32)
    mid = r.astype(BF16)
    lo = (r - mid.astype(F32)).astype(BF16)
    return hi, mid, lo


def _dot3_rhs(a3_bf16, x):
    return _dot(a3_bf16, jnp.concatenate(_split3(x), axis=0))


def _dot3_lhs(x, b3_bf16):
    return _dot(jnp.concatenate(_split3(x), axis=1), b3_bf16)


def _head_expand_matrix():
    head = lax.broadcasted_iota(jnp.int32, (3 * HEAD_PAD, D_SSM), 0) & (HEAD_PAD - 1)
    chan = lax.broadcasted_iota(jnp.int32, (3 * HEAD_PAD, D_SSM), 1)
    return jnp.where((chan >> 6) == head, 1.0, 0.0).astype(BF16)


def _on_tile_rows(i, a_ref, b_ref, fn):
    tm = a_ref.shape[0]
    if b_ref is None:
        fn(slice(0, tm), a_ref, slice(0, tm))
        return
    nb = b_ref.shape[0]
    na = tm - nb
    last = pl.num_programs(0) - 1

    @pl.when(i != last)
    def _():
        fn(slice(0, tm), a_ref, slice(0, tm))

    @pl.when(i == last)
    def _():
        fn(slice(0, na), a_ref, slice(0, na))
        fn(slice(na, tm), b_ref, slice(0, nb))


def _check_split(n_prompt, n_samp):
    assert (n_prompt + n_samp) % TM == 0 and n_samp < TM and n_samp % 16 == 0


def _ffn_kernel(*refs, split, n_next, cast_t, emit_w16, n_carry):
    n_cast = len(cast_t)
    refs = list(refs)
    xa_ref = refs.pop(0)
    xb_ref = refs.pop(0) if split else None
    gpre_ref, wg_ref, wu_ref, wd_ref, gpost_ref = refs[:5]
    refs = refs[5:]
    gnext_ref = refs.pop(0) if n_next else None
    cast_in, refs = refs[:n_cast], refs[n_cast:]
    refs = refs[n_carry:]
    o_ref = refs.pop(0)
    unext_ref = refs.pop(0) if n_next else None
    cast_out, refs = refs[:n_cast], refs[n_cast:]
    w16_refs, refs = (refs[:3], refs[3:]) if emit_w16 else ((), refs)
    (u_scr,) = refs
    i = pl.program_id(0)
    j = pl.program_id(1)

    @pl.when(j == 0)
    def _():
        def pre(rows, src, srows):
            u_scr[rows, :] = _rms(src[srows, :], gpre_ref[...]).astype(BF16)
        _on_tile_rows(i, xa_ref, xb_ref, pre)
        o_ref[...] = jnp.zeros_like(o_ref)

    for ci, co, transpose in zip(cast_in, cast_out, cast_t):
        co[...] = (ci[...].T if transpose else ci[...]).astype(BF16)

    wg, wu, wd = wg_ref[...], wu_ref[...], wd_ref[...]
    if emit_w16:
        wg, wu, wd = wg.astype(BF16), wu.astype(BF16), wd.astype(BF16)
        for ref, w in zip(w16_refs, (wg, wu, wd)):
            ref[...] = w
    u = u_scr[...]
    act = (_silu(_dot(u, wg)) * _dot(u, wu)).astype(BF16)
    o_ref[...] += _dot(act, wd)

    @pl.when(j == pl.num_programs(1) - 1)
    def _():
        def post(rows, src, srows):
            step = rows.stop - rows.start if n_next else NORM_ROWS
            for r in range(0, rows.stop - rows.start, step):
                dst = slice(rows.start + r, rows.start + r + step)
                h = src[srows.start + r:srows.start + r + step, :] + _rms(o_ref[dst, :], gpost_ref[...])
                o_ref[dst, :] = h
                if n_next:
                    unext_ref[dst, :] = _rms(h, gnext_ref[...]).astype(BF16)
        _on_tile_rows(i, xa_ref, xb_ref, post)


def _ffn(x, gpre, wg, wu, wd, gpost, gnext=None, x_tail=None, casts=(), m=None, tiles=None,
         carry=(), emit_w16=False, tf=TF, tm=TM, name="ffn"):
    if m is None:
        m = x.shape[0] + (0 if x_tail is None else x_tail.shape[0])
    first, count = tiles if tiles is not None else (0, m // tm)
    grid = (count, D_FF // tf)
    steps = grid[0] * grid[1]
    row = pl.BlockSpec((tm, D_MODEL), lambda i, j: (i + first, 0))
    vec = pl.BlockSpec((1, D_MODEL), lambda i, j: (0, 0))
    wcol = pl.BlockSpec((D_MODEL, tf), lambda i, j: (0, j))
    wrow = pl.BlockSpec((tf, D_MODEL), lambda i, j: (j, 0))
    in_specs, args = [row], [x]
    if x_tail is not None:
        in_specs.append(pl.BlockSpec(x_tail.shape, lambda i, j: (0, 0)))
        args.append(x_tail)
    in_specs += [vec, wcol, wcol, wrow, vec]
    args += [gpre, wg, wu, wd, 0.5 * gpost]
    out_shape = [jax.ShapeDtypeStruct((m, D_MODEL), F32)]
    out_specs = [row]
    if gnext is not None:
        in_specs.append(vec)
        args.append(gnext)
        out_shape.append(jax.ShapeDtypeStruct((m, D_MODEL), BF16))
        out_specs.append(row)
    assert len(carry) in (0, len(out_shape))
    for w, r, transpose in casts:
        nblk = pl.cdiv(w.shape[0], r)
        assert r % 16 == 0 and nblk <= steps and (w.shape[0] % r == 0 or transpose)
        slab = lambda i, j, nblk=nblk: jnp.minimum(i * grid[1] + j, nblk - 1)
        in_specs.append(pl.BlockSpec((r, w.shape[1]), lambda i, j, slab=slab: (slab(i, j), 0)))
        args.append(w)
        if transpose:
            assert r % 128 == 0
            out_specs.append(pl.BlockSpec((w.shape[1], r), lambda i, j, slab=slab: (0, slab(i, j))))
            out_shape.append(jax.ShapeDtypeStruct(w.shape[::-1], BF16))
        else:
            out_specs.append(pl.BlockSpec((r, w.shape[1]), lambda i, j, slab=slab: (slab(i, j), 0)))
            out_shape.append(jax.ShapeDtypeStruct(w.shape, BF16))
    aliases = {}
    for k, c in enumerate(carry):
        aliases[len(args)] = k
        in_specs.append(pl.BlockSpec(memory_space=pl.ANY))
        args.append(c)
    if emit_w16:
        out_specs += [wcol, wcol, wrow]
        out_shape += [jax.ShapeDtypeStruct(w.shape, BF16) for w in (wg, wu, wd)]
    return pl.pallas_call(
        functools.partial(_ffn_kernel, split=x_tail is not None, n_next=gnext is not None,
                          cast_t=tuple(t for _, _, t in casts), emit_w16=emit_w16,
                          n_carry=len(carry)),
        grid=grid, in_specs=in_specs, out_specs=out_specs, out_shape=out_shape,
        scratch_shapes=[pltpu.VMEM((tm, D_MODEL), BF16)],
        input_output_aliases=aliases,
        compiler_params=_params(("arbitrary", "arbitrary")), name=name,
    )(*args)


N_GLU_STEPS = D_CONV // TN
N_Z_STEPS = D_SSM // TN
N_XBC_STEPS = D_XBC // TN


def _inproj_kernel(u_ref, w_ref, wb_ref, wdt_ref, dtb_ref, glu_ref, z_ref, xbc_ref, dt_ref):
    j = pl.program_id(1)

    @pl.when(j < N_GLU_STEPS)
    def _():
        u = u_ref[...]
        glu_ref[...] = _dot(u, w_ref[...]) * jax.nn.sigmoid(_dot(u, wb_ref[...]))

    @pl.when(jnp.logical_and(j >= N_GLU_STEPS, j < N_GLU_STEPS + N_Z_STEPS))
    def _():
        z_ref[...] = _dot(u_ref[...], w_ref[...])

    @pl.when(j >= N_GLU_STEPS + N_Z_STEPS)
    def _():
        xbc_ref[...] = _dot(u_ref[...], w_ref[...])

    @pl.when(j == pl.num_programs(1) - 1)
    def _():
        x = _dot(u_ref[...], wdt_ref[...]) + dtb_ref[...]
        dt_ref[...] = jnp.maximum(x, 0.0) + jnp.log1p(jnp.exp(-jnp.abs(x)))


def _in_proj(u, w_in, w_dt, dt_bias):
    m = u.shape[0]
    assert m % TMP == 0
    g, nz, nx = N_GLU_STEPS, N_Z_STEPS, N_XBC_STEPS
    return pl.pallas_call(
        _inproj_kernel, grid=(m // TMP, g + nz + nx),
        in_specs=[pl.BlockSpec((TMP, D_MODEL), lambda i, j: (i, 0)),
                  pl.BlockSpec((D_MODEL, TN), lambda i, j: (0, jnp.where(j < g, j, j + g))),
                  pl.BlockSpec((D_MODEL, TN), lambda i, j: (0, g + jnp.minimum(j, g - 1))),
                  pl.BlockSpec((D_MODEL, HEAD_PAD), lambda i, j: (0, 0)),
                  pl.BlockSpec((1, HEAD_PAD), lambda i, j: (0, 0))],
        out_specs=[pl.BlockSpec((TMP, TN), lambda i, j: (i, jnp.minimum(j, g - 1))),
                   pl.BlockSpec((TMP, TN), lambda i, j: (i, jnp.clip(j - g, 0, nz - 1))),
                   pl.BlockSpec((TMP, TN), lambda i, j: (i, jnp.clip(j - g - nz, 0, nx - 1))),
                   pl.BlockSpec((TMP, HEAD_PAD), lambda i, j: (i, 0))],
        out_shape=[jax.ShapeDtypeStruct((m, D_CONV), F32), jax.ShapeDtypeStruct((m, D_SSM), F32),
                   jax.ShapeDtypeStruct((m, D_XBC), F32), jax.ShapeDtypeStruct((m, HEAD_PAD), F32)],
        compiler_params=_params(("arbitrary", "arbitrary")), name="in_proj",
    )(u, w_in, w_in, w_dt, dt_bias)


def _ln_swish(y, g, b):
    mu = jnp.mean(y, axis=-1, keepdims=True)
    yc = y - mu
    yn = yc * lax.rsqrt(jnp.mean(yc * yc, axis=-1, keepdims=True) + EPS) * g + b
    return _silu(yn)


def _pconv_kernel(v_ref, w_ref, b_ref, lg_ref, lb_ref, o_ref, tail_ref, xpad_scr, conv_scr):
    t = pl.program_id(1)

    @pl.when(t == 0)
    def _():
        xpad_scr[0:CARRY, :] = jnp.zeros((CARRY, D_CONV), F32)

    xpad_scr[CARRY:CARRY + TL, :] = v_ref[...]
    first = CARRY - (CONV_W - 1)
    hb = CONV_ROWS
    n_rows = CARRY + TL
    for cb in range(D_CONV // 128):
        lanes = slice(cb * 128, (cb + 1) * 128)
        xfull = xpad_scr[:, lanes]
        conv_scr[:, lanes] = jnp.broadcast_to(b_ref[:, lanes], (TL, 128))
        for phase in range(8):
            xs = pltpu.roll(xfull, n_rows - phase, axis=0) if phase else xfull
            taps = [k for k in range(CONV_W) if (first + k) % 8 == phase]
            for base in range(0, TL, hb):
                acc = conv_scr[base:base + hb, lanes]
                for k in taps:
                    off = 8 * ((first + k) // 8)
                    acc = acc + w_ref[k:k + 1, lanes] * xs[base + off:base + off + hb]
                conv_scr[base:base + hb, lanes] = acc
    xpad_scr[0:CARRY, :] = xpad_scr[TL:TL + CARRY, :]
    for r in range(0, TL, NORM_ROWS):
        rows = slice(r, r + NORM_ROWS)
        o_ref[rows, :] = _ln_swish(conv_scr[rows, :], lg_ref[...], lb_ref[...]).astype(BF16)

    @pl.when(t == pl.num_programs(1) - 1)
    def _():
        tail_ref[0] = xpad_scr[0:CARRY, :]


def _prompt_conv(v, n_batch, seq, w, b, lg, lb):
    vec = pl.BlockSpec((1, D_CONV), lambda bi, t: (0, 0))
    steps = seq // TL
    return pl.pallas_call(
        _pconv_kernel, grid=(n_batch, steps),
        in_specs=[pl.BlockSpec((TL, D_CONV), lambda bi, t: (bi * steps + t, 0)),
                  pl.BlockSpec((CONV_W, D_CONV), lambda bi, t: (0, 0)), vec, vec, vec],
        out_specs=[pl.BlockSpec((TL, D_CONV), lambda bi, t: (bi * steps + t, 0)),
                   pl.BlockSpec((1, CARRY, D_CONV), lambda bi, t: (bi, 0, 0))],
        out_shape=[jax.ShapeDtypeStruct((n_batch * seq, D_CONV), BF16),
                   jax.ShapeDtypeStruct((n_batch, CARRY, D_CONV), F32)],
        scratch_shapes=[pltpu.VMEM((CARRY + TL, D_CONV), F32), pltpu.VMEM((TL, D_CONV), F32)],
        compiler_params=_params(("arbitrary", "arbitrary")), name="prompt_conv")(v, w, b, lg, lb)


def _gated_norm(y, z, g):
    yg = y * _silu(z)
    return yg * lax.rsqrt(jnp.mean(yg * yg, axis=-1, keepdims=True) + EPS) * g


def _sample_state_update(step, dec_ref, h0_ref, bc_ref, hn_ref, yt_scr, xdt_t_scr):
    per_step = h0_ref.shape[0]
    nb = bc_ref.shape[0]
    seq_i = lax.broadcasted_iota(jnp.int32, (nb, D_STATE), 0)
    seqs = [step * per_step + bb for bb in range(per_step)]
    brows = [bc_ref[pl.ds(b, 1), :] for b in seqs]

    def one_hot_rows(g):
        cols = slice(g * D_STATE, (g + 1) * D_STATE)
        return jnp.concatenate([jnp.where(seq_i == b, brow[:, cols], 0.0).astype(BF16)
                                for b, brow in zip(seqs, brows)], axis=1)

    for g in range(N_GROUPS):
        rows = slice(g * GROUP_W, (g + 1) * GROUP_W)
        s_new = _dot(xdt_t_scr[rows, :], one_hot_rows(g))
        h_all = []
        for bb, b in enumerate(seqs):
            parts = []
            for r in range(HEADS_PER_GROUP):
                h = g * HEADS_PER_GROUP + r
                hr = slice(h * HEAD_DIM, (h + 1) * HEAD_DIM)
                parts.append(h0_ref[bb, hr, :] * dec_ref[b * N_HEADS + h]
                             + s_new[r * HEAD_DIM:(r + 1) * HEAD_DIM, bb * D_STATE:(bb + 1) * D_STATE])
            h_new = jnp.concatenate(parts, axis=0)
            hn_ref[bb, rows, :] = h_new
            h_all.append(h_new.astype(BF16))
        yt_scr[rows, :] += lax.dot_general(jnp.concatenate(h_all, axis=1), one_hot_rows(N_GROUPS + g),
                                           (((1,), (1,)), ((), ())), preferred_element_type=F32)


def _pssd_kernel(xbc_ref, z_ref, dt_ref, cw_ref, cb_ref, a_ref, expand_ref, dexp_ref, ng_ref,
                 dec_ref, h0_ref, xdt_ref, bc_ref, skip_ref, zs_ref,
                 y_ref, hfin_ref, tail_ref, hn_ref, ys_ref,
                 state_scr, xpad_scr, xc_scr, yt_scr, xdt_t_scr):
    c = pl.program_id(1)
    q = CHUNK
    step = pl.program_id(0) * pl.num_programs(1) + c

    @pl.when(step == 0)
    def _():
        yt_scr[...] = jnp.zeros_like(yt_scr)
        xdt_t_scr[...] = xdt_ref[...].T.astype(BF16)

    @pl.when(c == 0)
    def _():
        state_scr[...] = jnp.zeros_like(state_scr)
        xpad_scr[0:XCARRY, :] = jnp.zeros((XCARRY, D_XBC), F32)

    xpad_scr[XCARRY:XCARRY + q, :] = xbc_ref[...]
    for cb in range(D_XBC // 512):
        lanes = slice(cb * 512, (cb + 1) * 512)
        acc = jnp.broadcast_to(cb_ref[:, lanes], (q, 512))
        for j in range(SSM_CONV_W):
            k = SSM_CONV_W - 1 - j
            acc = acc + cw_ref[k:k + 1, lanes] * xpad_scr[XCARRY - j:XCARRY - j + q, lanes]
        xc_scr[:, lanes] = _silu(acc)
    xpad_scr[0:XCARRY, :] = xpad_scr[q:q + XCARRY, :]

    row_i = lax.broadcasted_iota(jnp.int32, (q, q), 0)
    col_i = lax.broadcasted_iota(jnp.int32, (q, q), 1)
    tril = row_i >= col_i
    tri = jnp.where(tril, 1.0, 0.0).astype(BF16)
    expand = expand_ref[...]

    dt = dt_ref[...]
    tri3 = jnp.concatenate([tri, tri, tri], axis=1)
    a_cs = _dot3_rhs(tri3, dt * a_ref[...])
    a_cs_t = a_cs.T
    dt_exp = _dot3_lhs(dt, expand)
    acs_exp = _dot3_lhs(a_cs, expand)
    last = acs_exp[q - 1:q, :]
    lane_lo = lax.broadcasted_iota(jnp.int32, (q, 128), 1) < HEAD_DIM

    for g in range(N_GROUPS):
        ch = slice(g * GROUP_W, (g + 1) * GROUP_W)
        xs = xc_scr[:, ch]
        bg = xc_scr[:, D_SSM + g * D_STATE:D_SSM + (g + 1) * D_STATE]
        cg = xc_scr[:, D_SSM + (N_GROUPS + g) * D_STATE:D_SSM + (N_GROUPS + g + 1) * D_STATE]
        bg16 = bg.astype(BF16)
        cg16 = cg.astype(BF16)
        xdt = xs * dt_exp[:, ch]
        acs_g = acs_exp[:, ch]
        cb = lax.dot_general(cg16, bg16, (((1,), (1,)), ((), ())), preferred_element_type=F32)
        st = state_scr[:, ch]
        y = _dot(cg16, st.astype(BF16)) * jnp.exp(acs_g)
        pieces = []
        for pr in range(HEADS_PER_GROUP // 2):
            xpair = xdt[:, pr * 128:(pr + 1) * 128].astype(BF16)
            both = []
            for half in range(2):
                h = g * HEADS_PER_GROUP + 2 * pr + half
                seg = a_cs[:, h:h + 1] - a_cs_t[h:h + 1, :]
                decay = jnp.exp(jnp.where(tril, seg, NEG_BIG))
                both.append(_dot((cb * decay).astype(BF16), xpair))
            pieces.append(jnp.where(lane_lo, both[0], both[1]))
        y = y + jnp.concatenate(pieces, axis=1) + dexp_ref[:, ch] * xs
        xdec = (xdt * jnp.exp(last[:, ch] - acs_g)).astype(BF16)
        s_new = lax.dot_general(bg16, xdec, (((0,), (0,)), ((), ())), preferred_element_type=F32)
        state_scr[:, ch] = st * jnp.exp(last[:, ch]) + s_new
        y_ref[:, ch] = _gated_norm(y, z_ref[:, ch], ng_ref[:, ch]).astype(BF16)

    _sample_state_update(step, dec_ref, h0_ref, bc_ref, hn_ref, yt_scr, xdt_t_scr)

    @pl.when(c == pl.num_programs(1) - 1)
    def _():
        hfin_ref[0] = state_scr[...].T
        tail_ref[0] = xpad_scr[0:XCARRY, :]

    @pl.when(step == pl.num_programs(0) * pl.num_programs(1) - 1)
    def _():
        y = yt_scr[...].T + skip_ref[...]
        for g in range(N_GROUPS):
            ch = slice(g * GROUP_W, (g + 1) * GROUP_W)
            ys_ref[:, ch] = _gated_norm(y[:, ch], zs_ref[:, ch], ng_ref[:, ch]).astype(BF16)


def _ssd(xbc, z, dt, n_batch, seq, cw, cb, a_pad, d_exp, ng, dec, h0, xdt, bc, skip):
    nc = seq // CHUNK
    n_prompt = n_batch * seq
    n_samp = h0.shape[0]
    per_step = n_samp // (n_batch * nc)
    assert per_step * n_batch * nc == n_samp and n_prompt % n_samp == 0

    def rows(w):
        return pl.BlockSpec((CHUNK, w), lambda bi, c: (bi * nc + c, 0))

    def vec(*s):
        return pl.BlockSpec(s, lambda bi, c: (0,) * len(s))

    states = pl.BlockSpec((per_step, D_SSM, D_STATE), lambda bi, c: (bi * nc + c, 0, 0))
    return pl.pallas_call(
        _pssd_kernel, grid=(n_batch, nc),
        in_specs=[rows(D_XBC), rows(D_SSM), rows(HEAD_PAD), vec(SSM_CONV_W, D_XBC), vec(1, D_XBC),
                  vec(1, HEAD_PAD), vec(3 * HEAD_PAD, D_SSM), vec(1, D_SSM), vec(1, D_SSM),
                  pl.BlockSpec(memory_space=pltpu.SMEM), states, vec(n_samp, D_SSM),
                  vec(n_samp, D_XBC - D_SSM), vec(n_samp, D_SSM),
                  pl.BlockSpec((n_samp, D_SSM), lambda bi, c: (n_prompt // n_samp, 0))],
        out_specs=[rows(D_SSM), pl.BlockSpec((1, D_SSM, D_STATE), lambda bi, c: (bi, 0, 0)),
                   pl.BlockSpec((1, XCARRY, D_XBC), lambda bi, c: (bi, 0, 0)),
                   states, vec(n_samp, D_SSM)],
        out_shape=[jax.ShapeDtypeStruct((n_prompt, D_SSM), BF16),
                   jax.ShapeDtypeStruct((n_batch, D_SSM, D_STATE), F32),
                   jax.ShapeDtypeStruct((n_batch, XCARRY, D_XBC), F32),
                   jax.ShapeDtypeStruct((n_samp, D_SSM, D_STATE), F32),
                   jax.ShapeDtypeStruct((n_samp, D_SSM), BF16)],
        scratch_shapes=[pltpu.VMEM((D_STATE, D_SSM), F32), pltpu.VMEM((XCARRY + CHUNK, D_XBC), F32),
                        pltpu.VMEM((CHUNK, D_XBC), F32),
                        pltpu.VMEM((D_SSM, n_samp), F32), pltpu.VMEM((D_SSM, n_samp), BF16)],
        compiler_params=_params(("arbitrary", "arbitrary")), name="ssd",
    )(xbc, z, dt, cw, cb, a_pad, _head_expand_matrix(), d_exp, ng, dec, h0, xdt, bc, skip, z)


def _sprep_kernel(v_ref, cst_ref, w_ref, b_ref, lg_ref, lb_ref,
                  xbc_ref, xst_ref, cw_ref, cb_ref, dt_ref, a_ref, dexp_ref,
                  co_ref, ncst_ref, nxst_ref, xdt_ref, bc_ref, dec_ref, skip_ref):
    nb = v_ref.shape[0]
    v = v_ref[...]
    acc = jnp.broadcast_to(b_ref[...], (nb, D_CONV)) + w_ref[CONV_W - 1:CONV_W, :] * v
    for k in range(CONV_W - 1):
        acc = acc + w_ref[k:k + 1, :] * cst_ref[k]
    co_ref[...] = _ln_swish(acc, lg_ref[...], lb_ref[...]).astype(BF16)
    for k in range(CONV_W - 2):
        ncst_ref[k] = cst_ref[k + 1]
    ncst_ref[CONV_W - 2] = v

    xn = xbc_ref[...]
    acc = jnp.broadcast_to(cb_ref[...], (nb, D_XBC)) + cw_ref[SSM_CONV_W - 1:SSM_CONV_W, :] * xn
    for k in range(SSM_CONV_W - 1):
        acc = acc + cw_ref[k:k + 1, :] * xst_ref[k]
    xc = _silu(acc)
    for k in range(SSM_CONV_W - 2):
        nxst_ref[k] = xst_ref[k + 1]
    nxst_ref[SSM_CONV_W - 2] = xn

    xs = xc[:, :D_SSM]
    bc_ref[...] = xc[:, D_SSM:]
    dt = dt_ref[...]
    dec_ref[...] = jnp.exp(dt * a_ref[...])
    dt_exp = _dot3_lhs(dt, _head_expand_matrix())
    xdt_ref[...] = xs * dt_exp
    skip_ref[...] = dexp_ref[...] * xs


def _sample_prep(glu, xbc, dt, n_prompt, n_samp, cst, xst, mw, mb, lg, lb, cw, cb, a_pad, d_exp):
    sb = SAMPLE_BLOCK
    off = n_prompt // sb
    srow = lambda w: pl.BlockSpec((sb, w), lambda i: (off + i, 0))
    orow = lambda w: pl.BlockSpec((sb, w), lambda i: (i, 0))
    taps = lambda k, w: pl.BlockSpec((k, sb, w), lambda i: (0, i, 0))
    const = lambda *s: pl.BlockSpec(s, lambda i: (0,) * len(s))
    cst_t = jnp.swapaxes(cst, 0, 1)
    xst_t = jnp.swapaxes(xst, 0, 1)
    c_samp, ncst_t, nxst_t, xdt, bc, dec, skip = pl.pallas_call(
        _sprep_kernel, grid=(n_samp // sb,),
        in_specs=[srow(D_CONV), taps(CONV_W - 1, D_CONV), const(CONV_W, D_CONV), const(1, D_CONV),
                  const(1, D_CONV), const(1, D_CONV),
                  srow(D_XBC), taps(SSM_CONV_W - 1, D_XBC), const(SSM_CONV_W, D_XBC),
                  const(1, D_XBC), srow(HEAD_PAD), const(1, HEAD_PAD), const(1, D_SSM)],
        out_specs=[orow(D_CONV), taps(CONV_W - 1, D_CONV), taps(SSM_CONV_W - 1, D_XBC),
                   orow(D_SSM), orow(D_XBC - D_SSM), orow(HEAD_PAD), orow(D_SSM)],
        out_shape=[jax.ShapeDtypeStruct((n_samp, D_CONV), BF16),
                   jax.ShapeDtypeStruct((CONV_W - 1, n_samp, D_CONV), F32),
                   jax.ShapeDtypeStruct((SSM_CONV_W - 1, n_samp, D_XBC), F32),
                   jax.ShapeDtypeStruct((n_samp, D_SSM), F32),
                   jax.ShapeDtypeStruct((n_samp, D_XBC - D_SSM), F32),
                   jax.ShapeDtypeStruct((n_samp, HEAD_PAD), F32),
                   jax.ShapeDtypeStruct((n_samp, D_SSM), F32)],
        compiler_params=_params(("arbitrary",)), name="sample_prep",
    )(glu, cst_t, mw, mb, lg, lb, xbc, xst_t, cw, cb, dt, a_pad, d_exp)
    dec_flat = dec[:, :N_HEADS].reshape(-1)
    return (c_samp, jnp.swapaxes(ncst_t, 0, 1), jnp.swapaxes(nxst_t, 0, 1), xdt, bc, dec_flat, skip)


def _outproj_kernel(c_ref, cs_ref, y_ref, ys_ref, w_ref, h_ref, g_ref, o_ref):
    i = pl.program_id(0)
    last = pl.num_programs(0) - 1
    na = TM - cs_ref.shape[0]

    def run(c, y):
        mix = _dot(c, w_ref[0:D_CONV, :]) + _dot(y, w_ref[D_CONV:, :])
        o_ref[...] = h_ref[...] + _rms(mix, g_ref[...])

    @pl.when(i != last)
    def _():
        run(c_ref[...], y_ref[...])

    @pl.when(i == last)
    def _():
        run(jnp.concatenate([c_ref[0:na, :], cs_ref[...]], axis=0),
            jnp.concatenate([y_ref[0:na, :], ys_ref[...]], axis=0))


def _out_proj(c_prompt, c_samp, y_prompt, y_samp, w_out, h, g):
    m = h.shape[0]
    n_samp = c_samp.shape[0]
    row = pl.BlockSpec((TM, D_MODEL), lambda i: (i, 0))
    return pl.pallas_call(
        _outproj_kernel, grid=(m // TM,),
        in_specs=[pl.BlockSpec((TM, D_CONV), lambda i: (i, 0)),
                  pl.BlockSpec((n_samp, D_CONV), lambda i: (0, 0)),
                  pl.BlockSpec((TM, D_SSM), lambda i: (i, 0)),
                  pl.BlockSpec((n_samp, D_SSM), lambda i: (0, 0)),
                  pl.BlockSpec((D_CONV + D_SSM, D_MODEL), lambda i: (0, 0)),
                  row, pl.BlockSpec((1, D_MODEL), lambda i: (0, 0))],
        out_specs=row, out_shape=jax.ShapeDtypeStruct((m, D_MODEL), F32),
        compiler_params=_params(("arbitrary",)), name="out_proj",
    )(c_prompt, c_samp, y_prompt, y_samp, w_out, h, g)


def _ple_kernel(h_ref, pa_ref, pb_ref, gpre_ref, wg_ref, wp_ref, gpost_ref, oa_ref, ob_ref, emb_scr):
    i = pl.program_id(0)

    def embed(rows, src, srows):
        emb_scr[rows, :] = _dot(src[srows, :].astype(BF16), wp_ref[...])
    _on_tile_rows(i, pa_ref, pb_ref, embed)

    h = h_ref[...]
    gate = jax.nn.sigmoid(_dot(_rms(h, gpre_ref[...]).astype(BF16), wg_ref[...]))
    oa_ref[...] = h + _rms(gate * emb_scr[...], gpost_ref[...])

    @pl.when(i == pl.num_programs(0) - 1)
    def _():
        nb = ob_ref.shape[0]
        ob_ref[...] = oa_ref[TM - nb:TM, :]


def _ple(h, p_prompt, p_samp, gpre, wg, wp, gpost):
    m = h.shape[0]
    n_prompt, n_samp = p_prompt.shape[0], p_samp.shape[0]
    row = pl.BlockSpec((TM, D_MODEL), lambda i: (i, 0))
    vec = pl.BlockSpec((1, D_MODEL), lambda i: (0, 0))
    return pl.pallas_call(
        _ple_kernel, grid=(m // TM,),
        in_specs=[row, pl.BlockSpec((TM, PLE_DIM), lambda i: (i, 0)),
                  pl.BlockSpec((n_samp, PLE_DIM), lambda i: (0, 0)), vec,
                  pl.BlockSpec((D_MODEL, D_MODEL), lambda i: (0, 0)),
                  pl.BlockSpec((PLE_DIM, D_MODEL), lambda i: (0, 0)), vec],
        out_specs=[row, pl.BlockSpec((n_samp, D_MODEL), lambda i: (0, 0))],
        out_shape=[jax.ShapeDtypeStruct((n_prompt, D_MODEL), F32),
                   jax.ShapeDtypeStruct((n_samp, D_MODEL), F32)],
        scratch_shapes=[pltpu.VMEM((TM, D_MODEL), F32)],
        compiler_params=_params(("arbitrary",)), name="ple")(h, p_prompt, p_samp, gpre, wg, wp, gpost)


def _layer(x_prompt, x_samp, p_prompt, p_samp, n_batch, seq, cst, xst, h0, lw):
    (norm_ffn1_pre, w_ffn1_gate, w_ffn1_up, w_ffn1_down, norm_ffn1_post,
     norm_mix_pre, w_in, conv_mod_w, conv_mod_b, conv_mod_ln_g, conv_mod_ln_b,
     ssm_conv_w, ssm_conv_b, dt_bias, a_log, d_skip, ssm_norm_g, w_out, norm_mix_post,
     norm_ffn2_pre, w_ffn2_gate, w_ffn2_up, w_ffn2_down, norm_ffn2_post,
     norm_ple_pre, w_ple_gate, w_ple_proj, norm_ple_post) = lw
    n_prompt, n_samp = x_prompt.shape[0], x_samp.shape[0]
    m = n_prompt + n_samp
    _check_split(n_prompt, n_samp)
    row2 = lambda t: t.reshape(1, -1)

    later = ((jnp.swapaxes(w_in, 0, 1), 128, True), (w_out, 64, False), (w_ffn2_gate, 32, False),
             (w_ffn2_up, 32, False), (w_ffn2_down, 64, False), (w_ple_gate, 32, False),
             (w_ple_proj, 16, False))
    ffn1 = (row2(norm_ffn1_pre), row2(norm_ffn1_post), row2(norm_mix_pre))
    h1, u, wg1, wu1, wd1 = _ffn(
        x_prompt, ffn1[0], w_ffn1_gate, w_ffn1_up, w_ffn1_down, ffn1[1], gnext=ffn1[2],
        m=m, tiles=(0, 1), emit_w16=True, tf=TF_FIRST, tm=FIRST_TILES * TM, name="ffn_first")
    h1, u, w_in16, w_out16, wg2, wu2, wd2, wpg, wpp = _ffn(
        x_prompt, ffn1[0], wg1, wu1, wd1, ffn1[1], gnext=ffn1[2], x_tail=x_samp, casts=later,
        m=m, tiles=(FIRST_TILES, m // TM - FIRST_TILES), carry=(h1, u), name="ffn_rest")

    d_proj = w_in.shape[1]
    w_dt = jnp.pad(w_in16[:, d_proj - N_HEADS:], ((0, 0), (0, HEAD_PAD - N_HEADS)))
    pad_h = lambda t: jnp.pad(t.astype(F32), (0, HEAD_PAD - N_HEADS)).reshape(1, HEAD_PAD)
    glu, z, xbc, dt = _in_proj(u, w_in16, w_dt, pad_h(dt_bias))

    a = -jnp.exp(a_log.astype(F32))
    a_pad = pad_h(a)
    d_exp = row2(jnp.repeat(d_skip.astype(F32), HEAD_DIM))
    cw, cb = ssm_conv_w, row2(ssm_conv_b)
    mw, mb, lg, lb = conv_mod_w, row2(conv_mod_b), row2(conv_mod_ln_g), row2(conv_mod_ln_b)
    ng = row2(ssm_norm_g)

    c_prompt, glu_tail = _prompt_conv(glu, n_batch, seq, mw, mb, lg, lb)
    c_samp, new_cst_samp, new_xst_samp, xdt, bc, dec, skip = _sample_prep(
        glu, xbc, dt, n_prompt, n_samp, cst, xst, mw, mb, lg, lb, cw, cb, a_pad, d_exp)
    y_mix_prompt, hfin_prompt, xbc_tail, hn, y_mix_samp = _ssd(
        xbc, z, dt, n_batch, seq, cw, cb, a_pad, d_exp, ng,
        dec, h0.reshape(n_samp, D_SSM, D_STATE), xdt, bc, skip)

    h2 = _out_proj(c_prompt, c_samp, y_mix_prompt, y_mix_samp, w_out16, h1, row2(norm_mix_post))
    (h3,) = _ffn(h2, row2(norm_ffn2_pre), wg2, wu2, wd2, row2(norm_ffn2_post))
    y_prompt, y_samp = _ple(h3, p_prompt, p_samp, row2(norm_ple_pre), wpg, wpp, row2(norm_ple_post))

    new_cst_prompt = glu_tail[:, CARRY - (CONV_W - 1):]
    new_xst_prompt = xbc_tail[:, XCARRY - (SSM_CONV_W - 1):]
    new_h_prompt = hfin_prompt.reshape(n_batch, N_HEADS, HEAD_DIM, D_STATE)
    new_h_samp = hn.reshape(n_samp, N_HEADS, HEAD_DIM, D_STATE)
    return (y_prompt, y_samp, new_cst_prompt, new_xst_prompt, new_h_prompt,
            new_cst_samp, new_xst_samp, new_h_samp)


def kernel(x_prompt, x_sample, state_conv_mod, state_ssm_conv, state_ssm, p_prompt, p_sample,
           norm_ffn1_pre, w_ffn1_gate, w_ffn1_up, w_ffn1_down, norm_ffn1_post,
           norm_mix_pre, w_in, conv_mod_w, conv_mod_b, conv_mod_ln_g, conv_mod_ln_b,
           ssm_conv_w, ssm_conv_b, dt_bias, a_log, d_skip, ssm_norm_g, w_out, norm_mix_post,
           norm_ffn2_pre, w_ffn2_gate, w_ffn2_up, w_ffn2_down, norm_ffn2_post,
           norm_ple_pre, w_ple_gate, w_ple_proj, norm_ple_post):
    weights = (norm_ffn1_pre, w_ffn1_gate, w_ffn1_up, w_ffn1_down, norm_ffn1_post,
               norm_mix_pre, w_in, conv_mod_w, conv_mod_b, conv_mod_ln_g, conv_mod_ln_b,
               ssm_conv_w, ssm_conv_b, dt_bias, a_log, d_skip, ssm_norm_g, w_out, norm_mix_post,
               norm_ffn2_pre, w_ffn2_gate, w_ffn2_up, w_ffn2_down, norm_ffn2_post,
               norm_ple_pre, w_ple_gate, w_ple_proj, norm_ple_post)
    n_batch, seq, _ = x_prompt.shape
    n_samp = x_sample.shape[0]
    n_prompt = n_batch * seq
    depth = norm_ffn1_pre.shape[0]
    xp = x_prompt.reshape(n_prompt, D_MODEL)
    xs = x_sample.reshape(n_samp, D_MODEL)
    outs = [[] for _ in range(6)]
    for i in range(depth):
        res = _layer(xp, xs, p_prompt[i].reshape(n_prompt, PLE_DIM), p_sample[i].reshape(n_samp, PLE_DIM),
                     n_batch, seq, state_conv_mod[i], state_ssm_conv[i], state_ssm[i],
                     tuple(w[i] for w in weights))
        xp, xs = res[0], res[1]
        for lst, r in zip(outs, res[2:]):
            lst.append(r)
    return ((xp.reshape(n_batch, seq, D_MODEL), xs.reshape(n_samp, 1, D_MODEL))
            + tuple(jnp.stack(lst, axis=0) for lst in outs))
```

```python
import functools

import jax
import jax.numpy as jnp
from jax import lax
from jax.experimental import pallas as pl
from jax.experimental.pallas import tpu as pltpu

F32 = jnp.float32
BF16 = jnp.bfloat16

D_MODEL = 2048
D_FF = 5632
D_CONV = 1024
D_SSM = 3072
N_HEADS = 48
HEAD_DIM = 64
N_GROUPS = 8
HEADS_PER_GROUP = 6
GROUP_W = HEADS_PER_GROUP * HEAD_DIM
D_STATE = 128
D_XBC = D_SSM + 2 * N_GROUPS * D_STATE
MIX_W = D_XBC + D_CONV + D_SSM
GLU_COL = D_XBC // D_CONV
Z_COL = (D_XBC + D_CONV) // D_SSM
assert GLU_COL * D_CONV == D_XBC and Z_COL * D_SSM == D_XBC + D_CONV
CONV_W = 31
SSM_CONV_W = 4
CHUNK = 128
PLE_DIM = 256
EPS = 1e-6
NEG_BIG = -1e30
HEAD_PAD = 128

TM = 640
TF = 512
TF_FIRST = 256
FIRST_TILES = 2
TMP = 1664
TN = 1024
TL = 256
CARRY = 32
CONV_ROWS = 64
NORM_ROWS = 16
XCARRY = 8
SAMPLE_BLOCK = 32
VMEM_LIMIT = 56 * 1024 * 1024
VMEM_LIMIT_MAX = 58 * 1024 * 1024


def _params(dims, vmem=VMEM_LIMIT):
    return pltpu.CompilerParams(dimension_semantics=dims, vmem_limit_bytes=vmem)


def _rms(x, g):
    return x * lax.rsqrt(jnp.mean(x * x, axis=-1, keepdims=True) + EPS) * g


def _silu(x):
    return x * jax.nn.sigmoid(x)


def _dot(a, b):
    return jnp.dot(a, b, preferred_element_type=F32)


def _split3(x):
    hi = x.astype(BF16)
    r = x - hi.astype(F32)
    mid = r.astype(BF16)
    lo = (r - mid.astype(F32)).astype(BF16)
    return hi, mid, lo


def _dot3_rhs(a3_bf16, x):
    return _dot(a3_bf16, jnp.concatenate(_split3(x), axis=0))


def _dot3_lhs(x, b3_bf16):
    return _dot(jnp.concatenate(_split3(x), axis=1), b3_bf16)


def _head_expand_matrix():
    head = lax.broadcasted_iota(jnp.int32, (3 * HEAD_PAD, D_SSM), 0) & (HEAD_PAD - 1)
    chan = lax.broadcasted_iota(jnp.int32, (3 * HEAD_PAD, D_SSM), 1)
    return jnp.where((chan >> 6) == head, 1.0, 0.0).astype(BF16)


def _on_tile_rows(i, a_ref, b_ref, fn):
    tm = a_ref.shape[0]
    if b_ref is None:
        fn(slice(0, tm), a_ref, slice(0, tm))
        return
    nb = b_ref.shape[0]
    na = tm - nb
    last = pl.num_programs(0) - 1

    @pl.when(i != last)
    def _():
        fn(slice(0, tm), a_ref, slice(0, tm))

    @pl.when(i == last)
    def _():
        fn(slice(0, na), a_ref, slice(0, na))
        fn(slice(na, tm), b_ref, slice(0, nb))


def _check_split(n_prompt, n_samp):
    assert (n_prompt + n_samp) % TM == 0 and n_samp < TM and n_samp % 16 == 0


def _ffn_kernel(*refs, split, n_next, cast_t, emit_w16, n_carry):
    n_cast = len(cast_t)
    refs = list(refs)
    xa_ref = refs.pop(0)
    xb_ref = refs.pop(0) if split else None
    gpre_ref, wg_ref, wu_ref, wd_ref, gpost_ref = refs[:5]
    refs = refs[5:]
    gnext_ref = refs.pop(0) if n_next else None
    cast_in, refs = refs[:n_cast], refs[n_cast:]
    refs = refs[n_carry:]
    o_ref = refs.pop(0)
    unext_ref = refs.pop(0) if n_next else None
    cast_out, refs = refs[:n_cast], refs[n_cast:]
    w16_refs, refs = (refs[:3], refs[3:]) if emit_w16 else ((), refs)
    (u_scr,) = refs
    i = pl.program_id(0)
    j = pl.program_id(1)

    @pl.when(j == 0)
    def _():
        def pre(rows, src, srows):
            u_scr[rows, :] = _rms(src[srows, :], gpre_ref[...]).astype(BF16)
        _on_tile_rows(i, xa_ref, xb_ref, pre)
        o_ref[...] = jnp.zeros_like(o_ref)

    for ci, co, transpose in zip(cast_in, cast_out, cast_t):
        co[...] = (ci[...].T if transpose else ci[...]).astype(BF16)

    wg, wu, wd = wg_ref[...], wu_ref[...], wd_ref[...]
    if emit_w16:
        wg, wu, wd = wg.astype(BF16), wu.astype(BF16), wd.astype(BF16)
        for ref, w in zip(w16_refs, (wg, wu, wd)):
            ref[...] = w
    u = u_scr[...]
    act = (_silu(_dot(u, wg)) * _dot(u, wu)).astype(BF16)
    o_ref[...] += _dot(act, wd)

    @pl.when(j == pl.num_programs(1) - 1)
    def _():
        def post(rows, src, srows):
            step = rows.stop - rows.start if n_next else NORM_ROWS
            for r in range(0, rows.stop - rows.start, step):
                dst = slice(rows.start + r, rows.start + r + step)
                h = src[srows.start + r:srows.start + r + step, :] + _rms(o_ref[dst, :], gpost_ref[...])
                o_ref[dst, :] = h
                if n_next:
                    unext_ref[dst, :] = _rms(h, gnext_ref[...]).astype(BF16)
        _on_tile_rows(i, xa_ref, xb_ref, post)


def _ffn(x, gpre, wg, wu, wd, gpost, gnext=None, x_tail=None, casts=(), m=None, tiles=None,
         carry=(), emit_w16=False, tf=TF, tm=TM, name="ffn"):
    if m is None:
        m = x.shape[0] + (0 if x_tail is None else x_tail.shape[0])
    first, count = tiles if tiles is not None else (0, m // tm)
    grid = (count, D_FF // tf)
    steps = grid[0] * grid[1]
    row = pl.BlockSpec((tm, D_MODEL), lambda i, j: (i + first, 0))
    vec = pl.BlockSpec((1, D_MODEL), lambda i, j: (0, 0))
    wcol = pl.BlockSpec((D_MODEL, tf), lambda i, j: (0, j))
    wrow = pl.BlockSpec((tf, D_MODEL), lambda i, j: (j, 0))
    in_specs, args = [row], [x]
    if x_tail is not None:
        in_specs.append(pl.BlockSpec(x_tail.shape, lambda i, j: (0, 0)))
        args.append(x_tail)
    in_specs += [vec, wcol, wcol, wrow, vec]
    args += [gpre, wg, wu, wd, 0.5 * gpost]
    out_shape = [jax.ShapeDtypeStruct((m, D_MODEL), F32)]
    out_specs = [row]
    if gnext is not None:
        in_specs.append(vec)
        args.append(gnext)
        out_shape.append(jax.ShapeDtypeStruct((m, D_MODEL), BF16))
        out_specs.append(row)
    assert len(carry) in (0, len(out_shape))
    for w, r, transpose in casts:
        nblk = pl.cdiv(w.shape[0], r)
        assert r % 16 == 0 and nblk <= steps and (w.shape[0] % r == 0 or transpose)
        slab = lambda i, j, nblk=nblk: jnp.minimum(i * grid[1] + j, nblk - 1)
        in_specs.append(pl.BlockSpec((r, w.shape[1]), lambda i, j, slab=slab: (slab(i, j), 0)))
        args.append(w)
        if transpose:
            assert r % 128 == 0
            out_specs.append(pl.BlockSpec((w.shape[1], r), lambda i, j, slab=slab: (0, slab(i, j))))
            out_shape.append(jax.ShapeDtypeStruct(w.shape[::-1], BF16))
        else:
            out_specs.append(pl.BlockSpec((r, w.shape[1]), lambda i, j, slab=slab: (slab(i, j), 0)))
            out_shape.append(jax.ShapeDtypeStruct(w.shape, BF16))
    aliases = {}
    for k, c in enumerate(carry):
        aliases[len(args)] = k
        in_specs.append(pl.BlockSpec(memory_space=pl.ANY))
        args.append(c)
    if emit_w16:
        out_specs += [wcol, wcol, wrow]
        out_shape += [jax.ShapeDtypeStruct(w.shape, BF16) for w in (wg, wu, wd)]
    return pl.pallas_call(
        functools.partial(_ffn_kernel, split=x_tail is not None, n_next=gnext is not None,
                          cast_t=tuple(t for _, _, t in casts), emit_w16=emit_w16,
                          n_carry=len(carry)),
        grid=grid, in_specs=in_specs, out_specs=out_specs, out_shape=out_shape,
        scratch_shapes=[pltpu.VMEM((tm, D_MODEL), BF16)],
        input_output_aliases=aliases,
        compiler_params=_params(("arbitrary", "arbitrary")), name=name,
    )(*args)


N_GLU_STEPS = D_CONV // TN
N_Z_STEPS = D_SSM // TN
N_XBC_STEPS = D_XBC // TN


def _inproj_kernel(u_ref, w_ref, wb_ref, wdt_ref, dtb_ref, o_ref, dt_ref):
    j = pl.program_id(1)

    @pl.when(j < N_GLU_STEPS)
    def _():
        u = u_ref[...]
        o_ref[...] = _dot(u, w_ref[...]) * jax.nn.sigmoid(_dot(u, wb_ref[...]))

    @pl.when(j >= N_GLU_STEPS)
    def _():
        o_ref[...] = _dot(u_ref[...], w_ref[...])

    @pl.when(j == pl.num_programs(1) - 1)
    def _():
        x = _dot(u_ref[...], wdt_ref[...]) + dtb_ref[...]
        dt_ref[...] = jnp.maximum(x, 0.0) + jnp.log1p(jnp.exp(-jnp.abs(x)))


def _in_proj(u, w_in, w_dt, dt_bias):
    m = u.shape[0]
    g, nz, nx = N_GLU_STEPS, N_Z_STEPS, N_XBC_STEPS
    assert m % TMP == 0 and g == 1

    def out_slab(j):
        return jnp.where(j < g + nz, j + nx, j - g - nz)

    return pl.pallas_call(
        _inproj_kernel, grid=(m // TMP, g + nz + nx),
        in_specs=[pl.BlockSpec((TMP, D_MODEL), lambda i, j: (i, 0)),
                  pl.BlockSpec((D_MODEL, TN), lambda i, j: (0, jnp.where(j < g, j, j + g))),
                  pl.BlockSpec((D_MODEL, TN), lambda i, j: (0, g)),
                  pl.BlockSpec((D_MODEL, HEAD_PAD), lambda i, j: (0, 0)),
                  pl.BlockSpec((1, HEAD_PAD), lambda i, j: (0, 0))],
        out_specs=[pl.BlockSpec((TMP, TN), lambda i, j: (i, out_slab(j))),
                   pl.BlockSpec((TMP, HEAD_PAD), lambda i, j: (i, 0))],
        out_shape=[jax.ShapeDtypeStruct((m, MIX_W), F32), jax.ShapeDtypeStruct((m, HEAD_PAD), F32)],
        compiler_params=_params(("arbitrary", "arbitrary"), VMEM_LIMIT_MAX), name="in_proj",
    )(u, w_in, w_in, w_dt, dt_bias)


def _ln_swish(y, g, b):
    mu = jnp.mean(y, axis=-1, keepdims=True)
    yc = y - mu
    yn = yc * lax.rsqrt(jnp.mean(yc * yc, axis=-1, keepdims=True) + EPS) * g + b
    return _silu(yn)


def _pconv_kernel(v_ref, w_ref, b_ref, lg_ref, lb_ref, o_ref, tail_ref, xpad_scr, conv_scr):
    t = pl.program_id(1)

    @pl.when(t == 0)
    def _():
        xpad_scr[0:CARRY, :] = jnp.zeros((CARRY, D_CONV), F32)

    xpad_scr[CARRY:CARRY + TL, :] = v_ref[...]
    first = CARRY - (CONV_W - 1)
    hb = CONV_ROWS
    n_rows = CARRY + TL
    for cb in range(D_CONV // 128):
        lanes = slice(cb * 128, (cb + 1) * 128)
        xfull = xpad_scr[:, lanes]
        conv_scr[:, lanes] = jnp.broadcast_to(b_ref[:, lanes], (TL, 128))
        for phase in range(8):
            xs = pltpu.roll(xfull, n_rows - phase, axis=0) if phase else xfull
            taps = [k for k in range(CONV_W) if (first + k) % 8 == phase]
            for base in range(0, TL, hb):
                acc = conv_scr[base:base + hb, lanes]
                for k in taps:
                    off = 8 * ((first + k) // 8)
                    acc = acc + w_ref[k:k + 1, lanes] * xs[base + off:base + off + hb]
                conv_scr[base:base + hb, lanes] = acc
    xpad_scr[0:CARRY, :] = xpad_scr[TL:TL + CARRY, :]
    for r in range(0, TL, NORM_ROWS):
        rows = slice(r, r + NORM_ROWS)
        o_ref[rows, :] = _ln_swish(conv_scr[rows, :], lg_ref[...], lb_ref[...]).astype(BF16)

    @pl.when(t == pl.num_programs(1) - 1)
    def _():
        tail_ref[0] = xpad_scr[0:CARRY, :]


def _prompt_conv(mix, n_batch, seq, w, b, lg, lb):
    vec = pl.BlockSpec((1, D_CONV), lambda bi, t: (0, 0))
    steps = seq // TL
    return pl.pallas_call(
        _pconv_kernel, grid=(n_batch, steps),
        in_specs=[pl.BlockSpec((TL, D_CONV), lambda bi, t: (bi * steps + t, GLU_COL)),
                  pl.BlockSpec((CONV_W, D_CONV), lambda bi, t: (0, 0)), vec, vec, vec],
        out_specs=[pl.BlockSpec((TL, D_CONV), lambda bi, t: (bi * steps + t, 0)),
                   pl.BlockSpec((1, CARRY, D_CONV), lambda bi, t: (bi, 0, 0))],
        out_shape=[jax.ShapeDtypeStruct((n_batch * seq, D_CONV), BF16),
                   jax.ShapeDtypeStruct((n_batch, CARRY, D_CONV), F32)],
        scratch_shapes=[pltpu.VMEM((CARRY + TL, D_CONV), F32), pltpu.VMEM((TL, D_CONV), F32)],
        compiler_params=_params(("arbitrary", "arbitrary")), name="prompt_conv")(mix, w, b, lg, lb)


def _gated_norm(y, z, g):
    yg = y * _silu(z)
    return yg * lax.rsqrt(jnp.mean(yg * yg, axis=-1, keepdims=True) + EPS) * g


def _sample_state_update(step, dec_ref, h0_ref, bc_ref, hn_ref, yt_scr, xdt_t_scr):
    per_step = h0_ref.shape[0]
    nb = bc_ref.shape[0]
    seq_i = lax.broadcasted_iota(jnp.int32, (nb, D_STATE), 0)
    seqs = [step * per_step + bb for bb in range(per_step)]
    brows = [bc_ref[pl.ds(b, 1), :] for b in seqs]

    def one_hot_rows(g):
        cols = slice(g * D_STATE, (g + 1) * D_STATE)
        return jnp.concatenate([jnp.where(seq_i == b, brow[:, cols], 0.0).astype(BF16)
                                for b, brow in zip(seqs, brows)], axis=1)

    for g in range(N_GROUPS):
        rows = slice(g * GROUP_W, (g + 1) * GROUP_W)
        s_new = _dot(xdt_t_scr[rows, :], one_hot_rows(g))
        h_all = []
        for bb, b in enumerate(seqs):
            parts = []
            for r in range(HEADS_PER_GROUP):
                h = g * HEADS_PER_GROUP + r
                hr = slice(h * HEAD_DIM, (h + 1) * HEAD_DIM)
                parts.append(h0_ref[bb, hr, :] * dec_ref[b * N_HEADS + h]
                             + s_new[r * HEAD_DIM:(r + 1) * HEAD_DIM, bb * D_STATE:(bb + 1) * D_STATE])
            h_new = jnp.concatenate(parts, axis=0)
            hn_ref[bb, rows, :] = h_new
            h_all.append(h_new.astype(BF16))
        yt_scr[rows, :] += lax.dot_general(jnp.concatenate(h_all, axis=1), one_hot_rows(N_GROUPS + g),
                                           (((1,), (1,)), ((), ())), preferred_element_type=F32)


def _pssd_kernel(xbc_ref, z_ref, dt_ref, cw_ref, cb_ref, a_ref, expand_ref, dexp_ref, ng_ref,
                 dec_ref, h0_ref, xdt_ref, bc_ref, skip_ref, zs_ref,
                 y_ref, hfin_ref, tail_ref, hn_ref, ys_ref,
                 state_scr, xpad_scr, xc_scr, yt_scr, xdt_t_scr):
    c = pl.program_id(1)
    q = CHUNK
    step = pl.program_id(0) * pl.num_programs(1) + c

    @pl.when(step == 0)
    def _():
        yt_scr[...] = jnp.zeros_like(yt_scr)
        xdt_t_scr[...] = xdt_ref[...].T.astype(BF16)

    @pl.when(c == 0)
    def _():
        state_scr[...] = jnp.zeros_like(state_scr)
        xpad_scr[0:XCARRY, :] = jnp.zeros((XCARRY, D_XBC), F32)

    xpad_scr[XCARRY:XCARRY + q, :] = xbc_ref[...]
    for cb in range(D_XBC // 512):
        lanes = slice(cb * 512, (cb + 1) * 512)
        acc = jnp.broadcast_to(cb_ref[:, lanes], (q, 512))
        for j in range(SSM_CONV_W):
            k = SSM_CONV_W - 1 - j
            acc = acc + cw_ref[k:k + 1, lanes] * xpad_scr[XCARRY - j:XCARRY - j + q, lanes]
        xc_scr[:, lanes] = _silu(acc)
    xpad_scr[0:XCARRY, :] = xpad_scr[q:q + XCARRY, :]

    row_i = lax.broadcasted_iota(jnp.int32, (q, q), 0)
    col_i = lax.broadcasted_iota(jnp.int32, (q, q), 1)
    tril = row_i >= col_i
    tri = jnp.where(tril, 1.0, 0.0).astype(BF16)
    expand = expand_ref[...]

    dt = dt_ref[...]
    tri3 = jnp.concatenate([tri, tri, tri], axis=1)
    a_cs = _dot3_rhs(tri3, dt * a_ref[...])
    a_cs_t = a_cs.T
    dt_exp = _dot3_lhs(dt, expand)
    acs_exp = _dot3_lhs(a_cs, expand)
    last = acs_exp[q - 1:q, :]
    lane_lo = lax.broadcasted_iota(jnp.int32, (q, 128), 1) < HEAD_DIM

    for g in range(N_GROUPS):
        ch = slice(g * GROUP_W, (g + 1) * GROUP_W)
        xs = xc_scr[:, ch]
        bg = xc_scr[:, D_SSM + g * D_STATE:D_SSM + (g + 1) * D_STATE]
        cg = xc_scr[:, D_SSM + (N_GROUPS + g) * D_STATE:D_SSM + (N_GROUPS + g + 1) * D_STATE]
        bg16 = bg.astype(BF16)
        cg16 = cg.astype(BF16)
        xdt = xs * dt_exp[:, ch]
        acs_g = acs_exp[:, ch]
        cb = lax.dot_general(cg16, bg16, (((1,), (1,)), ((), ())), preferred_element_type=F32)
        st = state_scr[:, ch]
        y = _dot(cg16, st.astype(BF16)) * jnp.exp(acs_g)
        pieces = []
        for pr in range(HEADS_PER_GROUP // 2):
            xpair = xdt[:, pr * 128:(pr + 1) * 128].astype(BF16)
            both = []
            for half in range(2):
                h = g * HEADS_PER_GROUP + 2 * pr + half
                seg = a_cs[:, h:h + 1] - a_cs_t[h:h + 1, :]
                decay = jnp.exp(jnp.where(tril, seg, NEG_BIG))
                both.append(_dot((cb * decay).astype(BF16), xpair))
            pieces.append(jnp.where(lane_lo, both[0], both[1]))
        y = y + jnp.concatenate(pieces, axis=1) + dexp_ref[:, ch] * xs
        xdec = (xdt * jnp.exp(last[:, ch] - acs_g)).astype(BF16)
        s_new = lax.dot_general(bg16, xdec, (((0,), (0,)), ((), ())), preferred_element_type=F32)
        state_scr[:, ch] = st * jnp.exp(last[:, ch]) + s_new
        y_ref[:, ch] = _gated_norm(y, z_ref[:, ch], ng_ref[:, ch]).astype(BF16)

    _sample_state_update(step, dec_ref, h0_ref, bc_ref, hn_ref, yt_scr, xdt_t_scr)

    @pl.when(c == pl.num_programs(1) - 1)
    def _():
        hfin_ref[0] = state_scr[...].T
        tail_ref[0] = xpad_scr[0:XCARRY, :]

    @pl.when(step == pl.num_programs(0) * pl.num_programs(1) - 1)
    def _():
        y = yt_scr[...].T + skip_ref[...]
        for g in range(N_GROUPS):
            ch = slice(g * GROUP_W, (g + 1) * GROUP_W)
            ys_ref[:, ch] = _gated_norm(y[:, ch], zs_ref[:, ch], ng_ref[:, ch]).astype(BF16)


def _ssd(mix, dt, n_batch, seq, cw, cb, a_pad, d_exp, ng, dec, h0, xdt, bc, skip):
    nc = seq // CHUNK
    n_prompt = n_batch * seq
    n_samp = h0.shape[0]
    per_step = n_samp // (n_batch * nc)
    assert per_step * n_batch * nc == n_samp and n_prompt % n_samp == 0

    def rows(w, col=0):
        return pl.BlockSpec((CHUNK, w), lambda bi, c: (bi * nc + c, col))

    def vec(*s):
        return pl.BlockSpec(s, lambda bi, c: (0,) * len(s))

    states = pl.BlockSpec((per_step, D_SSM, D_STATE), lambda bi, c: (bi * nc + c, 0, 0))
    return pl.pallas_call(
        _pssd_kernel, grid=(n_batch, nc),
        in_specs=[rows(D_XBC), rows(D_SSM, Z_COL), rows(HEAD_PAD), vec(SSM_CONV_W, D_XBC),
                  vec(1, D_XBC), vec(1, HEAD_PAD), vec(3 * HEAD_PAD, D_SSM), vec(1, D_SSM),
                  vec(1, D_SSM), pl.BlockSpec(memory_space=pltpu.SMEM), states, vec(n_samp, D_SSM),
                  vec(n_samp, D_XBC - D_SSM), vec(n_samp, D_SSM),
                  pl.BlockSpec((n_samp, D_SSM), lambda bi, c: (n_prompt // n_samp, Z_COL))],
        out_specs=[rows(D_SSM), pl.BlockSpec((1, D_SSM, D_STATE), lambda bi, c: (bi, 0, 0)),
                   pl.BlockSpec((1, XCARRY, D_XBC), lambda bi, c: (bi, 0, 0)),
                   states, vec(n_samp, D_SSM)],
        out_shape=[jax.ShapeDtypeStruct((n_prompt, D_SSM), BF16),
                   jax.ShapeDtypeStruct((n_batch, D_SSM, D_STATE), F32),
                   jax.ShapeDtypeStruct((n_batch, XCARRY, D_XBC), F32),
                   jax.ShapeDtypeStruct((n_samp, D_SSM, D_STATE), F32),
                   jax.ShapeDtypeStruct((n_samp, D_SSM), BF16)],
        scratch_shapes=[pltpu.VMEM((D_STATE, D_SSM), F32), pltpu.VMEM((XCARRY + CHUNK, D_XBC), F32),
                        pltpu.VMEM((CHUNK, D_XBC), F32),
                        pltpu.VMEM((D_SSM, n_samp), F32), pltpu.VMEM((D_SSM, n_samp), BF16)],
        compiler_params=_params(("arbitrary", "arbitrary")), name="ssd",
    )(mix, mix, dt, cw, cb, a_pad, _head_expand_matrix(), d_exp, ng, dec, h0, xdt, bc, skip, mix)


def _sprep_kernel(v_ref, cst_ref, w_ref, b_ref, lg_ref, lb_ref,
                  xbc_ref, xst_ref, cw_ref, cb_ref, dt_ref, a_ref, dexp_ref,
                  co_ref, ncst_ref, nxst_ref, xdt_ref, bc_ref, dec_ref, skip_ref):
    nb = v_ref.shape[0]
    v = v_ref[...]
    acc = jnp.broadcast_to(b_ref[...], (nb, D_CONV)) + w_ref[CONV_W - 1:CONV_W, :] * v
    for k in range(CONV_W - 1):
        acc = acc + w_ref[k:k + 1, :] * cst_ref[k]
    co_ref[...] = _ln_swish(acc, lg_ref[...], lb_ref[...]).astype(BF16)
    for k in range(CONV_W - 2):
        ncst_ref[k] = cst_ref[k + 1]
    ncst_ref[CONV_W - 2] = v

    xn = xbc_ref[...]
    acc = jnp.broadcast_to(cb_ref[...], (nb, D_XBC)) + cw_ref[SSM_CONV_W - 1:SSM_CONV_W, :] * xn
    for k in range(SSM_CONV_W - 1):
        acc = acc + cw_ref[k:k + 1, :] * xst_ref[k]
    xc = _silu(acc)
    for k in range(SSM_CONV_W - 2):
        nxst_ref[k] = xst_ref[k + 1]
    nxst_ref[SSM_CONV_W - 2] = xn

    xs = xc[:, :D_SSM]
    bc_ref[...] = xc[:, D_SSM:]
    dt = dt_ref[...]
    dec_ref[...] = jnp.exp(dt * a_ref[...])
    dt_exp = _dot3_lhs(dt, _head_expand_matrix())
    xdt_ref[...] = xs * dt_exp
    skip_ref[...] = dexp_ref[...] * xs


def _sample_prep(mix, dt, n_prompt, n_samp, cst, xst, mw, mb, lg, lb, cw, cb, a_pad, d_exp):
    sb = SAMPLE_BLOCK
    off = n_prompt // sb
    srow = lambda w, col=0: pl.BlockSpec((sb, w), lambda i: (off + i, col))
    orow = lambda w: pl.BlockSpec((sb, w), lambda i: (i, 0))
    taps = lambda k, w: pl.BlockSpec((k, sb, w), lambda i: (0, i, 0))
    const = lambda *s: pl.BlockSpec(s, lambda i: (0,) * len(s))
    cst_t = jnp.swapaxes(cst, 0, 1)
    xst_t = jnp.swapaxes(xst, 0, 1)
    c_samp, ncst_t, nxst_t, xdt, bc, dec, skip = pl.pallas_call(
        _sprep_kernel, grid=(n_samp // sb,),
        in_specs=[srow(D_CONV, GLU_COL), taps(CONV_W - 1, D_CONV), const(CONV_W, D_CONV),
                  const(1, D_CONV), const(1, D_CONV), const(1, D_CONV),
                  srow(D_XBC), taps(SSM_CONV_W - 1, D_XBC), const(SSM_CONV_W, D_XBC),
                  const(1, D_XBC), srow(HEAD_PAD), const(1, HEAD_PAD), const(1, D_SSM)],
        out_specs=[orow(D_CONV), taps(CONV_W - 1, D_CONV), taps(SSM_CONV_W - 1, D_XBC),
                   orow(D_SSM), orow(D_XBC - D_SSM), orow(HEAD_PAD), orow(D_SSM)],
        out_shape=[jax.ShapeDtypeStruct((n_samp, D_CONV), BF16),
                   jax.ShapeDtypeStruct((CONV_W - 1, n_samp, D_CONV), F32),
                   jax.ShapeDtypeStruct((SSM_CONV_W - 1, n_samp, D_XBC), F32),
                   jax.ShapeDtypeStruct((n_samp, D_SSM), F32),
                   jax.ShapeDtypeStruct((n_samp, D_XBC - D_SSM), F32),
                   jax.ShapeDtypeStruct((n_samp, HEAD_PAD), F32),
                   jax.ShapeDtypeStruct((n_samp, D_SSM), F32)],
        compiler_params=_params(("arbitrary",)), name="sample_prep",
    )(mix, cst_t, mw, mb, lg, lb, mix, xst_t, cw, cb, dt, a_pad, d_exp)
    dec_flat = dec[:, :N_HEADS].reshape(-1)
    return (c_samp, jnp.swapaxes(ncst_t, 0, 1), jnp.swapaxes(nxst_t, 0, 1), xdt, bc, dec_flat, skip)


def _outproj_kernel(c_ref, cs_ref, y_ref, ys_ref, w_ref, h_ref, g_ref, o_ref):
    i = pl.program_id(0)
    last = pl.num_programs(0) - 1
    na = TM - cs_ref.shape[0]

    def run(c, y):
        mix = _dot(c, w_ref[0:D_CONV, :]) + _dot(y, w_ref[D_CONV:, :])
        o_ref[...] = h_ref[...] + _rms(mix, g_ref[...])

    @pl.when(i != last)
    def _():
        run(c_ref[...], y_ref[...])

    @pl.when(i == last)
    def _():
        run(jnp.concatenate([c_ref[0:na, :], cs_ref[...]], axis=0),
            jnp.concatenate([y_ref[0:na, :], ys_ref[...]], axis=0))


def _out_proj(c_prompt, c_samp, y_prompt, y_samp, w_out, h, g):
    m = h.shape[0]
    n_samp = c_samp.shape[0]
    row = pl.BlockSpec((TM, D_MODEL), lambda i: (i, 0))
    return pl.pallas_call(
        _outproj_kernel, grid=(m // TM,),
        in_specs=[pl.BlockSpec((TM, D_CONV), lambda i: (i, 0)),
                  pl.BlockSpec((n_samp, D_CONV), lambda i: (0, 0)),
                  pl.BlockSpec((TM, D_SSM), lambda i: (i, 0)),
                  pl.BlockSpec((n_samp, D_SSM), lambda i: (0, 0)),
                  pl.BlockSpec((D_CONV + D_SSM, D_MODEL), lambda i: (0, 0)),
                  row, pl.BlockSpec((1, D_MODEL), lambda i: (0, 0))],
        out_specs=row, out_shape=jax.ShapeDtypeStruct((m, D_MODEL), F32),
        compiler_params=_params(("arbitrary",)), name="out_proj",
    )(c_prompt, c_samp, y_prompt, y_samp, w_out, h, g)


def _ple_kernel(h_ref, pa_ref, pb_ref, gpre_ref, wg_ref, wp_ref, gpost_ref, oa_ref, ob_ref, emb_scr):
    i = pl.program_id(0)

    def embed(rows, src, srows):
        emb_scr[rows, :] = _dot(src[srows, :].astype(BF16), wp_ref[...])
    _on_tile_rows(i, pa_ref, pb_ref, embed)

    h = h_ref[...]
    gate = jax.nn.sigmoid(_dot(_rms(h, gpre_ref[...]).astype(BF16), wg_ref[...]))
    oa_ref[...] = h + _rms(gate * emb_scr[...], gpost_ref[...])

    @pl.when(i == pl.num_programs(0) - 1)
    def _():
        nb = ob_ref.shape[0]
        ob_ref[...] = oa_ref[TM - nb:TM, :]


def _ple(h, p_prompt, p_samp, gpre, wg, wp, gpost):
    m = h.shape[0]
    n_prompt, n_samp = p_prompt.shape[0], p_samp.shape[0]
    row = pl.BlockSpec((TM, D_MODEL), lambda i: (i, 0))
    vec = pl.BlockSpec((1, D_MODEL), lambda i: (0, 0))
    return pl.pallas_call(
        _ple_kernel, grid=(m // TM,),
        in_specs=[row, pl.BlockSpec((TM, PLE_DIM), lambda i: (i, 0)),
                  pl.BlockSpec((n_samp, PLE_DIM), lambda i: (0, 0)), vec,
                  pl.BlockSpec((D_MODEL, D_MODEL), lambda i: (0, 0)),
                  pl.BlockSpec((PLE_DIM, D_MODEL), lambda i: (0, 0)), vec],
        out_specs=[row, pl.BlockSpec((n_samp, D_MODEL), lambda i: (0, 0))],
        out_shape=[jax.ShapeDtypeStruct((n_prompt, D_MODEL), F32),
                   jax.ShapeDtypeStruct((n_samp, D_MODEL), F32)],
        scratch_shapes=[pltpu.VMEM((TM, D_MODEL), F32)],
        compiler_params=_params(("arbitrary",)), name="ple")(h, p_prompt, p_samp, gpre, wg, wp, gpost)


def _layer(x_prompt, x_samp, p_prompt, p_samp, n_batch, seq, cst, xst, h0, lw):
    (norm_ffn1_pre, w_ffn1_gate, w_ffn1_up, w_ffn1_down, norm_ffn1_post,
     norm_mix_pre, w_in, conv_mod_w, conv_mod_b, conv_mod_ln_g, conv_mod_ln_b,
     ssm_conv_w, ssm_conv_b, dt_bias, a_log, d_skip, ssm_norm_g, w_out, norm_mix_post,
     norm_ffn2_pre, w_ffn2_gate, w_ffn2_up, w_ffn2_down, norm_ffn2_post,
     norm_ple_pre, w_ple_gate, w_ple_proj, norm_ple_post) = lw
    n_prompt, n_samp = x_prompt.shape[0], x_samp.shape[0]
    m = n_prompt + n_samp
    _check_split(n_prompt, n_samp)
    row2 = lambda t: t.reshape(1, -1)

    later = ((jnp.swapaxes(w_in, 0, 1), 128, True), (w_out, 64, False), (w_ffn2_gate, 32, False),
             (w_ffn2_up, 32, False), (w_ffn2_down, 64, False), (w_ple_gate, 32, False),
             (w_ple_proj, 16, False))
    ffn1 = (row2(norm_ffn1_pre), row2(norm_ffn1_post), row2(norm_mix_pre))
    h1, u, wg1, wu1, wd1 = _ffn(
        x_prompt, ffn1[0], w_ffn1_gate, w_ffn1_up, w_ffn1_down, ffn1[1], gnext=ffn1[2],
        m=m, tiles=(0, 1), emit_w16=True, tf=TF_FIRST, tm=FIRST_TILES * TM, name="ffn_first")
    h1, u, w_in16, w_out16, wg2, wu2, wd2, wpg, wpp = _ffn(
        x_prompt, ffn1[0], wg1, wu1, wd1, ffn1[1], gnext=ffn1[2], x_tail=x_samp, casts=later,
        m=m, tiles=(FIRST_TILES, m // TM - FIRST_TILES), carry=(h1, u), name="ffn_rest")

    d_proj = w_in.shape[1]
    w_dt = jnp.pad(w_in16[:, d_proj - N_HEADS:], ((0, 0), (0, HEAD_PAD - N_HEADS)))
    pad_h = lambda t: jnp.pad(t.astype(F32), (0, HEAD_PAD - N_HEADS)).reshape(1, HEAD_PAD)
    mix, dt = _in_proj(u, w_in16, w_dt, pad_h(dt_bias))

    a = -jnp.exp(a_log.astype(F32))
    a_pad = pad_h(a)
    d_exp = row2(jnp.repeat(d_skip.astype(F32), HEAD_DIM))
    cw, cb = ssm_conv_w, row2(ssm_conv_b)
    mw, mb, lg, lb = conv_mod_w, row2(conv_mod_b), row2(conv_mod_ln_g), row2(conv_mod_ln_b)
    ng = row2(ssm_norm_g)

    c_prompt, glu_tail = _prompt_conv(mix, n_batch, seq, mw, mb, lg, lb)
    c_samp, new_cst_samp, new_xst_samp, xdt, bc, dec, skip = _sample_prep(
        mix, dt, n_prompt, n_samp, cst, xst, mw, mb, lg, lb, cw, cb, a_pad, d_exp)
    y_mix_prompt, hfin_prompt, xbc_tail, hn, y_mix_samp = _ssd(
        mix, dt, n_batch, seq, cw, cb, a_pad, d_exp, ng,
        dec, h0.reshape(n_samp, D_SSM, D_STATE), xdt, bc, skip)

    h2 = _out_proj(c_prompt, c_samp, y_mix_prompt, y_mix_samp, w_out16, h1, row2(norm_mix_post))
    (h3,) = _ffn(h2, row2(norm_ffn2_pre), wg2, wu2, wd2, row2(norm_ffn2_post))
    y_prompt, y_samp = _ple(h3, p_prompt, p_samp, row2(norm_ple_pre), wpg, wpp, row2(norm_ple_post))

    new_cst_prompt = glu_tail[:, CARRY - (CONV_W - 1):]
    new_xst_prompt = xbc_tail[:, XCARRY - (SSM_CONV_W - 1):]
    new_h_prompt = hfin_prompt.reshape(n_batch, N_HEADS, HEAD_DIM, D_STATE)
    new_h_samp = hn.reshape(n_samp, N_HEADS, HEAD_DIM, D_STATE)
    return (y_prompt, y_samp, new_cst_prompt, new_xst_prompt, new_h_prompt,
            new_cst_samp, new_xst_samp, new_h_samp)


def kernel(x_prompt, x_sample, state_conv_mod, state_ssm_conv, state_ssm, p_prompt, p_sample,
           norm_ffn1_pre, w_ffn1_gate, w_ffn1_up, w_ffn1_down, norm_ffn1_post,
           norm_mix_pre, w_in, conv_mod_w, conv_mod_b, conv_mod_ln_g, conv_mod_ln_b,
           ssm_conv_w, ssm_conv_b, dt_bias, a_log, d_skip, ssm_norm_g, w_out, norm_mix_post,
           norm_ffn2_pre, w_ffn2_gate, w_ffn2_up, w_ffn2_down, norm_ffn2_post,
           norm_ple_pre, w_ple_gate, w_ple_proj, norm_ple_post):
    weights = (norm_ffn1_pre, w_ffn1_gate, w_ffn1_up, w_ffn1_down, norm_ffn1_post,
               norm_mix_pre, w_in, conv_mod_w, conv_mod_b, conv_mod_ln_g, conv_mod_ln_b,
               ssm_conv_w, ssm_conv_b, dt_bias, a_log, d_skip, ssm_norm_g, w_out, norm_mix_post,
               norm_ffn2_pre, w_ffn2_gate, w_ffn2_up, w_ffn2_down, norm_ffn2_post,
               norm_ple_pre, w_ple_gate, w_ple_proj, norm_ple_post)
    n_batch, seq, _ = x_prompt.shape
    n_samp = x_sample.shape[0]
    n_prompt = n_batch * seq
    depth = norm_ffn1_pre.shape[0]
    xp = x_prompt.reshape(n_prompt, D_MODEL)
    xs = x_sample.reshape(n_samp, D_MODEL)
    outs = [[] for _ in range(6)]
    for i in range(depth):
        res = _layer(xp, xs, p_prompt[i].reshape(n_prompt, PLE_DIM), p_sample[i].reshape(n_samp, PLE_DIM),
                     n_batch, seq, state_conv_mod[i], state_ssm_conv[i], state_ssm[i],
                     tuple(w[i] for w in weights))
        xp, xs = res[0], res[1]
        for lst, r in zip(outs, res[2:]):
            lst.append(r)
    return ((xp.reshape(n_batch, seq, D_MODEL), xs.reshape(n_samp, 1, D_MODEL))
            + tuple(jnp.stack(lst, axis=0) for lst in outs))
```

```python
import functools

import jax
import jax.numpy as jnp
from jax import lax
from jax.experimental import pallas as pl
from jax.experimental.pallas import tpu as pltpu

F32 = jnp.float32
BF16 = jnp.bfloat16

D_MODEL = 2048
D_FF = 5632
D_CONV = 1024
D_SSM = 3072
N_HEADS = 48
HEAD_DIM = 64
N_GROUPS = 8
HEADS_PER_GROUP = 6
GROUP_W = HEADS_PER_GROUP * HEAD_DIM
D_STATE = 128
D_XBC = D_SSM + 2 * N_GROUPS * D_STATE
MIX_W = D_XBC + D_CONV + D_SSM
GLU_COL = D_XBC // D_CONV
Z_COL = (D_XBC + D_CONV) // D_SSM
assert GLU_COL * D_CONV == D_XBC and Z_COL * D_SSM == D_XBC + D_CONV
CONV_W = 31
SSM_CONV_W = 4
CHUNK = 128
PLE_DIM = 256
EPS = 1e-6
NEG_BIG = -1e30

LANE = 128
SUBLANE = 8
BF16_ROWS = 16
HEAD_PAD = LANE
HEAD_SHIFT = HEAD_DIM.bit_length() - 1
assert 1 << HEAD_SHIFT == HEAD_DIM and 2 * HEAD_DIM == LANE

TM = 640
TF = 512
TF_FIRST = 256
FIRST_TILES = 2
TMP = 1664
TN = 1024
TL = 256
CARRY = 32
CONV_ROWS = 64
NORM_ROWS = 16
XCARRY = SUBLANE
SAMPLE_BLOCK = 32
VMEM_LIMIT = 56 * 1024 * 1024
VMEM_LIMIT_MAX = 58 * 1024 * 1024


def _params(dims, vmem=VMEM_LIMIT):
    return pltpu.CompilerParams(dimension_semantics=dims, vmem_limit_bytes=vmem)


def _rms(x, g):
    return x * lax.rsqrt(jnp.mean(x * x, axis=-1, keepdims=True) + EPS) * g


def _silu(x):
    return x * jax.nn.sigmoid(x)


def _dot(a, b):
    return jnp.dot(a, b, preferred_element_type=F32)


def _split3(x):
    hi = x.astype(BF16)
    r = x - hi.astype(F32)
    mid = r.astype(BF16)
    lo = (r - mid.astype(F32)).astype(BF16)
    return hi, mid, lo


def _dot3_rhs(a3_bf16, x):
    return _dot(a3_bf16, jnp.concatenate(_split3(x), axis=0))


def _dot3_lhs(x, b3_bf16):
    return _dot(jnp.concatenate(_split3(x), axis=1), b3_bf16)


def _head_expand_matrix():
    head = lax.broadcasted_iota(jnp.int32, (3 * HEAD_PAD, D_SSM), 0) & (HEAD_PAD - 1)
    chan = lax.broadcasted_iota(jnp.int32, (3 * HEAD_PAD, D_SSM), 1)
    return jnp.where((chan >> HEAD_SHIFT) == head, 1.0, 0.0).astype(BF16)


def _on_tile_rows(i, a_ref, b_ref, fn):
    tm = a_ref.shape[0]
    if b_ref is None:
        fn(slice(0, tm), a_ref, slice(0, tm))
        return
    nb = b_ref.shape[0]
    na = tm - nb
    last = pl.num_programs(0) - 1

    @pl.when(i != last)
    def _():
        fn(slice(0, tm), a_ref, slice(0, tm))

    @pl.when(i == last)
    def _():
        fn(slice(0, na), a_ref, slice(0, na))
        fn(slice(na, tm), b_ref, slice(0, nb))


def _cast_specs(casts, steps, step_of):
    in_specs, out_specs, out_shapes = [], [], []
    for w, r, transpose in casts:
        nblk = pl.cdiv(w.shape[0], r)
        assert r % BF16_ROWS == 0 and nblk <= steps and (w.shape[0] % r == 0 or transpose)
        slab = lambda *idx, nblk=nblk: jnp.minimum(step_of(*idx), nblk - 1)
        in_specs.append(pl.BlockSpec((r, w.shape[1]), lambda *idx, slab=slab: (slab(*idx), 0)))
        if transpose:
            assert r % LANE == 0
            out_specs.append(pl.BlockSpec((w.shape[1], r), lambda *idx, slab=slab: (0, slab(*idx))))
            out_shapes.append(jax.ShapeDtypeStruct(w.shape[::-1], BF16))
        else:
            out_specs.append(pl.BlockSpec((r, w.shape[1]), lambda *idx, slab=slab: (slab(*idx), 0)))
            out_shapes.append(jax.ShapeDtypeStruct(w.shape, BF16))
    return in_specs, out_specs, out_shapes


def _run_casts(cast_in, cast_out, cast_t):
    for ci, co, transpose in zip(cast_in, cast_out, cast_t):
        co[...] = (ci[...].T if transpose else ci[...]).astype(BF16)


def _check_split(n_prompt, n_samp):
    assert (n_prompt + n_samp) % TM == 0 and n_samp < TM and n_samp % BF16_ROWS == 0


def _ffn_kernel(*refs, split, n_next, cast_t, emit_w16, n_carry):
    n_cast = len(cast_t)
    refs = list(refs)
    xa_ref = refs.pop(0)
    xb_ref = refs.pop(0) if split else None
    gpre_ref, wg_ref, wu_ref, wd_ref, gpost_ref = refs[:5]
    refs = refs[5:]
    gnext_ref = refs.pop(0) if n_next else None
    cast_in, refs = refs[:n_cast], refs[n_cast:]
    refs = refs[n_carry:]
    o_ref = refs.pop(0)
    unext_ref = refs.pop(0) if n_next else None
    cast_out, refs = refs[:n_cast], refs[n_cast:]
    w16_refs, refs = (refs[:3], refs[3:]) if emit_w16 else ((), refs)
    (u_scr,) = refs
    i = pl.program_id(0)
    j = pl.program_id(1)

    @pl.when(j == 0)
    def _():
        def pre(rows, src, srows):
            u_scr[rows, :] = _rms(src[srows, :], gpre_ref[...]).astype(BF16)
        _on_tile_rows(i, xa_ref, xb_ref, pre)
        o_ref[...] = jnp.zeros_like(o_ref)

    _run_casts(cast_in, cast_out, cast_t)

    wg, wu, wd = wg_ref[...], wu_ref[...], wd_ref[...]
    if emit_w16:
        wg, wu, wd = wg.astype(BF16), wu.astype(BF16), wd.astype(BF16)
        for ref, w in zip(w16_refs, (wg, wu, wd)):
            ref[...] = w
    u = u_scr[...]
    act = (_silu(_dot(u, wg)) * _dot(u, wu)).astype(BF16)
    o_ref[...] += _dot(act, wd)

    @pl.when(j == pl.num_programs(1) - 1)
    def _():
        def post(rows, src, srows):
            step = rows.stop - rows.start if n_next else NORM_ROWS
            for r in range(0, rows.stop - rows.start, step):
                dst = slice(rows.start + r, rows.start + r + step)
                h = src[srows.start + r:srows.start + r + step, :] + _rms(o_ref[dst, :], gpost_ref[...])
                o_ref[dst, :] = h
                if n_next:
                    unext_ref[dst, :] = _rms(h, gnext_ref[...]).astype(BF16)
        _on_tile_rows(i, xa_ref, xb_ref, post)


def _ffn(x, gpre, wg, wu, wd, gpost, gnext=None, x_tail=None, casts=(), m=None, tiles=None,
         carry=(), emit_w16=False, tf=TF, tm=TM, name="ffn"):
    if m is None:
        m = x.shape[0] + (0 if x_tail is None else x_tail.shape[0])
    first, count = tiles if tiles is not None else (0, m // tm)
    grid = (count, D_FF // tf)
    steps = grid[0] * grid[1]
    row = pl.BlockSpec((tm, D_MODEL), lambda i, j: (i + first, 0))
    vec = pl.BlockSpec((1, D_MODEL), lambda i, j: (0, 0))
    wcol = pl.BlockSpec((D_MODEL, tf), lambda i, j: (0, j))
    wrow = pl.BlockSpec((tf, D_MODEL), lambda i, j: (j, 0))
    in_specs, args = [row], [x]
    if x_tail is not None:
        in_specs.append(pl.BlockSpec(x_tail.shape, lambda i, j: (0, 0)))
        args.append(x_tail)
    in_specs += [vec, wcol, wcol, wrow, vec]
    args += [gpre, wg, wu, wd, 0.5 * gpost]
    out_shape = [jax.ShapeDtypeStruct((m, D_MODEL), F32)]
    out_specs = [row]
    if gnext is not None:
        in_specs.append(vec)
        args.append(gnext)
        out_shape.append(jax.ShapeDtypeStruct((m, D_MODEL), BF16))
        out_specs.append(row)
    assert len(carry) in (0, len(out_shape))
    cast_in_specs, cast_out_specs, cast_shapes = _cast_specs(casts, steps, lambda i, j: i * grid[1] + j)
    in_specs += cast_in_specs
    args += [w for w, _, _ in casts]
    out_specs += cast_out_specs
    out_shape += cast_shapes
    aliases = {}
    for k, c in enumerate(carry):
        aliases[len(args)] = k
        in_specs.append(pl.BlockSpec(memory_space=pl.ANY))
        args.append(c)
    if emit_w16:
        out_specs += [wcol, wcol, wrow]
        out_shape += [jax.ShapeDtypeStruct(w.shape, BF16) for w in (wg, wu, wd)]
    return pl.pallas_call(
        functools.partial(_ffn_kernel, split=x_tail is not None, n_next=gnext is not None,
                          cast_t=tuple(t for _, _, t in casts), emit_w16=emit_w16,
                          n_carry=len(carry)),
        grid=grid, in_specs=in_specs, out_specs=out_specs, out_shape=out_shape,
        scratch_shapes=[pltpu.VMEM((tm, D_MODEL), BF16)],
        input_output_aliases=aliases,
        compiler_params=_params(("arbitrary", "arbitrary")), name=name,
    )(*args)


N_GLU_STEPS = D_CONV // TN
N_Z_STEPS = D_SSM // TN
N_XBC_STEPS = D_XBC // TN


def _inproj_kernel(u_ref, w_ref, wb_ref, wdt_ref, dtb_ref, o_ref, dt_ref):
    j = pl.program_id(1)

    @pl.when(j < N_GLU_STEPS)
    def _():
        u = u_ref[...]
        o_ref[...] = _dot(u, w_ref[...]) * jax.nn.sigmoid(_dot(u, wb_ref[...]))

    @pl.when(j >= N_GLU_STEPS)
    def _():
        o_ref[...] = _dot(u_ref[...], w_ref[...])

    @pl.when(j == pl.num_programs(1) - 1)
    def _():
        x = _dot(u_ref[...], wdt_ref[...]) + dtb_ref[...]
        dt_ref[...] = jnp.maximum(x, 0.0) + jnp.log1p(jnp.exp(-jnp.abs(x)))


def _in_proj(u, w_in, w_dt, dt_bias):
    m = u.shape[0]
    g, nz, nx = N_GLU_STEPS, N_Z_STEPS, N_XBC_STEPS
    assert m % TMP == 0 and g == 1

    def out_slab(j):
        return jnp.where(j < g + nz, j + nx, j - g - nz)

    return pl.pallas_call(
        _inproj_kernel, grid=(m // TMP, g + nz + nx),
        in_specs=[pl.BlockSpec((TMP, D_MODEL), lambda i, j: (i, 0)),
                  pl.BlockSpec((D_MODEL, TN), lambda i, j: (0, jnp.where(j < g, j, j + g))),
                  pl.BlockSpec((D_MODEL, TN), lambda i, j: (0, g)),
                  pl.BlockSpec((D_MODEL, HEAD_PAD), lambda i, j: (0, 0)),
                  pl.BlockSpec((1, HEAD_PAD), lambda i, j: (0, 0))],
        out_specs=[pl.BlockSpec((TMP, TN), lambda i, j: (i, out_slab(j))),
                   pl.BlockSpec((TMP, HEAD_PAD), lambda i, j: (i, 0))],
        out_shape=[jax.ShapeDtypeStruct((m, MIX_W), F32), jax.ShapeDtypeStruct((m, HEAD_PAD), F32)],
        compiler_params=_params(("arbitrary", "arbitrary"), VMEM_LIMIT_MAX), name="in_proj",
    )(u, w_in, w_in, w_dt, dt_bias)


def _ln_swish(y, g, b):
    mu = jnp.mean(y, axis=-1, keepdims=True)
    yc = y - mu
    yn = yc * lax.rsqrt(jnp.mean(yc * yc, axis=-1, keepdims=True) + EPS) * g + b
    return _silu(yn)


def _pconv_kernel(v_ref, w_ref, b_ref, lg_ref, lb_ref, *rest, cast_t):
    n_cast = len(cast_t)
    cast_in, (o_ref, tail_ref), rest = rest[:n_cast], rest[n_cast:n_cast + 2], rest[n_cast + 2:]
    cast_out, (xpad_scr, conv_scr) = rest[:n_cast], rest[n_cast:]
    t = pl.program_id(1)
    _run_casts(cast_in, cast_out, cast_t)

    @pl.when(t == 0)
    def _():
        xpad_scr[0:CARRY, :] = jnp.zeros((CARRY, D_CONV), F32)

    xpad_scr[CARRY:CARRY + TL, :] = v_ref[...]
    first = CARRY - (CONV_W - 1)
    hb = CONV_ROWS
    n_rows = CARRY + TL
    for cb in range(D_CONV // LANE):
        lanes = slice(cb * LANE, (cb + 1) * LANE)
        xfull = xpad_scr[:, lanes]
        conv_scr[:, lanes] = jnp.broadcast_to(b_ref[:, lanes], (TL, LANE))
        for phase in range(SUBLANE):
            xs = pltpu.roll(xfull, n_rows - phase, axis=0) if phase else xfull
            taps = [k for k in range(CONV_W) if (first + k) % SUBLANE == phase]
            for base in range(0, TL, hb):
                acc = conv_scr[base:base + hb, lanes]
                for k in taps:
                    off = SUBLANE * ((first + k) // SUBLANE)
                    acc = acc + w_ref[k:k + 1, lanes] * xs[base + off:base + off + hb]
                conv_scr[base:base + hb, lanes] = acc
    xpad_scr[0:CARRY, :] = xpad_scr[TL:TL + CARRY, :]
    for r in range(0, TL, NORM_ROWS):
        rows = slice(r, r + NORM_ROWS)
        o_ref[rows, :] = _ln_swish(conv_scr[rows, :], lg_ref[...], lb_ref[...]).astype(BF16)

    @pl.when(t == pl.num_programs(1) - 1)
    def _():
        tail_ref[0] = xpad_scr[0:CARRY, :]


def _prompt_conv(mix, n_batch, seq, w, b, lg, lb, casts=()):
    vec = pl.BlockSpec((1, D_CONV), lambda bi, t: (0, 0))
    steps = seq // TL
    cast_in_specs, cast_out_specs, cast_shapes = _cast_specs(
        casts, n_batch * steps, lambda bi, t: bi * steps + t)
    return pl.pallas_call(
        functools.partial(_pconv_kernel, cast_t=tuple(t for _, _, t in casts)),
        grid=(n_batch, steps),
        in_specs=[pl.BlockSpec((TL, D_CONV), lambda bi, t: (bi * steps + t, GLU_COL)),
                  pl.BlockSpec((CONV_W, D_CONV), lambda bi, t: (0, 0)), vec, vec, vec] + cast_in_specs,
        out_specs=[pl.BlockSpec((TL, D_CONV), lambda bi, t: (bi * steps + t, 0)),
                   pl.BlockSpec((1, CARRY, D_CONV), lambda bi, t: (bi, 0, 0))] + cast_out_specs,
        out_shape=[jax.ShapeDtypeStruct((n_batch * seq, D_CONV), BF16),
                   jax.ShapeDtypeStruct((n_batch, CARRY, D_CONV), F32)] + cast_shapes,
        scratch_shapes=[pltpu.VMEM((CARRY + TL, D_CONV), F32), pltpu.VMEM((TL, D_CONV), F32)],
        compiler_params=_params(("arbitrary", "arbitrary")), name="prompt_conv",
    )(mix, w, b, lg, lb, *[c for c, _, _ in casts])


def _gated_norm(y, z, g):
    yg = y * _silu(z)
    return yg * lax.rsqrt(jnp.mean(yg * yg, axis=-1, keepdims=True) + EPS) * g


def _sample_state_update(step, dec_ref, h0_ref, bc_ref, hn_ref, yt_scr, xdt_t_scr):
    per_step = h0_ref.shape[0]
    nb = bc_ref.shape[0]
    seq_i = lax.broadcasted_iota(jnp.int32, (nb, D_STATE), 0)
    seqs = [step * per_step + bb for bb in range(per_step)]
    brows = [bc_ref[pl.ds(b, 1), :] for b in seqs]

    def one_hot_rows(g):
        cols = slice(g * D_STATE, (g + 1) * D_STATE)
        return jnp.concatenate([jnp.where(seq_i == b, brow[:, cols], 0.0).astype(BF16)
                                for b, brow in zip(seqs, brows)], axis=1)

    for g in range(N_GROUPS):
        rows = slice(g * GROUP_W, (g + 1) * GROUP_W)
        s_new = _dot(xdt_t_scr[rows, :], one_hot_rows(g))
        h_all = []
        for bb, b in enumerate(seqs):
            parts = []
            for r in range(HEADS_PER_GROUP):
                h = g * HEADS_PER_GROUP + r
                hr = slice(h * HEAD_DIM, (h + 1) * HEAD_DIM)
                parts.append(h0_ref[bb, hr, :] * dec_ref[b * N_HEADS + h]
                             + s_new[r * HEAD_DIM:(r + 1) * HEAD_DIM, bb * D_STATE:(bb + 1) * D_STATE])
            h_new = jnp.concatenate(parts, axis=0)
            hn_ref[bb, rows, :] = h_new
            h_all.append(h_new.astype(BF16))
        yt_scr[rows, :] += lax.dot_general(jnp.concatenate(h_all, axis=1), one_hot_rows(N_GROUPS + g),
                                           (((1,), (1,)), ((), ())), preferred_element_type=F32)


def _pssd_kernel(xbc_ref, z_ref, dt_ref, cw_ref, cb_ref, a_ref, expand_ref, dexp_ref, ng_ref,
                 dec_ref, h0_ref, xdt_ref, bc_ref, skip_ref, zs_ref,
                 y_ref, hfin_ref, tail_ref, hn_ref, ys_ref,
                 state_scr, xpad_scr, xc_scr, yt_scr, xdt_t_scr):
    c = pl.program_id(1)
    q = CHUNK
    step = pl.program_id(0) * pl.num_programs(1) + c

    @pl.when(step == 0)
    def _():
        yt_scr[...] = jnp.zeros_like(yt_scr)
        xdt_t_scr[...] = xdt_ref[...].T.astype(BF16)

    @pl.when(c == 0)
    def _():
        state_scr[...] = jnp.zeros_like(state_scr)
        xpad_scr[0:XCARRY, :] = jnp.zeros((XCARRY, D_XBC), F32)

    xpad_scr[XCARRY:XCARRY + q, :] = xbc_ref[...]
    for cb in range(D_XBC // 512):
        lanes = slice(cb * 512, (cb + 1) * 512)
        acc = jnp.broadcast_to(cb_ref[:, lanes], (q, 512))
        for j in range(SSM_CONV_W):
            k = SSM_CONV_W - 1 - j
            acc = acc + cw_ref[k:k + 1, lanes] * xpad_scr[XCARRY - j:XCARRY - j + q, lanes]
        xc_scr[:, lanes] = _silu(acc)
    xpad_scr[0:XCARRY, :] = xpad_scr[q:q + XCARRY, :]

    row_i = lax.broadcasted_iota(jnp.int32, (q, q), 0)
    col_i = lax.broadcasted_iota(jnp.int32, (q, q), 1)
    tril = row_i >= col_i
    tri = jnp.where(tril, 1.0, 0.0).astype(BF16)
    expand = expand_ref[...]

    dt = dt_ref[...]
    tri3 = jnp.concatenate([tri, tri, tri], axis=1)
    a_cs = _dot3_rhs(tri3, dt * a_ref[...])
    a_cs_t = a_cs.T
    dt_exp = _dot3_lhs(dt, expand)
    acs_exp = _dot3_lhs(a_cs, expand)
    last = acs_exp[q - 1:q, :]
    lane_lo = lax.broadcasted_iota(jnp.int32, (q, LANE), 1) < HEAD_DIM

    for g in range(N_GROUPS):
        ch = slice(g * GROUP_W, (g + 1) * GROUP_W)
        xs = xc_scr[:, ch]
        bg = xc_scr[:, D_SSM + g * D_STATE:D_SSM + (g + 1) * D_STATE]
        cg = xc_scr[:, D_SSM + (N_GROUPS + g) * D_STATE:D_SSM + (N_GROUPS + g + 1) * D_STATE]
        bg16 = bg.astype(BF16)
        cg16 = cg.astype(BF16)
        xdt = xs * dt_exp[:, ch]
        acs_g = acs_exp[:, ch]
        cb = lax.dot_general(cg16, bg16, (((1,), (1,)), ((), ())), preferred_element_type=F32)
        st = state_scr[:, ch]
        y = _dot(cg16, st.astype(BF16)) * jnp.exp(acs_g)
        pieces = []
        for pr in range(HEADS_PER_GROUP // 2):
            xpair = xdt[:, pr * LANE:(pr + 1) * LANE].astype(BF16)
            both = []
            for half in range(2):
                h = g * HEADS_PER_GROUP + 2 * pr + half
                seg = a_cs[:, h:h + 1] - a_cs_t[h:h + 1, :]
                decay = jnp.exp(jnp.where(tril, seg, NEG_BIG))
                both.append(_dot((cb * decay).astype(BF16), xpair))
            pieces.append(jnp.where(lane_lo, both[0], both[1]))
        y = y + jnp.concatenate(pieces, axis=1) + dexp_ref[:, ch] * xs
        xdec = (xdt * jnp.exp(last[:, ch] - acs_g)).astype(BF16)
        s_new = lax.dot_general(bg16, xdec, (((0,), (0,)), ((), ())), preferred_element_type=F32)
        state_scr[:, ch] = st * jnp.exp(last[:, ch]) + s_new
        y_ref[:, ch] = _gated_norm(y, z_ref[:, ch], ng_ref[:, ch]).astype(BF16)

    _sample_state_update(step, dec_ref, h0_ref, bc_ref, hn_ref, yt_scr, xdt_t_scr)

    @pl.when(c == pl.num_programs(1) - 1)
    def _():
        hfin_ref[0] = state_scr[...].T
        tail_ref[0] = xpad_scr[0:XCARRY, :]

    @pl.when(step == pl.num_programs(0) * pl.num_programs(1) - 1)
    def _():
        y = yt_scr[...].T + skip_ref[...]
        for g in range(N_GROUPS):
            ch = slice(g * GROUP_W, (g + 1) * GROUP_W)
            ys_ref[:, ch] = _gated_norm(y[:, ch], zs_ref[:, ch], ng_ref[:, ch]).astype(BF16)


def _ssd(mix, dt, n_batch, seq, cw, cb, a_pad, d_exp, ng, dec, h0, xdt, bc, skip):
    nc = seq // CHUNK
    n_prompt = n_batch * seq
    n_samp = h0.shape[0]
    per_step = n_samp // (n_batch * nc)
    assert per_step * n_batch * nc == n_samp and n_prompt % n_samp == 0

    def rows(w, col=0):
        return pl.BlockSpec((CHUNK, w), lambda bi, c: (bi * nc + c, col))

    def vec(*s):
        return pl.BlockSpec(s, lambda bi, c: (0,) * len(s))

    states = pl.BlockSpec((per_step, D_SSM, D_STATE), lambda bi, c: (bi * nc + c, 0, 0))
    return pl.pallas_call(
        _pssd_kernel, grid=(n_batch, nc),
        in_specs=[rows(D_XBC), rows(D_SSM, Z_COL), rows(HEAD_PAD), vec(SSM_CONV_W, D_XBC),
                  vec(1, D_XBC), vec(1, HEAD_PAD), vec(3 * HEAD_PAD, D_SSM), vec(1, D_SSM),
                  vec(1, D_SSM), pl.BlockSpec(memory_space=pltpu.SMEM), states, vec(n_samp, D_SSM),
                  vec(n_samp, D_XBC - D_SSM), vec(n_samp, D_SSM),
                  pl.BlockSpec((n_samp, D_SSM), lambda bi, c: (n_prompt // n_samp, Z_COL))],
        out_specs=[rows(D_SSM), pl.BlockSpec((1, D_SSM, D_STATE), lambda bi, c: (bi, 0, 0)),
                   pl.BlockSpec((1, XCARRY, D_XBC), lambda bi, c: (bi, 0, 0)),
                   states, vec(n_samp, D_SSM)],
        out_shape=[jax.ShapeDtypeStruct((n_prompt, D_SSM), BF16),
                   jax.ShapeDtypeStruct((n_batch, D_SSM, D_STATE), F32),
                   jax.ShapeDtypeStruct((n_batch, XCARRY, D_XBC), F32),
                   jax.ShapeDtypeStruct((n_samp, D_SSM, D_STATE), F32),
                   jax.ShapeDtypeStruct((n_samp, D_SSM), BF16)],
        scratch_shapes=[pltpu.VMEM((D_STATE, D_SSM), F32), pltpu.VMEM((XCARRY + CHUNK, D_XBC), F32),
                        pltpu.VMEM((CHUNK, D_XBC), F32),
                        pltpu.VMEM((D_SSM, n_samp), F32), pltpu.VMEM((D_SSM, n_samp), BF16)],
        compiler_params=_params(("arbitrary", "arbitrary")), name="ssd",
    )(mix, mix, dt, cw, cb, a_pad, _head_expand_matrix(), d_exp, ng, dec, h0, xdt, bc, skip, mix)


def _sprep_kernel(v_ref, cst_ref, w_ref, b_ref, lg_ref, lb_ref,
                  xbc_ref, xst_ref, cw_ref, cb_ref, dt_ref, a_ref, dexp_ref,
                  co_ref, ncst_ref, nxst_ref, xdt_ref, bc_ref, dec_ref, skip_ref):
    nb = v_ref.shape[0]
    v = v_ref[...]
    acc = jnp.broadcast_to(b_ref[...], (nb, D_CONV)) + w_ref[CONV_W - 1:CONV_W, :] * v
    for k in range(CONV_W - 1):
        acc = acc + w_ref[k:k + 1, :] * cst_ref[k]
    co_ref[...] = _ln_swish(acc, lg_ref[...], lb_ref[...]).astype(BF16)
    for k in range(CONV_W - 2):
        ncst_ref[k] = cst_ref[k + 1]
    ncst_ref[CONV_W - 2] = v

    xn = xbc_ref[...]
    acc = jnp.broadcast_to(cb_ref[...], (nb, D_XBC)) + cw_ref[SSM_CONV_W - 1:SSM_CONV_W, :] * xn
    for k in range(SSM_CONV_W - 1):
        acc = acc + cw_ref[k:k + 1, :] * xst_ref[k]
    xc = _silu(acc)
    for k in range(SSM_CONV_W - 2):
        nxst_ref[k] = xst_ref[k + 1]
    nxst_ref[SSM_CONV_W - 2] = xn

    xs = xc[:, :D_SSM]
    bc_ref[...] = xc[:, D_SSM:]
    dt = dt_ref[...]
    dec_ref[...] = jnp.exp(dt * a_ref[...])
    dt_exp = _dot3_lhs(dt, _head_expand_matrix())
    xdt_ref[...] = xs * dt_exp
    skip_ref[...] = dexp_ref[...] * xs


def _sample_prep(mix, dt, n_prompt, n_samp, cst, xst, mw, mb, lg, lb, cw, cb, a_pad, d_exp):
    sb = SAMPLE_BLOCK
    off = n_prompt // sb
    srow = lambda w, col=0: pl.BlockSpec((sb, w), lambda i: (off + i, col))
    orow = lambda w: pl.BlockSpec((sb, w), lambda i: (i, 0))
    taps = lambda k, w: pl.BlockSpec((k, sb, w), lambda i: (0, i, 0))
    const = lambda *s: pl.BlockSpec(s, lambda i: (0,) * len(s))
    cst_t = jnp.swapaxes(cst, 0, 1)
    xst_t = jnp.swapaxes(xst, 0, 1)
    c_samp, ncst_t, nxst_t, xdt, bc, dec, skip = pl.pallas_call(
        _sprep_kernel, grid=(n_samp // sb,),
        in_specs=[srow(D_CONV, GLU_COL), taps(CONV_W - 1, D_CONV), const(CONV_W, D_CONV),
                  const(1, D_CONV), const(1, D_CONV), const(1, D_CONV),
                  srow(D_XBC), taps(SSM_CONV_W - 1, D_XBC), const(SSM_CONV_W, D_XBC),
                  const(1, D_XBC), srow(HEAD_PAD), const(1, HEAD_PAD), const(1, D_SSM)],
        out_specs=[orow(D_CONV), taps(CONV_W - 1, D_CONV), taps(SSM_CONV_W - 1, D_XBC),
                   orow(D_SSM), orow(D_XBC - D_SSM), orow(HEAD_PAD), orow(D_SSM)],
        out_shape=[jax.ShapeDtypeStruct((n_samp, D_CONV), BF16),
                   jax.ShapeDtypeStruct((CONV_W - 1, n_samp, D_CONV), F32),
                   jax.ShapeDtypeStruct((SSM_CONV_W - 1, n_samp, D_XBC), F32),
                   jax.ShapeDtypeStruct((n_samp, D_SSM), F32),
                   jax.ShapeDtypeStruct((n_samp, D_XBC - D_SSM), F32),
                   jax.ShapeDtypeStruct((n_samp, HEAD_PAD), F32),
                   jax.ShapeDtypeStruct((n_samp, D_SSM), F32)],
        compiler_params=_params(("arbitrary",)), name="sample_prep",
    )(mix, cst_t, mw, mb, lg, lb, mix, xst_t, cw, cb, dt, a_pad, d_exp)
    dec_flat = dec[:, :N_HEADS].reshape(-1)
    return (c_samp, jnp.swapaxes(ncst_t, 0, 1), jnp.swapaxes(nxst_t, 0, 1), xdt, bc, dec_flat, skip)


def _outproj_kernel(c_ref, cs_ref, y_ref, ys_ref, w_ref, h_ref, g_ref, o_ref):
    i = pl.program_id(0)
    last = pl.num_programs(0) - 1
    na = TM - cs_ref.shape[0]

    def run(c, y):
        mix = _dot(c, w_ref[0:D_CONV, :]) + _dot(y, w_ref[D_CONV:, :])
        o_ref[...] = h_ref[...] + _rms(mix, g_ref[...])

    @pl.when(i != last)
    def _():
        run(c_ref[...], y_ref[...])

    @pl.when(i == last)
    def _():
        run(jnp.concatenate([c_ref[0:na, :], cs_ref[...]], axis=0),
            jnp.concatenate([y_ref[0:na, :], ys_ref[...]], axis=0))


def _out_proj(c_prompt, c_samp, y_prompt, y_samp, w_out, h, g):
    m = h.shape[0]
    n_samp = c_samp.shape[0]
    row = pl.BlockSpec((TM, D_MODEL), lambda i: (i, 0))
    return pl.pallas_call(
        _outproj_kernel, grid=(m // TM,),
        in_specs=[pl.BlockSpec((TM, D_CONV), lambda i: (i, 0)),
                  pl.BlockSpec((n_samp, D_CONV), lambda i: (0, 0)),
                  pl.BlockSpec((TM, D_SSM), lambda i: (i, 0)),
                  pl.BlockSpec((n_samp, D_SSM), lambda i: (0, 0)),
                  pl.BlockSpec((D_CONV + D_SSM, D_MODEL), lambda i: (0, 0)),
                  row, pl.BlockSpec((1, D_MODEL), lambda i: (0, 0))],
        out_specs=row, out_shape=jax.ShapeDtypeStruct((m, D_MODEL), F32),
        compiler_params=_params(("arbitrary",)), name="out_proj",
    )(c_prompt, c_samp, y_prompt, y_samp, w_out, h, g)


def _ple_kernel(h_ref, pa_ref, pb_ref, gpre_ref, wg_ref, wp_ref, gpost_ref, oa_ref, ob_ref, emb_scr):
    i = pl.program_id(0)

    def embed(rows, src, srows):
        emb_scr[rows, :] = _dot(src[srows, :].astype(BF16), wp_ref[...])
    _on_tile_rows(i, pa_ref, pb_ref, embed)

    h = h_ref[...]
    gate = jax.nn.sigmoid(_dot(_rms(h, gpre_ref[...]).astype(BF16), wg_ref[...]))
    oa_ref[...] = h + _rms(gate * emb_scr[...], gpost_ref[...])

    @pl.when(i == pl.num_programs(0) - 1)
    def _():
        nb = ob_ref.shape[0]
        ob_ref[...] = oa_ref[TM - nb:TM, :]


def _ple(h, p_prompt, p_samp, gpre, wg, wp, gpost):
    m = h.shape[0]
    n_prompt, n_samp = p_prompt.shape[0], p_samp.shape[0]
    row = pl.BlockSpec((TM, D_MODEL), lambda i: (i, 0))
    vec = pl.BlockSpec((1, D_MODEL), lambda i: (0, 0))
    return pl.pallas_call(
        _ple_kernel, grid=(m // TM,),
        in_specs=[row, pl.BlockSpec((TM, PLE_DIM), lambda i: (i, 0)),
                  pl.BlockSpec((n_samp, PLE_DIM), lambda i: (0, 0)), vec,
                  pl.BlockSpec((D_MODEL, D_MODEL), lambda i: (0, 0)),
                  pl.BlockSpec((PLE_DIM, D_MODEL), lambda i: (0, 0)), vec],
        out_specs=[row, pl.BlockSpec((n_samp, D_MODEL), lambda i: (0, 0))],
        out_shape=[jax.ShapeDtypeStruct((n_prompt, D_MODEL), F32),
                   jax.ShapeDtypeStruct((n_samp, D_MODEL), F32)],
        scratch_shapes=[pltpu.VMEM((TM, D_MODEL), F32)],
        compiler_params=_params(("arbitrary",)), name="ple")(h, p_prompt, p_samp, gpre, wg, wp, gpost)


def _layer(x_prompt, x_samp, p_prompt, p_samp, n_batch, seq, cst, xst, h0, lw):
    (norm_ffn1_pre, w_ffn1_gate, w_ffn1_up, w_ffn1_down, norm_ffn1_post,
     norm_mix_pre, w_in, conv_mod_w, conv_mod_b, conv_mod_ln_g, conv_mod_ln_b,
     ssm_conv_w, ssm_conv_b, dt_bias, a_log, d_skip, ssm_norm_g, w_out, norm_mix_post,
     norm_ffn2_pre, w_ffn2_gate, w_ffn2_up, w_ffn2_down, norm_ffn2_post,
     norm_ple_pre, w_ple_gate, w_ple_proj, norm_ple_post) = lw
    n_prompt, n_samp = x_prompt.shape[0], x_samp.shape[0]
    m = n_prompt + n_samp
    _check_split(n_prompt, n_samp)
    row2 = lambda t: t.reshape(1, -1)

    casts_ffn1 = ((jnp.swapaxes(w_in, 0, 1), LANE, True),)
    casts_conv = ((w_out, 128, False), (w_ffn2_gate, 64, False), (w_ffn2_up, 64, False),
                  (w_ffn2_down, 176, False), (w_ple_gate, 64, False), (w_ple_proj, 16, False))
    ffn1 = (row2(norm_ffn1_pre), row2(norm_ffn1_post), row2(norm_mix_pre))
    h1, u, wg1, wu1, wd1 = _ffn(
        x_prompt, ffn1[0], w_ffn1_gate, w_ffn1_up, w_ffn1_down, ffn1[1], gnext=ffn1[2],
        m=m, tiles=(0, 1), emit_w16=True, tf=TF_FIRST, tm=FIRST_TILES * TM, name="ffn_first")
    h1, u, w_in16 = _ffn(
        x_prompt, ffn1[0], wg1, wu1, wd1, ffn1[1], gnext=ffn1[2], x_tail=x_samp, casts=casts_ffn1,
        m=m, tiles=(FIRST_TILES, m // TM - FIRST_TILES), carry=(h1, u), name="ffn_rest")

    d_proj = w_in.shape[1]
    w_dt = jnp.pad(w_in16[:, d_proj - N_HEADS:], ((0, 0), (0, HEAD_PAD - N_HEADS)))
    pad_h = lambda t: jnp.pad(t.astype(F32), (0, HEAD_PAD - N_HEADS)).reshape(1, HEAD_PAD)
    mix, dt = _in_proj(u, w_in16, w_dt, pad_h(dt_bias))

    a = -jnp.exp(a_log.astype(F32))
    a_pad = pad_h(a)
    d_exp = row2(jnp.repeat(d_skip.astype(F32), HEAD_DIM))
    cw, cb = ssm_conv_w, row2(ssm_conv_b)
    mw, mb, lg, lb = conv_mod_w, row2(conv_mod_b), row2(conv_mod_ln_g), row2(conv_mod_ln_b)
    ng = row2(ssm_norm_g)

    c_prompt, glu_tail, w_out16, wg2, wu2, wd2, wpg, wpp = _prompt_conv(
        mix, n_batch, seq, mw, mb, lg, lb, casts=casts_conv)
    c_samp, new_cst_samp, new_xst_samp, xdt, bc, dec, skip = _sample_prep(
        mix, dt, n_prompt, n_samp, cst, xst, mw, mb, lg, lb, cw, cb, a_pad, d_exp)
    y_mix_prompt, hfin_prompt, xbc_tail, hn, y_mix_samp = _ssd(
        mix, dt, n_batch, seq, cw, cb, a_pad, d_exp, ng,
        dec, h0.reshape(n_samp, D_SSM, D_STATE), xdt, bc, skip)

    h2 = _out_proj(c_prompt, c_samp, y_mix_prompt, y_mix_samp, w_out16, h1, row2(norm_mix_post))
    (h3,) = _ffn(h2, row2(norm_ffn2_pre), wg2, wu2, wd2, row2(norm_ffn2_post))
    y_prompt, y_samp = _ple(h3, p_prompt, p_samp, row2(norm_ple_pre), wpg, wpp, row2(norm_ple_post))

    new_cst_prompt = glu_tail[:, CARRY - (CONV_W - 1):]
    new_xst_prompt = xbc_tail[:, XCARRY - (SSM_CONV_W - 1):]
    new_h_prompt = hfin_prompt.reshape(n_batch, N_HEADS, HEAD_DIM, D_STATE)
    new_h_samp = hn.reshape(n_samp, N_HEADS, HEAD_DIM, D_STATE)
    return (y_prompt, y_samp, new_cst_prompt, new_xst_prompt, new_h_prompt,
            new_cst_samp, new_xst_samp, new_h_samp)


def kernel(x_prompt, x_sample, state_conv_mod, state_ssm_conv, state_ssm, p_prompt, p_sample,
           norm_ffn1_pre, w_ffn1_gate, w_ffn1_up, w_ffn1_down, norm_ffn1_post,
           norm_mix_pre, w_in, conv_mod_w, conv_mod_b, conv_mod_ln_g, conv_mod_ln_b,
           ssm_conv_w, ssm_conv_b, dt_bias, a_log, d_skip, ssm_norm_g, w_out, norm_mix_post,
           norm_ffn2_pre, w_ffn2_gate, w_ffn2_up, w_ffn2_down, norm_ffn2_post,
           norm_ple_pre, w_ple_gate, w_ple_proj, norm_ple_post):
    weights = (norm_ffn1_pre, w_ffn1_gate, w_ffn1_up, w_ffn1_down, norm_ffn1_post,
               norm_mix_pre, w_in, conv_mod_w, conv_mod_b, conv_mod_ln_g, conv_mod_ln_b,
               ssm_conv_w, ssm_conv_b, dt_bias, a_log, d_skip, ssm_norm_g, w_out, norm_mix_post,
               norm_ffn2_pre, w_ffn2_gate, w_ffn2_up, w_ffn2_down, norm_ffn2_post,
               norm_ple_pre, w_ple_gate, w_ple_proj, norm_ple_post)
    n_batch, seq, _ = x_prompt.shape
    n_samp = x_sample.shape[0]
    n_prompt = n_batch * seq
    depth = norm_ffn1_pre.shape[0]
    xp = x_prompt.reshape(n_prompt, D_MODEL)
    xs = x_sample.reshape(n_samp, D_MODEL)
    outs = [[] for _ in range(6)]
    for i in range(depth):
        res = _layer(xp, xs, p_prompt[i].reshape(n_prompt, PLE_DIM), p_sample[i].reshape(n_samp, PLE_DIM),
                     n_batch, seq, state_conv_mod[i], state_ssm_conv[i], state_ssm[i],
                     tuple(w[i] for w in weights))
        xp, xs = res[0], res[1]
        for lst, r in zip(outs, res[2:]):
            lst.append(r)
    return ((xp.reshape(n_batch, seq, D_MODEL), xs.reshape(n_samp, 1, D_MODEL))
            + tuple(jnp.stack(lst, axis=0) for lst in outs))
```

```python
import functools

import jax
import jax.numpy as jnp
from jax import lax
from jax.experimental import pallas as pl
from jax.experimental.pallas import tpu as pltpu

F32 = jnp.float32
BF16 = jnp.bfloat16

D_MODEL = 2048
D_FF = 5632
D_CONV = 1024
D_SSM = 3072
N_HEADS = 48
HEAD_DIM = 64
N_GROUPS = 8
HEADS_PER_GROUP = 6
GROUP_W = HEADS_PER_GROUP * HEAD_DIM
D_STATE = 128
D_XBC = D_SSM + 2 * N_GROUPS * D_STATE
MIX_W = D_XBC + D_CONV + D_SSM
GLU_COL = D_XBC // D_CONV
Z_COL = (D_XBC + D_CONV) // D_SSM
assert GLU_COL * D_CONV == D_XBC and Z_COL * D_SSM == D_XBC + D_CONV
CONV_W = 31
SSM_CONV_W = 4
CHUNK = 128
PLE_DIM = 256
EPS = 1e-6
NEG_BIG = -1e30

LANE = 128
SUBLANE = 8
BF16_ROWS = 16
HEAD_PAD = LANE
HEAD_SHIFT = HEAD_DIM.bit_length() - 1
assert 1 << HEAD_SHIFT == HEAD_DIM and 2 * HEAD_DIM == LANE

TM = 640
TF = 512
TF_FIRST = 256
FIRST_TILES = 2
DONE_ROWS = 64
TMP = 1664
TN = 1024
TL = 256
CARRY = 32
CONV_ROWS = 64
NORM_ROWS = 16
XCARRY = SUBLANE
SAMPLE_BLOCK = 32
VMEM_LIMIT = 56 * 1024 * 1024
VMEM_LIMIT_MAX = 58 * 1024 * 1024


def _params(dims, vmem=VMEM_LIMIT):
    return pltpu.CompilerParams(dimension_semantics=dims, vmem_limit_bytes=vmem)


def _rms(x, g):
    return x * lax.rsqrt(jnp.mean(x * x, axis=-1, keepdims=True) + EPS) * g


def _silu(x):
    return x * jax.nn.sigmoid(x)


def _dot(a, b):
    return jnp.dot(a, b, preferred_element_type=F32)


def _split3(x):
    hi = x.astype(BF16)
    r = x - hi.astype(F32)
    mid = r.astype(BF16)
    lo = (r - mid.astype(F32)).astype(BF16)
    return hi, mid, lo


def _dot3_rhs(a3_bf16, x):
    return _dot(a3_bf16, jnp.concatenate(_split3(x), axis=0))


def _dot3_lhs(x, b3_bf16):
    return _dot(jnp.concatenate(_split3(x), axis=1), b3_bf16)


def _head_expand_matrix():
    head = lax.broadcasted_iota(jnp.int32, (3 * HEAD_PAD, D_SSM), 0) & (HEAD_PAD - 1)
    chan = lax.broadcasted_iota(jnp.int32, (3 * HEAD_PAD, D_SSM), 1)
    return jnp.where((chan >> HEAD_SHIFT) == head, 1.0, 0.0).astype(BF16)


def _on_tile_rows(i, a_ref, b_ref, fn):
    tm = a_ref.shape[0]
    if b_ref is None:
        fn(slice(0, tm), a_ref, slice(0, tm))
        return
    nb = b_ref.shape[0]
    na = tm - nb
    last = pl.num_programs(0) - 1

    @pl.when(i != last)
    def _():
        fn(slice(0, tm), a_ref, slice(0, tm))

    @pl.when(i == last)
    def _():
        fn(slice(0, na), a_ref, slice(0, na))
        fn(slice(na, tm), b_ref, slice(0, nb))


def _cast_specs(casts, steps, step_of):
    in_specs, out_specs, out_shapes = [], [], []
    for w, r, transpose in casts:
        nblk = pl.cdiv(w.shape[0], r)
        assert r % BF16_ROWS == 0 and nblk <= steps and (w.shape[0] % r == 0 or transpose)
        slab = lambda *idx, nblk=nblk: jnp.minimum(step_of(*idx), nblk - 1)
        in_specs.append(pl.BlockSpec((r, w.shape[1]), lambda *idx, slab=slab: (slab(*idx), 0)))
        if transpose:
            assert r % LANE == 0
            out_specs.append(pl.BlockSpec((w.shape[1], r), lambda *idx, slab=slab: (0, slab(*idx))))
            out_shapes.append(jax.ShapeDtypeStruct(w.shape[::-1], BF16))
        else:
            out_specs.append(pl.BlockSpec((r, w.shape[1]), lambda *idx, slab=slab: (slab(*idx), 0)))
            out_shapes.append(jax.ShapeDtypeStruct(w.shape, BF16))
    return in_specs, out_specs, out_shapes


def _run_casts(cast_in, cast_out, cast_t):
    for ci, co, transpose in zip(cast_in, cast_out, cast_t):
        co[...] = (ci[...].T if transpose else ci[...]).astype(BF16)


def _check_split(n_prompt, n_samp):
    assert (n_prompt + n_samp) % TM == 0 and n_samp < TM and n_samp % BF16_ROWS == 0


def _ffn_kernel(*refs, split, n_next, cast_t, emit_w16, n_done):
    n_cast = len(cast_t)
    refs = list(refs)
    xa_ref = refs.pop(0)
    xb_ref = refs.pop(0) if split else None
    gpre_ref, wg_ref, wu_ref, wd_ref, gpost_ref = refs[:5]
    refs = refs[5:]
    gnext_ref = refs.pop(0) if n_next else None
    cast_in, refs = refs[:n_cast], refs[n_cast:]
    n_out = 2 if n_next else 1
    done_refs, refs = (refs[:n_out], refs[n_out:]) if n_done else ((), refs)
    o_ref = refs.pop(0)
    unext_ref = refs.pop(0) if n_next else None
    cast_out, refs = refs[:n_cast], refs[n_cast:]
    w16_refs, refs = (refs[:3], refs[3:]) if emit_w16 else ((), refs)
    (u_scr,) = refs
    i = pl.program_id(0)
    j = pl.program_id(1)

    _run_casts(cast_in, cast_out, cast_t)

    def tile_step():
        @pl.when(j == 0)
        def _():
            def pre(rows, src, srows):
                u_scr[rows, :] = _rms(src[srows, :], gpre_ref[...]).astype(BF16)
            _on_tile_rows(i, xa_ref, xb_ref, pre)
            o_ref[...] = jnp.zeros_like(o_ref)

        wg, wu, wd = wg_ref[...], wu_ref[...], wd_ref[...]
        if emit_w16:
            wg, wu, wd = wg.astype(BF16), wu.astype(BF16), wd.astype(BF16)
            for ref, w in zip(w16_refs, (wg, wu, wd)):
                ref[...] = w
        u = u_scr[...]
        act = (_silu(_dot(u, wg)) * _dot(u, wu)).astype(BF16)
        o_ref[...] += _dot(act, wd)

        @pl.when(j == pl.num_programs(1) - 1)
        def _():
            def post(rows, src, srows):
                step = rows.stop - rows.start if n_next else NORM_ROWS
                for r in range(0, rows.stop - rows.start, step):
                    dst = slice(rows.start + r, rows.start + r + step)
                    h = (src[srows.start + r:srows.start + r + step, :]
                         + _rms(o_ref[dst, :], gpost_ref[...]))
                    o_ref[dst, :] = h
                    if n_next:
                        unext_ref[dst, :] = _rms(h, gnext_ref[...]).astype(BF16)
            _on_tile_rows(i, xa_ref, xb_ref, post)

    if not n_done:
        tile_step()
        return

    @pl.when(jnp.logical_and(i < n_done, j < o_ref.shape[0] // DONE_ROWS))
    def _():
        rows = pl.ds(pl.multiple_of(j * DONE_ROWS, DONE_ROWS), DONE_ROWS)
        for ref, done in zip((o_ref, unext_ref), done_refs):
            ref[rows, :] = done[...]

    pl.when(i >= n_done)(tile_step)


def _ffn(x, gpre, wg, wu, wd, gpost, gnext=None, x_tail=None, casts=(), rows=None, done=(),
         emit_w16=False, tf=TF, tm=TM, name="ffn"):
    m = rows if rows is not None else x.shape[0] + (0 if x_tail is None else x_tail.shape[0])
    n_done = done[0].shape[0] // tm if done else 0
    assert m % tm == 0 and all(d.shape[0] == n_done * tm for d in done)
    grid = (m // tm, D_FF // tf)
    steps = grid[0] * grid[1]
    row = pl.BlockSpec((tm, D_MODEL), lambda i, j: (i, 0))
    vec = pl.BlockSpec((1, D_MODEL), lambda i, j: (0, 0))
    slab = (lambda i, j: jnp.where(i < n_done, 0, j)) if n_done else (lambda i, j: j)
    wcol = pl.BlockSpec((D_MODEL, tf), lambda i, j: (0, slab(i, j)))
    wrow = pl.BlockSpec((tf, D_MODEL), lambda i, j: (slab(i, j), 0))
    in_specs, args = [row], [x]
    if x_tail is not None:
        in_specs.append(pl.BlockSpec(x_tail.shape, lambda i, j: (0, 0)))
        args.append(x_tail)
    in_specs += [vec, wcol, wcol, wrow, vec]
    args += [gpre, wg, wu, wd, 0.5 * gpost]
    out_shape = [jax.ShapeDtypeStruct((m, D_MODEL), F32)]
    out_specs = [row]
    if gnext is not None:
        in_specs.append(vec)
        args.append(gnext)
        out_shape.append(jax.ShapeDtypeStruct((m, D_MODEL), BF16))
        out_specs.append(row)
    assert len(done) in (0, len(out_shape))
    cast_in_specs, cast_out_specs, cast_shapes = _cast_specs(casts, steps, lambda i, j: i * grid[1] + j)
    in_specs += cast_in_specs
    args += [w for w, _, _ in casts]
    out_specs += cast_out_specs
    out_shape += cast_shapes
    pieces = tm // DONE_ROWS
    assert not done or (pieces * DONE_ROWS == tm and pieces <= grid[1])
    for d in done:
        in_specs.append(pl.BlockSpec(
            (DONE_ROWS, D_MODEL),
            lambda i, j: (jnp.minimum(i, n_done - 1) * pieces + jnp.minimum(j, pieces - 1), 0)))
        args.append(d)
    if emit_w16:
        out_specs += [wcol, wcol, wrow]
        out_shape += [jax.ShapeDtypeStruct(w.shape, BF16) for w in (wg, wu, wd)]
    return pl.pallas_call(
        functools.partial(_ffn_kernel, split=x_tail is not None, n_next=gnext is not None,
                          cast_t=tuple(t for _, _, t in casts), emit_w16=emit_w16, n_done=n_done),
        grid=grid, in_specs=in_specs, out_specs=out_specs, out_shape=out_shape,
        scratch_shapes=[pltpu.VMEM((tm, D_MODEL), BF16)],
        compiler_params=_params(("arbitrary", "arbitrary")), name=name,
    )(*args)


N_GLU_STEPS = D_CONV // TN
N_Z_STEPS = D_SSM // TN
N_XBC_STEPS = D_XBC // TN


def _inproj_kernel(u_ref, w_ref, wb_ref, wdt_ref, dtb_ref, o_ref, dt_ref):
    j = pl.program_id(1)

    @pl.when(j < N_GLU_STEPS)
    def _():
        u = u_ref[...]
        o_ref[...] = _dot(u, w_ref[...]) * jax.nn.sigmoid(_dot(u, wb_ref[...]))

    @pl.when(j >= N_GLU_STEPS)
    def _():
        o_ref[...] = _dot(u_ref[...], w_ref[...])

    @pl.when(j == pl.num_programs(1) - 1)
    def _():
        x = _dot(u_ref[...], wdt_ref[...]) + dtb_ref[...]
        dt_ref[...] = jnp.maximum(x, 0.0) + jnp.log1p(jnp.exp(-jnp.abs(x)))


def _in_proj(u, w_in, w_dt, dt_bias):
    m = u.shape[0]
    g, nz, nx = N_GLU_STEPS, N_Z_STEPS, N_XBC_STEPS
    assert m % TMP == 0 and g == 1

    def out_slab(j):
        return jnp.where(j < g + nz, j + nx, j - g - nz)

    return pl.pallas_call(
        _inproj_kernel, grid=(m // TMP, g + nz + nx),
        in_specs=[pl.BlockSpec((TMP, D_MODEL), lambda i, j: (i, 0)),
                  pl.BlockSpec((D_MODEL, TN), lambda i, j: (0, jnp.where(j < g, j, j + g))),
                  pl.BlockSpec((D_MODEL, TN), lambda i, j: (0, g)),
                  pl.BlockSpec((D_MODEL, HEAD_PAD), lambda i, j: (0, 0)),
                  pl.BlockSpec((1, HEAD_PAD), lambda i, j: (0, 0))],
        out_specs=[pl.BlockSpec((TMP, TN), lambda i, j: (i, out_slab(j))),
                   pl.BlockSpec((TMP, HEAD_PAD), lambda i, j: (i, 0))],
        out_shape=[jax.ShapeDtypeStruct((m, MIX_W), F32), jax.ShapeDtypeStruct((m, HEAD_PAD), F32)],
        compiler_params=_params(("arbitrary", "arbitrary"), VMEM_LIMIT_MAX), name="in_proj",
    )(u, w_in, w_in, w_dt, dt_bias)


def _ln_swish(y, g, b):
    mu = jnp.mean(y, axis=-1, keepdims=True)
    yc = y - mu
    yn = yc * lax.rsqrt(jnp.mean(yc * yc, axis=-1, keepdims=True) + EPS) * g + b
    return _silu(yn)


def _pconv_kernel(v_ref, w_ref, b_ref, lg_ref, lb_ref, *rest, cast_t):
    n_cast = len(cast_t)
    cast_in, (o_ref, tail_ref), rest = rest[:n_cast], rest[n_cast:n_cast + 2], rest[n_cast + 2:]
    cast_out, (xpad_scr, conv_scr) = rest[:n_cast], rest[n_cast:]
    t = pl.program_id(1)
    _run_casts(cast_in, cast_out, cast_t)

    @pl.when(t == 0)
    def _():
        xpad_scr[0:CARRY, :] = jnp.zeros((CARRY, D_CONV), F32)

    xpad_scr[CARRY:CARRY + TL, :] = v_ref[...]
    first = CARRY - (CONV_W - 1)
    hb = CONV_ROWS
    n_rows = CARRY + TL
    for cb in range(D_CONV // LANE):
        lanes = slice(cb * LANE, (cb + 1) * LANE)
        xfull = xpad_scr[:, lanes]
        conv_scr[:, lanes] = jnp.broadcast_to(b_ref[:, lanes], (TL, LANE))
        for phase in range(SUBLANE):
            xs = pltpu.roll(xfull, n_rows - phase, axis=0) if phase else xfull
            taps = [k for k in range(CONV_W) if (first + k) % SUBLANE == phase]
            for base in range(0, TL, hb):
                acc = conv_scr[base:base + hb, lanes]
                for k in taps:
                    off = SUBLANE * ((first + k) // SUBLANE)
                    acc = acc + w_ref[k:k + 1, lanes] * xs[base + off:base + off + hb]
                conv_scr[base:base + hb, lanes] = acc
    xpad_scr[0:CARRY, :] = xpad_scr[TL:TL + CARRY, :]
    for r in range(0, TL, NORM_ROWS):
        rows = slice(r, r + NORM_ROWS)
        o_ref[rows, :] = _ln_swish(conv_scr[rows, :], lg_ref[...], lb_ref[...]).astype(BF16)

    @pl.when(t == pl.num_programs(1) - 1)
    def _():
        tail_ref[0] = xpad_scr[0:CARRY, :]


def _prompt_conv(mix, n_batch, seq, w, b, lg, lb, casts=()):
    vec = pl.BlockSpec((1, D_CONV), lambda bi, t: (0, 0))
    steps = seq // TL
    cast_in_specs, cast_out_specs, cast_shapes = _cast_specs(
        casts, n_batch * steps, lambda bi, t: bi * steps + t)
    return pl.pallas_call(
        functools.partial(_pconv_kernel, cast_t=tuple(t for _, _, t in casts)),
        grid=(n_batch, steps),
        in_specs=[pl.BlockSpec((TL, D_CONV), lambda bi, t: (bi * steps + t, GLU_COL)),
                  pl.BlockSpec((CONV_W, D_CONV), lambda bi, t: (0, 0)), vec, vec, vec] + cast_in_specs,
        out_specs=[pl.BlockSpec((TL, D_CONV), lambda bi, t: (bi * steps + t, 0)),
                   pl.BlockSpec((1, CARRY, D_CONV), lambda bi, t: (bi, 0, 0))] + cast_out_specs,
        out_shape=[jax.ShapeDtypeStruct((n_batch * seq, D_CONV), BF16),
                   jax.ShapeDtypeStruct((n_batch, CARRY, D_CONV), F32)] + cast_shapes,
        scratch_shapes=[pltpu.VMEM((CARRY + TL, D_CONV), F32), pltpu.VMEM((TL, D_CONV), F32)],
        compiler_params=_params(("arbitrary", "arbitrary")), name="prompt_conv",
    )(mix, w, b, lg, lb, *[c for c, _, _ in casts])


def _gated_norm(y, z, g):
    yg = y * _silu(z)
    return yg * lax.rsqrt(jnp.mean(yg * yg, axis=-1, keepdims=True) + EPS) * g


def _sample_state_update(step, dec_ref, h0_ref, bc_ref, hn_ref, yt_scr, xdt_t_scr):
    per_step = h0_ref.shape[0]
    nb = bc_ref.shape[0]
    seq_i = lax.broadcasted_iota(jnp.int32, (nb, D_STATE), 0)
    seqs = [step * per_step + bb for bb in range(per_step)]
    brows = [bc_ref[pl.ds(b, 1), :] for b in seqs]

    def one_hot_rows(g):
        cols = slice(g * D_STATE, (g + 1) * D_STATE)
        return jnp.concatenate([jnp.where(seq_i == b, brow[:, cols], 0.0).astype(BF16)
                                for b, brow in zip(seqs, brows)], axis=1)

    for g in range(N_GROUPS):
        rows = slice(g * GROUP_W, (g + 1) * GROUP_W)
        s_new = _dot(xdt_t_scr[rows, :], one_hot_rows(g))
        h_all = []
        for bb, b in enumerate(seqs):
            parts = []
            for r in range(HEADS_PER_GROUP):
                h = g * HEADS_PER_GROUP + r
                hr = slice(h * HEAD_DIM, (h + 1) * HEAD_DIM)
                parts.append(h0_ref[bb, hr, :] * dec_ref[b * N_HEADS + h]
                             + s_new[r * HEAD_DIM:(r + 1) * HEAD_DIM, bb * D_STATE:(bb + 1) * D_STATE])
            h_new = jnp.concatenate(parts, axis=0)
            hn_ref[bb, rows, :] = h_new
            h_all.append(h_new.astype(BF16))
        yt_scr[rows, :] += lax.dot_general(jnp.concatenate(h_all, axis=1), one_hot_rows(N_GROUPS + g),
                                           (((1,), (1,)), ((), ())), preferred_element_type=F32)


def _pssd_kernel(xbc_ref, z_ref, dt_ref, cw_ref, cb_ref, a_ref, expand_ref, dexp_ref, ng_ref,
                 dec_ref, h0_ref, xdt_ref, bc_ref, skip_ref, zs_ref,
                 y_ref, hfin_ref, tail_ref, hn_ref, ys_ref,
                 state_scr, xpad_scr, xc_scr, yt_scr, xdt_t_scr):
    c = pl.program_id(1)
    q = CHUNK
    step = pl.program_id(0) * pl.num_programs(1) + c

    @pl.when(step == 0)
    def _():
        yt_scr[...] = jnp.zeros_like(yt_scr)
        xdt_t_scr[...] = xdt_ref[...].T.astype(BF16)

    @pl.when(c == 0)
    def _():
        state_scr[...] = jnp.zeros_like(state_scr)
        xpad_scr[0:XCARRY, :] = jnp.zeros((XCARRY, D_XBC), F32)

    xpad_scr[XCARRY:XCARRY + q, :] = xbc_ref[...]
    for cb in range(D_XBC // 512):
        lanes = slice(cb * 512, (cb + 1) * 512)
        acc = jnp.broadcast_to(cb_ref[:, lanes], (q, 512))
        for j in range(SSM_CONV_W):
            k = SSM_CONV_W - 1 - j
            acc = acc + cw_ref[k:k + 1, lanes] * xpad_scr[XCARRY - j:XCARRY - j + q, lanes]
        xc_scr[:, lanes] = _silu(acc)
    xpad_scr[0:XCARRY, :] = xpad_scr[q:q + XCARRY, :]

    row_i = lax.broadcasted_iota(jnp.int32, (q, q), 0)
    col_i = lax.broadcasted_iota(jnp.int32, (q, q), 1)
    tril = row_i >= col_i
    tri = jnp.where(tril, 1.0, 0.0).astype(BF16)
    expand = expand_ref[...]

    dt = dt_ref[...]
    tri3 = jnp.concatenate([tri, tri, tri], axis=1)
    a_cs = _dot3_rhs(tri3, dt * a_ref[...])
    a_cs_t = a_cs.T
    dt_exp = _dot3_lhs(dt, expand)
    acs_exp = _dot3_lhs(a_cs, expand)
    last = acs_exp[q - 1:q, :]
    lane_lo = lax.broadcasted_iota(jnp.int32, (q, LANE), 1) < HEAD_DIM

    for g in range(N_GROUPS):
        ch = slice(g * GROUP_W, (g + 1) * GROUP_W)
        xs = xc_scr[:, ch]
        bg = xc_scr[:, D_SSM + g * D_STATE:D_SSM + (g + 1) * D_STATE]
        cg = xc_scr[:, D_SSM + (N_GROUPS + g) * D_STATE:D_SSM + (N_GROUPS + g + 1) * D_STATE]
        bg16 = bg.astype(BF16)
        cg16 = cg.astype(BF16)
        xdt = xs * dt_exp[:, ch]
        acs_g = acs_exp[:, ch]
        cb = lax.dot_general(cg16, bg16, (((1,), (1,)), ((), ())), preferred_element_type=F32)
        st = state_scr[:, ch]
        y = _dot(cg16, st.astype(BF16)) * jnp.exp(acs_g)
        pieces = []
        for pr in range(HEADS_PER_GROUP // 2):
            xpair = xdt[:, pr * LANE:(pr + 1) * LANE].astype(BF16)
            both = []
            for half in range(2):
                h = g * HEADS_PER_GROUP + 2 * pr + half
                seg = a_cs[:, h:h + 1] - a_cs_t[h:h + 1, :]
                decay = jnp.exp(jnp.where(tril, seg, NEG_BIG))
                both.append(_dot((cb * decay).astype(BF16), xpair))
            pieces.append(jnp.where(lane_lo, both[0], both[1]))
        y = y + jnp.concatenate(pieces, axis=1) + dexp_ref[:, ch] * xs
        xdec = (xdt * jnp.exp(last[:, ch] - acs_g)).astype(BF16)
        s_new = lax.dot_general(bg16, xdec, (((0,), (0,)), ((), ())), preferred_element_type=F32)
        state_scr[:, ch] = st * jnp.exp(last[:, ch]) + s_new
        y_ref[:, ch] = _gated_norm(y, z_ref[:, ch], ng_ref[:, ch]).astype(BF16)

    _sample_state_update(step, dec_ref, h0_ref, bc_ref, hn_ref, yt_scr, xdt_t_scr)

    @pl.when(c == pl.num_programs(1) - 1)
    def _():
        hfin_ref[0] = state_scr[...].T
        tail_ref[0] = xpad_scr[0:XCARRY, :]

    @pl.when(step == pl.num_programs(0) * pl.num_programs(1) - 1)
    def _():
        y = yt_scr[...].T + skip_ref[...]
        for g in range(N_GROUPS):
            ch = slice(g * GROUP_W, (g + 1) * GROUP_W)
            ys_ref[:, ch] = _gated_norm(y[:, ch], zs_ref[:, ch], ng_ref[:, ch]).astype(BF16)


def _ssd(mix, dt, n_batch, seq, cw, cb, a_pad, d_exp, ng, dec, h0, xdt, bc, skip):
    nc = seq // CHUNK
    n_prompt = n_batch * seq
    n_samp = h0.shape[0]
    per_step = n_samp // (n_batch * nc)
    assert per_step * n_batch * nc == n_samp and n_prompt % n_samp == 0

    def rows(w, col=0):
        return pl.BlockSpec((CHUNK, w), lambda bi, c: (bi * nc + c, col))

    def vec(*s):
        return pl.BlockSpec(s, lambda bi, c: (0,) * len(s))

    states = pl.BlockSpec((per_step, D_SSM, D_STATE), lambda bi, c: (bi * nc + c, 0, 0))
    return pl.pallas_call(
        _pssd_kernel, grid=(n_batch, nc),
        in_specs=[rows(D_XBC), rows(D_SSM, Z_COL), rows(HEAD_PAD), vec(SSM_CONV_W, D_XBC),
                  vec(1, D_XBC), vec(1, HEAD_PAD), vec(3 * HEAD_PAD, D_SSM), vec(1, D_SSM),
                  vec(1, D_SSM), pl.BlockSpec(memory_space=pltpu.SMEM), states, vec(n_samp, D_SSM),
                  vec(n_samp, D_XBC - D_SSM), vec(n_samp, D_SSM),
                  pl.BlockSpec((n_samp, D_SSM), lambda bi, c: (n_prompt // n_samp, Z_COL))],
        out_specs=[rows(D_SSM), pl.BlockSpec((1, D_SSM, D_STATE), lambda bi, c: (bi, 0, 0)),
                   pl.BlockSpec((1, XCARRY, D_XBC), lambda bi, c: (bi, 0, 0)),
                   states, vec(n_samp, D_SSM)],
        out_shape=[jax.ShapeDtypeStruct((n_prompt, D_SSM), BF16),
                   jax.ShapeDtypeStruct((n_batch, D_SSM, D_STATE), F32),
                   jax.ShapeDtypeStruct((n_batch, XCARRY, D_XBC), F32),
                   jax.ShapeDtypeStruct((n_samp, D_SSM, D_STATE), F32),
                   jax.ShapeDtypeStruct((n_samp, D_SSM), BF16)],
        scratch_shapes=[pltpu.VMEM((D_STATE, D_SSM), F32), pltpu.VMEM((XCARRY + CHUNK, D_XBC), F32),
                        pltpu.VMEM((CHUNK, D_XBC), F32),
                        pltpu.VMEM((D_SSM, n_samp), F32), pltpu.VMEM((D_SSM, n_samp), BF16)],
        compiler_params=_params(("arbitrary", "arbitrary")), name="ssd",
    )(mix, mix, dt, cw, cb, a_pad, _head_expand_matrix(), d_exp, ng, dec, h0, xdt, bc, skip, mix)


def _sprep_kernel(v_ref, cst_ref, w_ref, b_ref, lg_ref, lb_ref,
                  xbc_ref, xst_ref, cw_ref, cb_ref, dt_ref, a_ref, dexp_ref,
                  co_ref, ncst_ref, nxst_ref, xdt_ref, bc_ref, dec_ref, skip_ref):
    nb = v_ref.shape[0]
    v = v_ref[...]
    acc = jnp.broadcast_to(b_ref[...], (nb, D_CONV)) + w_ref[CONV_W - 1:CONV_W, :] * v
    for k in range(CONV_W - 1):
        acc = acc + w_ref[k:k + 1, :] * cst_ref[k]
    co_ref[...] = _ln_swish(acc, lg_ref[...], lb_ref[...]).astype(BF16)
    for k in range(CONV_W - 2):
        ncst_ref[k] = cst_ref[k + 1]
    ncst_ref[CONV_W - 2] = v

    xn = xbc_ref[...]
    acc = jnp.broadcast_to(cb_ref[...], (nb, D_XBC)) + cw_ref[SSM_CONV_W - 1:SSM_CONV_W, :] * xn
    for k in range(SSM_CONV_W - 1):
        acc = acc + cw_ref[k:k + 1, :] * xst_ref[k]
    xc = _silu(acc)
    for k in range(SSM_CONV_W - 2):
        nxst_ref[k] = xst_ref[k + 1]
    nxst_ref[SSM_CONV_W - 2] = xn

    xs = xc[:, :D_SSM]
    bc_ref[...] = xc[:, D_SSM:]
    dt = dt_ref[...]
    dec_ref[...] = jnp.exp(dt * a_ref[...])
    dt_exp = _dot3_lhs(dt, _head_expand_matrix())
    xdt_ref[...] = xs * dt_exp
    skip_ref[...] = dexp_ref[...] * xs


def _sample_prep(mix, dt, n_prompt, n_samp, cst, xst, mw, mb, lg, lb, cw, cb, a_pad, d_exp):
    sb = SAMPLE_BLOCK
    off = n_prompt // sb
    srow = lambda w, col=0: pl.BlockSpec((sb, w), lambda i: (off + i, col))
    orow = lambda w: pl.BlockSpec((sb, w), lambda i: (i, 0))
    taps = lambda k, w: pl.BlockSpec((k, sb, w), lambda i: (0, i, 0))
    const = lambda *s: pl.BlockSpec(s, lambda i: (0,) * len(s))
    cst_t = jnp.swapaxes(cst, 0, 1)
    xst_t = jnp.swapaxes(xst, 0, 1)
    c_samp, ncst_t, nxst_t, xdt, bc, dec, skip = pl.pallas_call(
        _sprep_kernel, grid=(n_samp // sb,),
        in_specs=[srow(D_CONV, GLU_COL), taps(CONV_W - 1, D_CONV), const(CONV_W, D_CONV),
                  const(1, D_CONV), const(1, D_CONV), const(1, D_CONV),
                  srow(D_XBC), taps(SSM_CONV_W - 1, D_XBC), const(SSM_CONV_W, D_XBC),
                  const(1, D_XBC), srow(HEAD_PAD), const(1, HEAD_PAD), const(1, D_SSM)],
        out_specs=[orow(D_CONV), taps(CONV_W - 1, D_CONV), taps(SSM_CONV_W - 1, D_XBC),
                   orow(D_SSM), orow(D_XBC - D_SSM), orow(HEAD_PAD), orow(D_SSM)],
        out_shape=[jax.ShapeDtypeStruct((n_samp, D_CONV), BF16),
                   jax.ShapeDtypeStruct((CONV_W - 1, n_samp, D_CONV), F32),
                   jax.ShapeDtypeStruct((SSM_CONV_W - 1, n_samp, D_XBC), F32),
                   jax.ShapeDtypeStruct((n_samp, D_SSM), F32),
                   jax.ShapeDtypeStruct((n_samp, D_XBC - D_SSM), F32),
                   jax.ShapeDtypeStruct((n_samp, HEAD_PAD), F32),
                   jax.ShapeDtypeStruct((n_samp, D_SSM), F32)],
        compiler_params=_params(("arbitrary",)), name="sample_prep",
    )(mix, cst_t, mw, mb, lg, lb, mix, xst_t, cw, cb, dt, a_pad, d_exp)
    dec_flat = dec[:, :N_HEADS].reshape(-1)
    return (c_samp, jnp.swapaxes(ncst_t, 0, 1), jnp.swapaxes(nxst_t, 0, 1), xdt, bc, dec_flat, skip)


def _outproj_kernel(c_ref, cs_ref, y_ref, ys_ref, w_ref, h_ref, g_ref, o_ref):
    i = pl.program_id(0)
    last = pl.num_programs(0) - 1
    na = TM - cs_ref.shape[0]

    def run(c, y):
        mix = _dot(c, w_ref[0:D_CONV, :]) + _dot(y, w_ref[D_CONV:, :])
        o_ref[...] = h_ref[...] + _rms(mix, g_ref[...])

    @pl.when(i != last)
    def _():
        run(c_ref[...], y_ref[...])

    @pl.when(i == last)
    def _():
        run(jnp.concatenate([c_ref[0:na, :], cs_ref[...]], axis=0),
            jnp.concatenate([y_ref[0:na, :], ys_ref[...]], axis=0))


def _out_proj(c_prompt, c_samp, y_prompt, y_samp, w_out, h, g):
    m = h.shape[0]
    n_samp = c_samp.shape[0]
    row = pl.BlockSpec((TM, D_MODEL), lambda i: (i, 0))
    return pl.pallas_call(
        _outproj_kernel, grid=(m // TM,),
        in_specs=[pl.BlockSpec((TM, D_CONV), lambda i: (i, 0)),
                  pl.BlockSpec((n_samp, D_CONV), lambda i: (0, 0)),
                  pl.BlockSpec((TM, D_SSM), lambda i: (i, 0)),
                  pl.BlockSpec((n_samp, D_SSM), lambda i: (0, 0)),
                  pl.BlockSpec((D_CONV + D_SSM, D_MODEL), lambda i: (0, 0)),
                  row, pl.BlockSpec((1, D_MODEL), lambda i: (0, 0))],
        out_specs=row, out_shape=jax.ShapeDtypeStruct((m, D_MODEL), F32),
        compiler_params=_params(("arbitrary",)), name="out_proj",
    )(c_prompt, c_samp, y_prompt, y_samp, w_out, h, g)


def _ple_kernel(h_ref, pa_ref, pb_ref, gpre_ref, wg_ref, wp_ref, gpost_ref, oa_ref, ob_ref, emb_scr):
    i = pl.program_id(0)

    def embed(rows, src, srows):
        emb_scr[rows, :] = _dot(src[srows, :].astype(BF16), wp_ref[...])
    _on_tile_rows(i, pa_ref, pb_ref, embed)

    h = h_ref[...]
    gate = jax.nn.sigmoid(_dot(_rms(h, gpre_ref[...]).astype(BF16), wg_ref[...]))
    oa_ref[...] = h + _rms(gate * emb_scr[...], gpost_ref[...])

    @pl.when(i == pl.num_programs(0) - 1)
    def _():
        nb = ob_ref.shape[0]
        ob_ref[...] = oa_ref[TM - nb:TM, :]


def _ple(h, p_prompt, p_samp, gpre, wg, wp, gpost):
    m = h.shape[0]
    n_prompt, n_samp = p_prompt.shape[0], p_samp.shape[0]
    row = pl.BlockSpec((TM, D_MODEL), lambda i: (i, 0))
    vec = pl.BlockSpec((1, D_MODEL), lambda i: (0, 0))
    return pl.pallas_call(
        _ple_kernel, grid=(m // TM,),
        in_specs=[row, pl.BlockSpec((TM, PLE_DIM), lambda i: (i, 0)),
                  pl.BlockSpec((n_samp, PLE_DIM), lambda i: (0, 0)), vec,
                  pl.BlockSpec((D_MODEL, D_MODEL), lambda i: (0, 0)),
                  pl.BlockSpec((PLE_DIM, D_MODEL), lambda i: (0, 0)), vec],
        out_specs=[row, pl.BlockSpec((n_samp, D_MODEL), lambda i: (0, 0))],
        out_shape=[jax.ShapeDtypeStruct((n_prompt, D_MODEL), F32),
                   jax.ShapeDtypeStruct((n_samp, D_MODEL), F32)],
        scratch_shapes=[pltpu.VMEM((TM, D_MODEL), F32)],
        compiler_params=_params(("arbitrary",)), name="ple")(h, p_prompt, p_samp, gpre, wg, wp, gpost)


def _layer(x_prompt, x_samp, p_prompt, p_samp, n_batch, seq, cst, xst, h0, lw):
    (norm_ffn1_pre, w_ffn1_gate, w_ffn1_up, w_ffn1_down, norm_ffn1_post,
     norm_mix_pre, w_in, conv_mod_w, conv_mod_b, conv_mod_ln_g, conv_mod_ln_b,
     ssm_conv_w, ssm_conv_b, dt_bias, a_log, d_skip, ssm_norm_g, w_out, norm_mix_post,
     norm_ffn2_pre, w_ffn2_gate, w_ffn2_up, w_ffn2_down, norm_ffn2_post,
     norm_ple_pre, w_ple_gate, w_ple_proj, norm_ple_post) = lw
    n_prompt, n_samp = x_prompt.shape[0], x_samp.shape[0]
    m = n_prompt + n_samp
    _check_split(n_prompt, n_samp)
    row2 = lambda t: t.reshape(1, -1)

    casts_ffn1 = ((jnp.swapaxes(w_in, 0, 1), LANE, True),)
    casts_conv = ((w_out, 128, False), (w_ffn2_gate, 64, False), (w_ffn2_up, 64, False),
                  (w_ffn2_down, 176, False), (w_ple_gate, 64, False), (w_ple_proj, 16, False))
    ffn1 = (row2(norm_ffn1_pre), row2(norm_ffn1_post), row2(norm_mix_pre))
    h1_head, u_head, wg1, wu1, wd1 = _ffn(
        x_prompt, ffn1[0], w_ffn1_gate, w_ffn1_up, w_ffn1_down, ffn1[1], gnext=ffn1[2],
        rows=FIRST_TILES * TM, emit_w16=True, tf=TF_FIRST, tm=FIRST_TILES * TM, name="ffn_first")
    h1, u, w_in16 = _ffn(
        x_prompt, ffn1[0], wg1, wu1, wd1, ffn1[1], gnext=ffn1[2], x_tail=x_samp, casts=casts_ffn1,
        done=(h1_head, u_head), name="ffn_rest")

    d_proj = w_in.shape[1]
    w_dt = jnp.pad(w_in16[:, d_proj - N_HEADS:], ((0, 0), (0, HEAD_PAD - N_HEADS)))
    pad_h = lambda t: jnp.pad(t.astype(F32), (0, HEAD_PAD - N_HEADS)).reshape(1, HEAD_PAD)
    mix, dt = _in_proj(u, w_in16, w_dt, pad_h(dt_bias))

    a = -jnp.exp(a_log.astype(F32))
    a_pad = pad_h(a)
    d_exp = row2(jnp.repeat(d_skip.astype(F32), HEAD_DIM))
    cw, cb = ssm_conv_w, row2(ssm_conv_b)
    mw, mb, lg, lb = conv_mod_w, row2(conv_mod_b), row2(conv_mod_ln_g), row2(conv_mod_ln_b)
    ng = row2(ssm_norm_g)

    c_prompt, glu_tail, w_out16, wg2, wu2, wd2, wpg, wpp = _prompt_conv(
        mix, n_batch, seq, mw, mb, lg, lb, casts=casts_conv)
    c_samp, new_cst_samp, new_xst_samp, xdt, bc, dec, skip = _sample_prep(
        mix, dt, n_prompt, n_samp, cst, xst, mw, mb, lg, lb, cw, cb, a_pad, d_exp)
    y_mix_prompt, hfin_prompt, xbc_tail, hn, y_mix_samp = _ssd(
        mix, dt, n_batch, seq, cw, cb, a_pad, d_exp, ng,
        dec, h0.reshape(n_samp, D_SSM, D_STATE), xdt, bc, skip)

    h2 = _out_proj(c_prompt, c_samp, y_mix_prompt, y_mix_samp, w_out16, h1, row2(norm_mix_post))
    (h3,) = _ffn(h2, row2(norm_ffn2_pre), wg2, wu2, wd2, row2(norm_ffn2_post))
    y_prompt, y_samp = _ple(h3, p_prompt, p_samp, row2(norm_ple_pre), wpg, wpp, row2(norm_ple_post))

    new_cst_prompt = glu_tail[:, CARRY - (CONV_W - 1):]
    new_xst_prompt = xbc_tail[:, XCARRY - (SSM_CONV_W - 1):]
    new_h_prompt = hfin_prompt.reshape(n_batch, N_HEADS, HEAD_DIM, D_STATE)
    new_h_samp = hn.reshape(n_samp, N_HEADS, HEAD_DIM, D_STATE)
    return (y_prompt, y_samp, new_cst_prompt, new_xst_prompt, new_h_prompt,
            new_cst_samp, new_xst_samp, new_h_samp)


def kernel(x_prompt, x_sample, state_conv_mod, state_ssm_conv, state_ssm, p_prompt, p_sample,
           norm_ffn1_pre, w_ffn1_gate, w_ffn1_up, w_ffn1_down, norm_ffn1_post,
           norm_mix_pre, w_in, conv_mod_w, conv_mod_b, conv_mod_ln_g, conv_mod_ln_b,
           ssm_conv_w, ssm_conv_b, dt_bias, a_log, d_skip, ssm_norm_g, w_out, norm_mix_post,
           norm_ffn2_pre, w_ffn2_gate, w_ffn2_up, w_ffn2_down, norm_ffn2_post,
           norm_ple_pre, w_ple_gate, w_ple_proj, norm_ple_post):
    weights = (norm_ffn1_pre, w_ffn1_gate, w_ffn1_up, w_ffn1_down, norm_ffn1_post,
               norm_mix_pre, w_in, conv_mod_w, conv_mod_b, conv_mod_ln_g, conv_mod_ln_b,
               ssm_conv_w, ssm_conv_b, dt_bias, a_log, d_skip, ssm_norm_g, w_out, norm_mix_post,
               norm_ffn2_pre, w_ffn2_gate, w_ffn2_up, w_ffn2_down, norm_ffn2_post,
               norm_ple_pre, w_ple_gate, w_ple_proj, norm_ple_post)
    n_batch, seq, _ = x_prompt.shape
    n_samp = x_sample.shape[0]
    n_prompt = n_batch * seq
    depth = norm_ffn1_pre.shape[0]
    xp = x_prompt.reshape(n_prompt, D_MODEL)
    xs = x_sample.reshape(n_samp, D_MODEL)
    outs = [[] for _ in range(6)]
    for i in range(depth):
        res = _layer(xp, xs, p_prompt[i].reshape(n_prompt, PLE_DIM), p_sample[i].reshape(n_samp, PLE_DIM),
                     n_batch, seq, state_conv_mod[i], state_ssm_conv[i], state_ssm[i],
                     tuple(w[i] for w in weights))
        xp, xs = res[0], res[1]
        for lst, r in zip(outs, res[2:]):
            lst.append(r)
    return ((xp.reshape(n_batch, seq, D_MODEL), xs.reshape(n_samp, 1, D_MODEL))
            + tuple(jnp.stack(lst, axis=0) for lst in outs))
```

```python
import functools

import jax
import jax.numpy as jnp
from jax import lax
from jax.experimental import pallas as pl
from jax.experimental.pallas import tpu as pltpu

F32 = jnp.float32
BF16 = jnp.bfloat16

D_MODEL = 2048
D_FF = 5632
D_CONV = 1024
D_SSM = 3072
N_HEADS = 48
HEAD_DIM = 64
N_GROUPS = 8
HEADS_PER_GROUP = 6
GROUP_W = HEADS_PER_GROUP * HEAD_DIM
D_STATE = 128
D_XBC = D_SSM + 2 * N_GROUPS * D_STATE
MIX_W = D_XBC + D_CONV + D_SSM
GLU_COL = D_XBC // D_CONV
Z_COL = (D_XBC + D_CONV) // D_SSM
assert GLU_COL * D_CONV == D_XBC and Z_COL * D_SSM == D_XBC + D_CONV
CONV_W = 31
SSM_CONV_W = 4
CHUNK = 128
PLE_DIM = 256
EPS = 1e-6
NEG_BIG = -1e30

LANE = 128
SUBLANE = 8
BF16_ROWS = 16
HEAD_PAD = LANE
HEAD_SHIFT = HEAD_DIM.bit_length() - 1
assert 1 << HEAD_SHIFT == HEAD_DIM and 2 * HEAD_DIM == LANE

TM = 640
TF = 512
TF_FIRST = 256
FIRST_TILES = 2
DONE_ROWS = 64
TMP = 1664
TN = 1024
TL = 256
CARRY = 32
CONV_ROWS = 64
NORM_ROWS = 16
XCARRY = SUBLANE
SAMPLE_BLOCK = 32
VMEM_LIMIT = 56 * 1024 * 1024
VMEM_LIMIT_MAX = 58 * 1024 * 1024


def _params(dims, vmem=VMEM_LIMIT):
    return pltpu.CompilerParams(dimension_semantics=dims, vmem_limit_bytes=vmem)


def _rms(x, g):
    return x * lax.rsqrt(jnp.mean(x * x, axis=-1, keepdims=True) + EPS) * g


def _silu(x):
    return x * jax.nn.sigmoid(x)


def _dot(a, b):
    return jnp.dot(a, b, preferred_element_type=F32)


def _split3(x):
    hi = x.astype(BF16)
    r = x - hi.astype(F32)
    mid = r.astype(BF16)
    lo = (r - mid.astype(F32)).astype(BF16)
    return hi, mid, lo


def _dot3_rhs(a3_bf16, x):
    return _dot(a3_bf16, jnp.concatenate(_split3(x), axis=0))


def _dot3_lhs(x, b3_bf16):
    return _dot(jnp.concatenate(_split3(x), axis=1), b3_bf16)


def _head_expand_matrix():
    head = lax.broadcasted_iota(jnp.int32, (3 * HEAD_PAD, D_SSM), 0) & (HEAD_PAD - 1)
    chan = lax.broadcasted_iota(jnp.int32, (3 * HEAD_PAD, D_SSM), 1)
    return jnp.where((chan >> HEAD_SHIFT) == head, 1.0, 0.0).astype(BF16)


def _on_tile_rows(i, a_ref, b_ref, fn):
    tm = a_ref.shape[0]
    if b_ref is None:
        fn(slice(0, tm), a_ref, slice(0, tm))
        return
    nb = b_ref.shape[0]
    na = tm - nb
    last = pl.num_programs(0) - 1

    @pl.when(i != last)
    def _():
        fn(slice(0, tm), a_ref, slice(0, tm))

    @pl.when(i == last)
    def _():
        fn(slice(0, na), a_ref, slice(0, na))
        fn(slice(na, tm), b_ref, slice(0, nb))


def _cast_specs(casts, steps, step_of):
    in_specs, out_specs, out_shapes = [], [], []
    for w, r, transpose in casts:
        nblk = pl.cdiv(w.shape[0], r)
        assert r % BF16_ROWS == 0 and nblk <= steps and (w.shape[0] % r == 0 or transpose)
        slab = lambda *idx, nblk=nblk: jnp.minimum(step_of(*idx), nblk - 1)
        in_specs.append(pl.BlockSpec((r, w.shape[1]), lambda *idx, slab=slab: (slab(*idx), 0)))
        if transpose:
            assert r % LANE == 0
            out_specs.append(pl.BlockSpec((w.shape[1], r), lambda *idx, slab=slab: (0, slab(*idx))))
            out_shapes.append(jax.ShapeDtypeStruct(w.shape[::-1], BF16))
        else:
            out_specs.append(pl.BlockSpec((r, w.shape[1]), lambda *idx, slab=slab: (slab(*idx), 0)))
            out_shapes.append(jax.ShapeDtypeStruct(w.shape, BF16))
    return in_specs, out_specs, out_shapes


def _run_casts(cast_in, cast_out, cast_t):
    for ci, co, transpose in zip(cast_in, cast_out, cast_t):
        co[...] = (ci[...].T if transpose else ci[...]).astype(BF16)


def _check_split(n_prompt, n_samp):
    assert (n_prompt + n_samp) % TM == 0 and n_samp < TM and n_samp % BF16_ROWS == 0


def _ffn_kernel(*refs, split, n_next, cast_t, emit_w16, n_done):
    n_cast = len(cast_t)
    refs = list(refs)
    xa_ref = refs.pop(0)
    xb_ref = refs.pop(0) if split else None
    gpre_ref, wg_ref, wu_ref, wd_ref, gpost_ref = refs[:5]
    refs = refs[5:]
    gnext_ref = refs.pop(0) if n_next else None
    cast_in, refs = refs[:n_cast], refs[n_cast:]
    n_out = 2 if n_next else 1
    done_refs, refs = (refs[:n_out], refs[n_out:]) if n_done else ((), refs)
    o_ref = refs.pop(0)
    unext_ref = refs.pop(0) if n_next else None
    cast_out, refs = refs[:n_cast], refs[n_cast:]
    w16_refs, refs = (refs[:3], refs[3:]) if emit_w16 else ((), refs)
    (u_scr,) = refs
    i = pl.program_id(0)
    j = pl.program_id(1)

    def tile_step():
        _run_casts(cast_in, cast_out, cast_t)

        @pl.when(j == 0)
        def _():
            def pre(rows, src, srows):
                u_scr[rows, :] = _rms(src[srows, :], gpre_ref[...]).astype(BF16)
            _on_tile_rows(i, xa_ref, xb_ref, pre)
            o_ref[...] = jnp.zeros_like(o_ref)

        wg, wu, wd = wg_ref[...], wu_ref[...], wd_ref[...]
        if emit_w16:
            wg, wu, wd = wg.astype(BF16), wu.astype(BF16), wd.astype(BF16)
            for ref, w in zip(w16_refs, (wg, wu, wd)):
                ref[...] = w
        u = u_scr[...]
        act = (_silu(_dot(u, wg)) * _dot(u, wu)).astype(BF16)
        o_ref[...] += _dot(act, wd)

        @pl.when(j == pl.num_programs(1) - 1)
        def _():
            def post(rows, src, srows):
                step = rows.stop - rows.start if n_next else NORM_ROWS
                for r in range(0, rows.stop - rows.start, step):
                    dst = slice(rows.start + r, rows.start + r + step)
                    h = (src[srows.start + r:srows.start + r + step, :]
                         + _rms(o_ref[dst, :], gpost_ref[...]))
                    o_ref[dst, :] = h
                    if n_next:
                        unext_ref[dst, :] = _rms(h, gnext_ref[...]).astype(BF16)
            _on_tile_rows(i, xa_ref, xb_ref, post)

    if not n_done:
        tile_step()
        return

    @pl.when(jnp.logical_and(i < n_done, j < o_ref.shape[0] // DONE_ROWS))
    def _():
        rows = pl.ds(pl.multiple_of(j * DONE_ROWS, DONE_ROWS), DONE_ROWS)
        for ref, done in zip((o_ref, unext_ref), done_refs):
            ref[rows, :] = done[...]

    pl.when(i >= n_done)(tile_step)


def _ffn(x, gpre, wg, wu, wd, gpost, gnext=None, x_tail=None, casts=(), rows=None, done=(),
         emit_w16=False, tf=TF, tm=TM, name="ffn"):
    m = rows if rows is not None else x.shape[0] + (0 if x_tail is None else x_tail.shape[0])
    n_done = done[0].shape[0] // tm if done else 0
    assert m % tm == 0 and all(d.shape[0] == n_done * tm for d in done)
    grid = (m // tm, D_FF // tf)
    steps = (grid[0] - n_done) * grid[1]
    row = pl.BlockSpec((tm, D_MODEL), lambda i, j: (i, 0))
    vec = pl.BlockSpec((1, D_MODEL), lambda i, j: (0, 0))
    slab = (lambda i, j: jnp.where(i < n_done, 0, j)) if n_done else (lambda i, j: j)
    wcol = pl.BlockSpec((D_MODEL, tf), lambda i, j: (0, slab(i, j)))
    wrow = pl.BlockSpec((tf, D_MODEL), lambda i, j: (slab(i, j), 0))
    in_specs, args = [pl.BlockSpec((tm, D_MODEL), lambda i, j: (jnp.maximum(i, n_done), 0))], [x]
    if x_tail is not None:
        in_specs.append(pl.BlockSpec(x_tail.shape, lambda i, j: (0, 0)))
        args.append(x_tail)
    in_specs += [vec, wcol, wcol, wrow, vec]
    args += [gpre, wg, wu, wd, 0.5 * gpost]
    out_shape = [jax.ShapeDtypeStruct((m, D_MODEL), F32)]
    out_specs = [row]
    if gnext is not None:
        in_specs.append(vec)
        args.append(gnext)
        out_shape.append(jax.ShapeDtypeStruct((m, D_MODEL), BF16))
        out_specs.append(row)
    assert len(done) in (0, len(out_shape))
    cast_in_specs, cast_out_specs, cast_shapes = _cast_specs(
        casts, steps, lambda i, j: jnp.where(i < n_done, 0, (i - n_done) * grid[1] + j))
    in_specs += cast_in_specs
    args += [w for w, _, _ in casts]
    out_specs += cast_out_specs
    out_shape += cast_shapes
    pieces = tm // DONE_ROWS
    assert not done or (pieces * DONE_ROWS == tm and pieces <= grid[1])
    for d in done:
        in_specs.append(pl.BlockSpec(
            (DONE_ROWS, D_MODEL),
            lambda i, j: (jnp.minimum(i, n_done - 1) * pieces + jnp.minimum(j, pieces - 1), 0)))
        args.append(d)
    if emit_w16:
        out_specs += [wcol, wcol, wrow]
        out_shape += [jax.ShapeDtypeStruct(w.shape, BF16) for w in (wg, wu, wd)]
    return pl.pallas_call(
        functools.partial(_ffn_kernel, split=x_tail is not None, n_next=gnext is not None,
                          cast_t=tuple(t for _, _, t in casts), emit_w16=emit_w16, n_done=n_done),
        grid=grid, in_specs=in_specs, out_specs=out_specs, out_shape=out_shape,
        scratch_shapes=[pltpu.VMEM((tm, D_MODEL), BF16)],
        compiler_params=_params(("arbitrary", "arbitrary")), name=name,
    )(*args)


N_GLU_STEPS = D_CONV // TN
N_Z_STEPS = D_SSM // TN
N_XBC_STEPS = D_XBC // TN


def _inproj_kernel(u_ref, w_ref, wb_ref, wdt_ref, dtb_ref, o_ref, dt_ref):
    j = pl.program_id(1)

    @pl.when(j < N_GLU_STEPS)
    def _():
        u = u_ref[...]
        o_ref[...] = _dot(u, w_ref[...]) * jax.nn.sigmoid(_dot(u, wb_ref[...]))

    @pl.when(j >= N_GLU_STEPS)
    def _():
        o_ref[...] = _dot(u_ref[...], w_ref[...])

    @pl.when(j == pl.num_programs(1) - 1)
    def _():
        x = _dot(u_ref[...], wdt_ref[...]) + dtb_ref[...]
        dt_ref[...] = jnp.maximum(x, 0.0) + jnp.log1p(jnp.exp(-jnp.abs(x)))


def _in_proj(u, w_in, w_dt, dt_bias):
    m = u.shape[0]
    g, nz, nx = N_GLU_STEPS, N_Z_STEPS, N_XBC_STEPS
    assert m % TMP == 0 and g == 1

    def out_slab(j):
        return jnp.where(j < g + nz, j + nx, j - g - nz)

    return pl.pallas_call(
        _inproj_kernel, grid=(m // TMP, g + nz + nx),
        in_specs=[pl.BlockSpec((TMP, D_MODEL), lambda i, j: (i, 0)),
                  pl.BlockSpec((D_MODEL, TN), lambda i, j: (0, jnp.where(j < g, j, j + g))),
                  pl.BlockSpec((D_MODEL, TN), lambda i, j: (0, g)),
                  pl.BlockSpec((D_MODEL, HEAD_PAD), lambda i, j: (0, 0)),
                  pl.BlockSpec((1, HEAD_PAD), lambda i, j: (0, 0))],
        out_specs=[pl.BlockSpec((TMP, TN), lambda i, j: (i, out_slab(j))),
                   pl.BlockSpec((TMP, HEAD_PAD), lambda i, j: (i, 0))],
        out_shape=[jax.ShapeDtypeStruct((m, MIX_W), F32), jax.ShapeDtypeStruct((m, HEAD_PAD), F32)],
        compiler_params=_params(("arbitrary", "arbitrary"), VMEM_LIMIT_MAX), name="in_proj",
    )(u, w_in, w_in, w_dt, dt_bias)


def _ln_swish(y, g, b):
    mu = jnp.mean(y, axis=-1, keepdims=True)
    yc = y - mu
    yn = yc * lax.rsqrt(jnp.mean(yc * yc, axis=-1, keepdims=True) + EPS) * g + b
    return _silu(yn)


def _pconv_kernel(v_ref, w_ref, b_ref, lg_ref, lb_ref, *rest, cast_t):
    n_cast = len(cast_t)
    cast_in, (o_ref, tail_ref), rest = rest[:n_cast], rest[n_cast:n_cast + 2], rest[n_cast + 2:]
    cast_out, (xpad_scr, conv_scr) = rest[:n_cast], rest[n_cast:]
    t = pl.program_id(1)
    _run_casts(cast_in, cast_out, cast_t)

    @pl.when(t == 0)
    def _():
        xpad_scr[0:CARRY, :] = jnp.zeros((CARRY, D_CONV), F32)

    xpad_scr[CARRY:CARRY + TL, :] = v_ref[...]
    first = CARRY - (CONV_W - 1)
    hb = CONV_ROWS
    n_rows = CARRY + TL
    for cb in range(D_CONV // LANE):
        lanes = slice(cb * LANE, (cb + 1) * LANE)
        xfull = xpad_scr[:, lanes]
        conv_scr[:, lanes] = jnp.broadcast_to(b_ref[:, lanes], (TL, LANE))
        for phase in range(SUBLANE):
            xs = pltpu.roll(xfull, n_rows - phase, axis=0) if phase else xfull
            taps = [k for k in range(CONV_W) if (first + k) % SUBLANE == phase]
            for base in range(0, TL, hb):
                acc = conv_scr[base:base + hb, lanes]
                for k in taps:
                    off = SUBLANE * ((first + k) // SUBLANE)
                    acc = acc + w_ref[k:k + 1, lanes] * xs[base + off:base + off + hb]
                conv_scr[base:base + hb, lanes] = acc
    xpad_scr[0:CARRY, :] = xpad_scr[TL:TL + CARRY, :]
    for r in range(0, TL, NORM_ROWS):
        rows = slice(r, r + NORM_ROWS)
        o_ref[rows, :] = _ln_swish(conv_scr[rows, :], lg_ref[...], lb_ref[...]).astype(BF16)

    @pl.when(t == pl.num_programs(1) - 1)
    def _():
        tail_ref[0] = xpad_scr[0:CARRY, :]


def _prompt_conv(mix, n_batch, seq, w, b, lg, lb, casts=()):
    vec = pl.BlockSpec((1, D_CONV), lambda bi, t: (0, 0))
    steps = seq // TL
    cast_in_specs, cast_out_specs, cast_shapes = _cast_specs(
        casts, n_batch * steps, lambda bi, t: bi * steps + t)
    return pl.pallas_call(
        functools.partial(_pconv_kernel, cast_t=tuple(t for _, _, t in casts)),
        grid=(n_batch, steps),
        in_specs=[pl.BlockSpec((TL, D_CONV), lambda bi, t: (bi * steps + t, GLU_COL)),
                  pl.BlockSpec((CONV_W, D_CONV), lambda bi, t: (0, 0)), vec, vec, vec] + cast_in_specs,
        out_specs=[pl.BlockSpec((TL, D_CONV), lambda bi, t: (bi * steps + t, 0)),
                   pl.BlockSpec((1, CARRY, D_CONV), lambda bi, t: (bi, 0, 0))] + cast_out_specs,
        out_shape=[jax.ShapeDtypeStruct((n_batch * seq, D_CONV), BF16),
                   jax.ShapeDtypeStruct((n_batch, CARRY, D_CONV), F32)] + cast_shapes,
        scratch_shapes=[pltpu.VMEM((CARRY + TL, D_CONV), F32), pltpu.VMEM((TL, D_CONV), F32)],
        compiler_params=_params(("arbitrary", "arbitrary")), name="prompt_conv",
    )(mix, w, b, lg, lb, *[c for c, _, _ in casts])


def _gated_norm(y, z, g):
    yg = y * _silu(z)
    return yg * lax.rsqrt(jnp.mean(yg * yg, axis=-1, keepdims=True) + EPS) * g


def _sample_state_update(step, dec_ref, h0_ref, bc_ref, hn_ref, yt_scr, xdt_t_scr):
    per_step = h0_ref.shape[0]
    nb = bc_ref.shape[0]
    seq_i = lax.broadcasted_iota(jnp.int32, (nb, D_STATE), 0)
    seqs = [step * per_step + bb for bb in range(per_step)]
    brows = [bc_ref[pl.ds(b, 1), :] for b in seqs]

    def one_hot_rows(g):
        cols = slice(g * D_STATE, (g + 1) * D_STATE)
        return jnp.concatenate([jnp.where(seq_i == b, brow[:, cols], 0.0).astype(BF16)
                                for b, brow in zip(seqs, brows)], axis=1)

    for g in range(N_GROUPS):
        rows = slice(g * GROUP_W, (g + 1) * GROUP_W)
        s_new = _dot(xdt_t_scr[rows, :], one_hot_rows(g))
        h_all = []
        for bb, b in enumerate(seqs):
            parts = []
            for r in range(HEADS_PER_GROUP):
                h = g * HEADS_PER_GROUP + r
                hr = slice(h * HEAD_DIM, (h + 1) * HEAD_DIM)
                parts.append(h0_ref[bb, hr, :] * dec_ref[b * N_HEADS + h]
                             + s_new[r * HEAD_DIM:(r + 1) * HEAD_DIM, bb * D_STATE:(bb + 1) * D_STATE])
            h_new = jnp.concatenate(parts, axis=0)
            hn_ref[bb, rows, :] = h_new
            h_all.append(h_new.astype(BF16))
        yt_scr[rows, :] += lax.dot_general(jnp.concatenate(h_all, axis=1), one_hot_rows(N_GROUPS + g),
                                           (((1,), (1,)), ((), ())), preferred_element_type=F32)


def _pssd_kernel(xbc_ref, z_ref, dt_ref, cw_ref, cb_ref, a_ref, expand_ref, dexp_ref, ng_ref,
                 dec_ref, h0_ref, xdt_ref, bc_ref, skip_ref, zs_ref,
                 y_ref, hfin_ref, tail_ref, hn_ref, ys_ref,
                 state_scr, xpad_scr, xc_scr, yt_scr, xdt_t_scr):
    c = pl.program_id(1)
    q = CHUNK
    step = pl.program_id(0) * pl.num_programs(1) + c

    @pl.when(step == 0)
    def _():
        yt_scr[...] = jnp.zeros_like(yt_scr)
        xdt_t_scr[...] = xdt_ref[...].T.astype(BF16)

    @pl.when(c == 0)
    def _():
        state_scr[...] = jnp.zeros_like(state_scr)
        xpad_scr[0:XCARRY, :] = jnp.zeros((XCARRY, D_XBC), F32)

    xpad_scr[XCARRY:XCARRY + q, :] = xbc_ref[...]
    for cb in range(D_XBC // 512):
        lanes = slice(cb * 512, (cb + 1) * 512)
        acc = jnp.broadcast_to(cb_ref[:, lanes], (q, 512))
        for j in range(SSM_CONV_W):
            k = SSM_CONV_W - 1 - j
            acc = acc + cw_ref[k:k + 1, lanes] * xpad_scr[XCARRY - j:XCARRY - j + q, lanes]
        xc_scr[:, lanes] = _silu(acc)
    xpad_scr[0:XCARRY, :] = xpad_scr[q:q + XCARRY, :]

    row_i = lax.broadcasted_iota(jnp.int32, (q, q), 0)
    col_i = lax.broadcasted_iota(jnp.int32, (q, q), 1)
    tril = row_i >= col_i
    tri = jnp.where(tril, 1.0, 0.0).astype(BF16)
    expand = expand_ref[...]

    dt = dt_ref[...]
    tri3 = jnp.concatenate([tri, tri, tri], axis=1)
    a_cs = _dot3_rhs(tri3, dt * a_ref[...])
    a_cs_t = a_cs.T
    dt_exp = _dot3_lhs(dt, expand)
    acs_exp = _dot3_lhs(a_cs, expand)
    last = acs_exp[q - 1:q, :]
    lane_lo = lax.broadcasted_iota(jnp.int32, (q, LANE), 1) < HEAD_DIM

    for g in range(N_GROUPS):
        ch = slice(g * GROUP_W, (g + 1) * GROUP_W)
        xs = xc_scr[:, ch]
        bg = xc_scr[:, D_SSM + g * D_STATE:D_SSM + (g + 1) * D_STATE]
        cg = xc_scr[:, D_SSM + (N_GROUPS + g) * D_STATE:D_SSM + (N_GROUPS + g + 1) * D_STATE]
        bg16 = bg.astype(BF16)
        cg16 = cg.astype(BF16)
        xdt = xs * dt_exp[:, ch]
        acs_g = acs_exp[:, ch]
        cb = lax.dot_general(cg16, bg16, (((1,), (1,)), ((), ())), preferred_element_type=F32)
        st = state_scr[:, ch]
        y = _dot(cg16, st.astype(BF16)) * jnp.exp(acs_g)
        pieces = []
        for pr in range(HEADS_PER_GROUP // 2):
            xpair = xdt[:, pr * LANE:(pr + 1) * LANE].astype(BF16)
            both = []
            for half in range(2):
                h = g * HEADS_PER_GROUP + 2 * pr + half
                seg = a_cs[:, h:h + 1] - a_cs_t[h:h + 1, :]
                decay = jnp.exp(jnp.where(tril, seg, NEG_BIG))
                both.append(_dot((cb * decay).astype(BF16), xpair))
            pieces.append(jnp.where(lane_lo, both[0], both[1]))
        y = y + jnp.concatenate(pieces, axis=1) + dexp_ref[:, ch] * xs
        xdec = (xdt * jnp.exp(last[:, ch] - acs_g)).astype(BF16)
        s_new = lax.dot_general(bg16, xdec, (((0,), (0,)), ((), ())), preferred_element_type=F32)
        state_scr[:, ch] = st * jnp.exp(last[:, ch]) + s_new
        y_ref[:, ch] = _gated_norm(y, z_ref[:, ch], ng_ref[:, ch]).astype(BF16)

    _sample_state_update(step, dec_ref, h0_ref, bc_ref, hn_ref, yt_scr, xdt_t_scr)

    @pl.when(c == pl.num_programs(1) - 1)
    def _():
        hfin_ref[0] = state_scr[...].T
        tail_ref[0] = xpad_scr[0:XCARRY, :]

    @pl.when(step == pl.num_programs(0) * pl.num_programs(1) - 1)
    def _():
        y = yt_scr[...].T + skip_ref[...]
        for g in range(N_GROUPS):
            ch = slice(g * GROUP_W, (g + 1) * GROUP_W)
            ys_ref[:, ch] = _gated_norm(y[:, ch], zs_ref[:, ch], ng_ref[:, ch]).astype(BF16)


def _ssd(mix, dt, n_batch, seq, cw, cb, a_pad, d_exp, ng, dec, h0, xdt, bc, skip):
    nc = seq // CHUNK
    n_prompt = n_batch * seq
    n_samp = h0.shape[0]
    per_step = n_samp // (n_batch * nc)
    assert per_step * n_batch * nc == n_samp and n_prompt % n_samp == 0

    def rows(w, col=0):
        return pl.BlockSpec((CHUNK, w), lambda bi, c: (bi * nc + c, col))

    def vec(*s):
        return pl.BlockSpec(s, lambda bi, c: (0,) * len(s))

    states = pl.BlockSpec((per_step, D_SSM, D_STATE), lambda bi, c: (bi * nc + c, 0, 0))
    return pl.pallas_call(
        _pssd_kernel, grid=(n_batch, nc),
        in_specs=[rows(D_XBC), rows(D_SSM, Z_COL), rows(HEAD_PAD), vec(SSM_CONV_W, D_XBC),
                  vec(1, D_XBC), vec(1, HEAD_PAD), vec(3 * HEAD_PAD, D_SSM), vec(1, D_SSM),
                  vec(1, D_SSM), pl.BlockSpec(memory_space=pltpu.SMEM), states, vec(n_samp, D_SSM),
                  vec(n_samp, D_XBC - D_SSM), vec(n_samp, D_SSM),
                  pl.BlockSpec((n_samp, D_SSM), lambda bi, c: (n_prompt // n_samp, Z_COL))],
        out_specs=[rows(D_SSM), pl.BlockSpec((1, D_SSM, D_STATE), lambda bi, c: (bi, 0, 0)),
                   pl.BlockSpec((1, XCARRY, D_XBC), lambda bi, c: (bi, 0, 0)),
                   states, vec(n_samp, D_SSM)],
        out_shape=[jax.ShapeDtypeStruct((n_prompt, D_SSM), BF16),
                   jax.ShapeDtypeStruct((n_batch, D_SSM, D_STATE), F32),
                   jax.ShapeDtypeStruct((n_batch, XCARRY, D_XBC), F32),
                   jax.ShapeDtypeStruct((n_samp, D_SSM, D_STATE), F32),
                   jax.ShapeDtypeStruct((n_samp, D_SSM), BF16)],
        scratch_shapes=[pltpu.VMEM((D_STATE, D_SSM), F32), pltpu.VMEM((XCARRY + CHUNK, D_XBC), F32),
                        pltpu.VMEM((CHUNK, D_XBC), F32),
                        pltpu.VMEM((D_SSM, n_samp), F32), pltpu.VMEM((D_SSM, n_samp), BF16)],
        compiler_params=_params(("arbitrary", "arbitrary")), name="ssd",
    )(mix, mix, dt, cw, cb, a_pad, _head_expand_matrix(), d_exp, ng, dec, h0, xdt, bc, skip, mix)


def _sprep_kernel(v_ref, cst_ref, w_ref, b_ref, lg_ref, lb_ref,
                  xbc_ref, xst_ref, cw_ref, cb_ref, dt_ref, a_ref, dexp_ref,
                  co_ref, ncst_ref, nxst_ref, xdt_ref, bc_ref, dec_ref, skip_ref):
    nb = v_ref.shape[0]
    v = v_ref[...]
    acc = jnp.broadcast_to(b_ref[...], (nb, D_CONV)) + w_ref[CONV_W - 1:CONV_W, :] * v
    for k in range(CONV_W - 1):
        acc = acc + w_ref[k:k + 1, :] * cst_ref[k]
    co_ref[...] = _ln_swish(acc, lg_ref[...], lb_ref[...]).astype(BF16)
    for k in range(CONV_W - 2):
        ncst_ref[k] = cst_ref[k + 1]
    ncst_ref[CONV_W - 2] = v

    xn = xbc_ref[...]
    acc = jnp.broadcast_to(cb_ref[...], (nb, D_XBC)) + cw_ref[SSM_CONV_W - 1:SSM_CONV_W, :] * xn
    for k in range(SSM_CONV_W - 1):
        acc = acc + cw_ref[k:k + 1, :] * xst_ref[k]
    xc = _silu(acc)
    for k in range(SSM_CONV_W - 2):
        nxst_ref[k] = xst_ref[k + 1]
    nxst_ref[SSM_CONV_W - 2] = xn

    xs = xc[:, :D_SSM]
    bc_ref[...] = xc[:, D_SSM:]
    dt = dt_ref[...]
    dec_ref[...] = jnp.exp(dt * a_ref[...])
    dt_exp = _dot3_lhs(dt, _head_expand_matrix())
    xdt_ref[...] = xs * dt_exp
    skip_ref[...] = dexp_ref[...] * xs


def _sample_prep(mix, dt, n_prompt, n_samp, cst, xst, mw, mb, lg, lb, cw, cb, a_pad, d_exp):
    sb = SAMPLE_BLOCK
    off = n_prompt // sb
    srow = lambda w, col=0: pl.BlockSpec((sb, w), lambda i: (off + i, col))
    orow = lambda w: pl.BlockSpec((sb, w), lambda i: (i, 0))
    taps = lambda k, w: pl.BlockSpec((k, sb, w), lambda i: (0, i, 0))
    const = lambda *s: pl.BlockSpec(s, lambda i: (0,) * len(s))
    cst_t = jnp.swapaxes(cst, 0, 1)
    xst_t = jnp.swapaxes(xst, 0, 1)
    c_samp, ncst_t, nxst_t, xdt, bc, dec, skip = pl.pallas_call(
        _sprep_kernel, grid=(n_samp // sb,),
        in_specs=[srow(D_CONV, GLU_COL), taps(CONV_W - 1, D_CONV), const(CONV_W, D_CONV),
                  const(1, D_CONV), const(1, D_CONV), const(1, D_CONV),
                  srow(D_XBC), taps(SSM_CONV_W - 1, D_XBC), const(SSM_CONV_W, D_XBC),
                  const(1, D_XBC), srow(HEAD_PAD), const(1, HEAD_PAD), const(1, D_SSM)],
        out_specs=[orow(D_CONV), taps(CONV_W - 1, D_CONV), taps(SSM_CONV_W - 1, D_XBC),
                   orow(D_SSM), orow(D_XBC - D_SSM), orow(HEAD_PAD), orow(D_SSM)],
        out_shape=[jax.ShapeDtypeStruct((n_samp, D_CONV), BF16),
                   jax.ShapeDtypeStruct((CONV_W - 1, n_samp, D_CONV), F32),
                   jax.ShapeDtypeStruct((SSM_CONV_W - 1, n_samp, D_XBC), F32),
                   jax.ShapeDtypeStruct((n_samp, D_SSM), F32),
                   jax.ShapeDtypeStruct((n_samp, D_XBC - D_SSM), F32),
                   jax.ShapeDtypeStruct((n_samp, HEAD_PAD), F32),
                   jax.ShapeDtypeStruct((n_samp, D_SSM), F32)],
        compiler_params=_params(("arbitrary",)), name="sample_prep",
    )(mix, cst_t, mw, mb, lg, lb, mix, xst_t, cw, cb, dt, a_pad, d_exp)
    dec_flat = dec[:, :N_HEADS].reshape(-1)
    return (c_samp, jnp.swapaxes(ncst_t, 0, 1), jnp.swapaxes(nxst_t, 0, 1), xdt, bc, dec_flat, skip)


def _outproj_kernel(c_ref, cs_ref, y_ref, ys_ref, w_ref, h_ref, g_ref, o_ref):
    i = pl.program_id(0)
    last = pl.num_programs(0) - 1
    na = TM - cs_ref.shape[0]

    def run(c, y):
        mix = _dot(c, w_ref[0:D_CONV, :]) + _dot(y, w_ref[D_CONV:, :])
        o_ref[...] = h_ref[...] + _rms(mix, g_ref[...])

    @pl.when(i != last)
    def _():
        run(c_ref[...], y_ref[...])

    @pl.when(i == last)
    def _():
        run(jnp.concatenate([c_ref[0:na, :], cs_ref[...]], axis=0),
            jnp.concatenate([y_ref[0:na, :], ys_ref[...]], axis=0))


def _out_proj(c_prompt, c_samp, y_prompt, y_samp, w_out, h, g):
    m = h.shape[0]
    n_samp = c_samp.shape[0]
    row = pl.BlockSpec((TM, D_MODEL), lambda i: (i, 0))
    return pl.pallas_call(
        _outproj_kernel, grid=(m // TM,),
        in_specs=[pl.BlockSpec((TM, D_CONV), lambda i: (i, 0)),
                  pl.BlockSpec((n_samp, D_CONV), lambda i: (0, 0)),
                  pl.BlockSpec((TM, D_SSM), lambda i: (i, 0)),
                  pl.BlockSpec((n_samp, D_SSM), lambda i: (0, 0)),
                  pl.BlockSpec((D_CONV + D_SSM, D_MODEL), lambda i: (0, 0)),
                  row, pl.BlockSpec((1, D_MODEL), lambda i: (0, 0))],
        out_specs=row, out_shape=jax.ShapeDtypeStruct((m, D_MODEL), F32),
        compiler_params=_params(("arbitrary",)), name="out_proj",
    )(c_prompt, c_samp, y_prompt, y_samp, w_out, h, g)


def _ple_kernel(h_ref, pa_ref, pb_ref, gpre_ref, wg_ref, wp_ref, gpost_ref, oa_ref, ob_ref, emb_scr):
    i = pl.program_id(0)

    def embed(rows, src, srows):
        emb_scr[rows, :] = _dot(src[srows, :].astype(BF16), wp_ref[...])
    _on_tile_rows(i, pa_ref, pb_ref, embed)

    h = h_ref[...]
    gate = jax.nn.sigmoid(_dot(_rms(h, gpre_ref[...]).astype(BF16), wg_ref[...]))
    oa_ref[...] = h + _rms(gate * emb_scr[...], gpost_ref[...])

    @pl.when(i == pl.num_programs(0) - 1)
    def _():
        nb = ob_ref.shape[0]
        ob_ref[...] = oa_ref[TM - nb:TM, :]


def _ple(h, p_prompt, p_samp, gpre, wg, wp, gpost):
    m = h.shape[0]
    n_prompt, n_samp = p_prompt.shape[0], p_samp.shape[0]
    row = pl.BlockSpec((TM, D_MODEL), lambda i: (i, 0))
    vec = pl.BlockSpec((1, D_MODEL), lambda i: (0, 0))
    return pl.pallas_call(
        _ple_kernel, grid=(m // TM,),
        in_specs=[row, pl.BlockSpec((TM, PLE_DIM), lambda i: (i, 0)),
                  pl.BlockSpec((n_samp, PLE_DIM), lambda i: (0, 0)), vec,
                  pl.BlockSpec((D_MODEL, D_MODEL), lambda i: (0, 0)),
                  pl.BlockSpec((PLE_DIM, D_MODEL), lambda i: (0, 0)), vec],
        out_specs=[row, pl.BlockSpec((n_samp, D_MODEL), lambda i: (0, 0))],
        out_shape=[jax.ShapeDtypeStruct((n_prompt, D_MODEL), F32),
                   jax.ShapeDtypeStruct((n_samp, D_MODEL), F32)],
        scratch_shapes=[pltpu.VMEM((TM, D_MODEL), F32)],
        compiler_params=_params(("arbitrary",)), name="ple")(h, p_prompt, p_samp, gpre, wg, wp, gpost)


def _layer(x_prompt, x_samp, p_prompt, p_samp, n_batch, seq, cst, xst, h0, lw):
    (norm_ffn1_pre, w_ffn1_gate, w_ffn1_up, w_ffn1_down, norm_ffn1_post,
     norm_mix_pre, w_in, conv_mod_w, conv_mod_b, conv_mod_ln_g, conv_mod_ln_b,
     ssm_conv_w, ssm_conv_b, dt_bias, a_log, d_skip, ssm_norm_g, w_out, norm_mix_post,
     norm_ffn2_pre, w_ffn2_gate, w_ffn2_up, w_ffn2_down, norm_ffn2_post,
     norm_ple_pre, w_ple_gate, w_ple_proj, norm_ple_post) = lw
    n_prompt, n_samp = x_prompt.shape[0], x_samp.shape[0]
    m = n_prompt + n_samp
    _check_split(n_prompt, n_samp)
    row2 = lambda t: t.reshape(1, -1)

    casts_ffn1 = ((jnp.swapaxes(w_in, 0, 1), LANE, True),)
    casts_conv = ((w_out, 128, False), (w_ffn2_gate, 64, False), (w_ffn2_up, 64, False),
                  (w_ffn2_down, 176, False), (w_ple_gate, 64, False), (w_ple_proj, 16, False))
    ffn1 = (row2(norm_ffn1_pre), row2(norm_ffn1_post), row2(norm_mix_pre))
    h1_head, u_head, wg1, wu1, wd1 = _ffn(
        x_prompt, ffn1[0], w_ffn1_gate, w_ffn1_up, w_ffn1_down, ffn1[1], gnext=ffn1[2],
        rows=FIRST_TILES * TM, emit_w16=True, tf=TF_FIRST, tm=FIRST_TILES * TM, name="ffn_first")
    h1, u, w_in16 = _ffn(
        x_prompt, ffn1[0], wg1, wu1, wd1, ffn1[1], gnext=ffn1[2], x_tail=x_samp, casts=casts_ffn1,
        done=(h1_head, u_head), name="ffn_rest")

    d_proj = w_in.shape[1]
    w_dt = jnp.pad(w_in16[:, d_proj - N_HEADS:], ((0, 0), (0, HEAD_PAD - N_HEADS)))
    pad_h = lambda t: jnp.pad(t.astype(F32), (0, HEAD_PAD - N_HEADS)).reshape(1, HEAD_PAD)
    mix, dt = _in_proj(u, w_in16, w_dt, pad_h(dt_bias))

    a = -jnp.exp(a_log.astype(F32))
    a_pad = pad_h(a)
    d_exp = row2(jnp.repeat(d_skip.astype(F32), HEAD_DIM))
    cw, cb = ssm_conv_w, row2(ssm_conv_b)
    mw, mb, lg, lb = conv_mod_w, row2(conv_mod_b), row2(conv_mod_ln_g), row2(conv_mod_ln_b)
    ng = row2(ssm_norm_g)

    c_prompt, glu_tail, w_out16, wg2, wu2, wd2, wpg, wpp = _prompt_conv(
        mix, n_batch, seq, mw, mb, lg, lb, casts=casts_conv)
    c_samp, new_cst_samp, new_xst_samp, xdt, bc, dec, skip = _sample_prep(
        mix, dt, n_prompt, n_samp, cst, xst, mw, mb, lg, lb, cw, cb, a_pad, d_exp)
    y_mix_prompt, hfin_prompt, xbc_tail, hn, y_mix_samp = _ssd(
        mix, dt, n_batch, seq, cw, cb, a_pad, d_exp, ng,
        dec, h0.reshape(n_samp, D_SSM, D_STATE), xdt, bc, skip)

    h2 = _out_proj(c_prompt, c_samp, y_mix_prompt, y_mix_samp, w_out16, h1, row2(norm_mix_post))
    (h3,) = _ffn(h2, row2(norm_ffn2_pre), wg2, wu2, wd2, row2(norm_ffn2_post))
    y_prompt, y_samp = _ple(h3, p_prompt, p_samp, row2(norm_ple_pre), wpg, wpp, row2(norm_ple_post))

    new_cst_prompt = glu_tail[:, CARRY - (CONV_W - 1):]
    new_xst_prompt = xbc_tail[:, XCARRY - (SSM_CONV_W - 1):]
    new_h_prompt = hfin_prompt.reshape(n_batch, N_HEADS, HEAD_DIM, D_STATE)
    new_h_samp = hn.reshape(n_samp, N_HEADS, HEAD_DIM, D_STATE)
    return (y_prompt, y_samp, new_cst_prompt, new_xst_prompt, new_h_prompt,
            new_cst_samp, new_xst_samp, new_h_samp)


def kernel(x_prompt, x_sample, state_conv_mod, state_ssm_conv, state_ssm, p_prompt, p_sample,
           norm_ffn1_pre, w_ffn1_gate, w_ffn1_up, w_ffn1_down, norm_ffn1_post,
           norm_mix_pre, w_in, conv_mod_w, conv_mod_b, conv_mod_ln_g, conv_mod_ln_b,
           ssm_conv_w, ssm_conv_b, dt_bias, a_log, d_skip, ssm_norm_g, w_out, norm_mix_post,
           norm_ffn2_pre, w_ffn2_gate, w_ffn2_up, w_ffn2_down, norm_ffn2_post,
           norm_ple_pre, w_ple_gate, w_ple_proj, norm_ple_post):
    weights = (norm_ffn1_pre, w_ffn1_gate, w_ffn1_up, w_ffn1_down, norm_ffn1_post,
               norm_mix_pre, w_in, conv_mod_w, conv_mod_b, conv_mod_ln_g, conv_mod_ln_b,
               ssm_conv_w, ssm_conv_b, dt_bias, a_log, d_skip, ssm_norm_g, w_out, norm_mix_post,
               norm_ffn2_pre, w_ffn2_gate, w_ffn2_up, w_ffn2_down, norm_ffn2_post,
               norm_ple_pre, w_ple_gate, w_ple_proj, norm_ple_post)
    n_batch, seq, _ = x_prompt.shape
    n_samp = x_sample.shape[0]
    n_prompt = n_batch * seq
    depth = norm_ffn1_pre.shape[0]
    xp = x_prompt.reshape(n_prompt, D_MODEL)
    xs = x_sample.reshape(n_samp, D_MODEL)
    outs = [[] for _ in range(6)]
    for i in range(depth):
        res = _layer(xp, xs, p_prompt[i].reshape(n_prompt, PLE_DIM), p_sample[i].reshape(n_samp, PLE_DIM),
                     n_batch, seq, state_conv_mod[i], state_ssm_conv[i], state_ssm[i],
                     tuple(w[i] for w in weights))
        xp, xs = res[0], res[1]
        for lst, r in zip(outs, res[2:]):
            lst.append(r)
    return ((xp.reshape(n_batch, seq, D_MODEL), xs.reshape(n_samp, 1, D_MODEL))
            + tuple(jnp.stack(lst, axis=0) for lst in outs))
```

```python
import functools

import jax
import jax.numpy as jnp
from jax import lax
from jax.experimental import pallas as pl
from jax.experimental.pallas import tpu as pltpu

F32 = jnp.float32
BF16 = jnp.bfloat16

D_MODEL = 2048
D_FF = 5632
D_CONV = 1024
D_SSM = 3072
N_HEADS = 48
HEAD_DIM = 64
N_GROUPS = 8
HEADS_PER_GROUP = 6
GROUP_W = HEADS_PER_GROUP * HEAD_DIM
D_STATE = 128
D_XBC = D_SSM + 2 * N_GROUPS * D_STATE
MIX_W = D_XBC + D_CONV + D_SSM
GLU_COL = D_XBC // D_CONV
Z_COL = (D_XBC + D_CONV) // D_SSM
assert GLU_COL * D_CONV == D_XBC and Z_COL * D_SSM == D_XBC + D_CONV
CONV_W = 31
SSM_CONV_W = 4
CHUNK = 128
PLE_DIM = 256
EPS = 1e-6
NEG_BIG = -1e30

LANE = 128
SUBLANE = 8
BF16_ROWS = 16
HEAD_PAD = LANE
HEAD_SHIFT = HEAD_DIM.bit_length() - 1
assert 1 << HEAD_SHIFT == HEAD_DIM and 2 * HEAD_DIM == LANE

TM = 640
TF = 512
TF_FIRST = 256
FIRST_TILES = 2
DONE_ROWS = 64
TMP = 1664
TN = 1024
TL = 256
CARRY = 32
CONV_ROWS = 64
NORM_ROWS = 16
XCARRY = SUBLANE
SAMPLE_BLOCK = 32
VMEM_LIMIT = 56 * 1024 * 1024
VMEM_LIMIT_MAX = 58 * 1024 * 1024


def _params(dims, vmem=VMEM_LIMIT):
    return pltpu.CompilerParams(dimension_semantics=dims, vmem_limit_bytes=vmem)


def _rms(x, g):
    return x * lax.rsqrt(jnp.mean(x * x, axis=-1, keepdims=True) + EPS) * g


def _silu(x):
    return x * jax.nn.sigmoid(x)


def _dot(a, b):
    return jnp.dot(a, b, preferred_element_type=F32)


def _split3(x):
    hi = x.astype(BF16)
    r = x - hi.astype(F32)
    mid = r.astype(BF16)
    lo = (r - mid.astype(F32)).astype(BF16)
    return hi, mid, lo


def _dot3_rhs(a3_bf16, x):
    return _dot(a3_bf16, jnp.concatenate(_split3(x), axis=0))


def _dot3_lhs(x, b3_bf16):
    return _dot(jnp.concatenate(_split3(x), axis=1), b3_bf16)


def _head_expand_matrix():
    head = lax.broadcasted_iota(jnp.int32, (3 * HEAD_PAD, D_SSM), 0) & (HEAD_PAD - 1)
    chan = lax.broadcasted_iota(jnp.int32, (3 * HEAD_PAD, D_SSM), 1)
    return jnp.where((chan >> HEAD_SHIFT) == head, 1.0, 0.0).astype(BF16)


def _on_tile_rows(i, a_ref, b_ref, fn):
    tm = a_ref.shape[0]
    if b_ref is None:
        fn(slice(0, tm), a_ref, slice(0, tm))
        return
    nb = b_ref.shape[0]
    na = tm - nb
    last = pl.num_programs(0) - 1

    @pl.when(i != last)
    def _():
        fn(slice(0, tm), a_ref, slice(0, tm))

    @pl.when(i == last)
    def _():
        fn(slice(0, na), a_ref, slice(0, na))
        fn(slice(na, tm), b_ref, slice(0, nb))


def _cast_specs(casts, steps, step_of):
    in_specs, out_specs, out_shapes = [], [], []
    for w, r, transpose in casts:
        nblk = pl.cdiv(w.shape[0], r)
        assert r % BF16_ROWS == 0 and nblk <= steps and (w.shape[0] % r == 0 or transpose)
        slab = lambda *idx, nblk=nblk: jnp.minimum(step_of(*idx), nblk - 1)
        in_specs.append(pl.BlockSpec((r, w.shape[1]), lambda *idx, slab=slab: (slab(*idx), 0)))
        if transpose:
            assert r % LANE == 0
            out_specs.append(pl.BlockSpec((w.shape[1], r), lambda *idx, slab=slab: (0, slab(*idx))))
            out_shapes.append(jax.ShapeDtypeStruct(w.shape[::-1], BF16))
        else:
            out_specs.append(pl.BlockSpec((r, w.shape[1]), lambda *idx, slab=slab: (slab(*idx), 0)))
            out_shapes.append(jax.ShapeDtypeStruct(w.shape, BF16))
    return in_specs, out_specs, out_shapes


def _run_casts(cast_in, cast_out, cast_t):
    for ci, co, transpose in zip(cast_in, cast_out, cast_t):
        co[...] = (ci[...].T if transpose else ci[...]).astype(BF16)


def _check_split(n_prompt, n_samp):
    assert (n_prompt + n_samp) % TM == 0 and n_samp < TM and n_samp % BF16_ROWS == 0


def _ffn_kernel(*refs, split, n_next, cast_t, emit_w16, n_done):
    n_cast = len(cast_t)
    refs = list(refs)
    xa_ref = refs.pop(0)
    xb_ref = refs.pop(0) if split else None
    gpre_ref, wg_ref, wu_ref, wd_ref, gpost_ref = refs[:5]
    refs = refs[5:]
    gnext_ref = refs.pop(0) if n_next else None
    cast_in, refs = refs[:n_cast], refs[n_cast:]
    n_out = 2 if n_next else 1
    done_refs, refs = (refs[:n_out], refs[n_out:]) if n_done else ((), refs)
    o_ref = refs.pop(0)
    unext_ref = refs.pop(0) if n_next else None
    cast_out, refs = refs[:n_cast], refs[n_cast:]
    w16_refs, refs = (refs[:3], refs[3:]) if emit_w16 else ((), refs)
    (u_scr,) = refs
    i = pl.program_id(0)
    j = pl.program_id(1)

    def slab_product():
        _run_casts(cast_in, cast_out, cast_t)
        wg, wu, wd = wg_ref[...], wu_ref[...], wd_ref[...]
        if emit_w16:
            wg, wu, wd = wg.astype(BF16), wu.astype(BF16), wd.astype(BF16)
            for ref, w in zip(w16_refs, (wg, wu, wd)):
                ref[...] = w
        u = u_scr[...]
        act = (_silu(_dot(u, wg)) * _dot(u, wu)).astype(BF16)
        return _dot(act, wd)

    def tile_step():
        @pl.when(j == 0)
        def _():
            def pre(rows, src, srows):
                u_scr[rows, :] = _rms(src[srows, :], gpre_ref[...]).astype(BF16)
            _on_tile_rows(i, xa_ref, xb_ref, pre)
            o_ref[...] = slab_product()

        @pl.when(j > 0)
        def _():
            o_ref[...] += slab_product()

        @pl.when(j == pl.num_programs(1) - 1)
        def _():
            def post(rows, src, srows):
                step = rows.stop - rows.start if n_next else NORM_ROWS
                for r in range(0, rows.stop - rows.start, step):
                    dst = slice(rows.start + r, rows.start + r + step)
                    h = (src[srows.start + r:srows.start + r + step, :]
                         + _rms(o_ref[dst, :], gpost_ref[...]))
                    o_ref[dst, :] = h
                    if n_next:
                        unext_ref[dst, :] = _rms(h, gnext_ref[...]).astype(BF16)
            _on_tile_rows(i, xa_ref, xb_ref, post)

    if not n_done:
        tile_step()
        return

    @pl.when(jnp.logical_and(i < n_done, j < o_ref.shape[0] // DONE_ROWS))
    def _():
        rows = pl.ds(pl.multiple_of(j * DONE_ROWS, DONE_ROWS), DONE_ROWS)
        for ref, done in zip((o_ref, unext_ref), done_refs):
            ref[rows, :] = done[...]

    pl.when(i >= n_done)(tile_step)


def _ffn(x, gpre, wg, wu, wd, gpost, gnext=None, x_tail=None, casts=(), rows=None, done=(),
         emit_w16=False, tf=TF, tm=TM, name="ffn"):
    m = rows if rows is not None else x.shape[0] + (0 if x_tail is None else x_tail.shape[0])
    n_done = done[0].shape[0] // tm if done else 0
    assert m % tm == 0 and all(d.shape[0] == n_done * tm for d in done)
    grid = (m // tm, D_FF // tf)
    steps = (grid[0] - n_done) * grid[1]
    row = pl.BlockSpec((tm, D_MODEL), lambda i, j: (i, 0))
    vec = pl.BlockSpec((1, D_MODEL), lambda i, j: (0, 0))
    slab = (lambda i, j: jnp.where(i < n_done, 0, j)) if n_done else (lambda i, j: j)
    wcol = pl.BlockSpec((D_MODEL, tf), lambda i, j: (0, slab(i, j)))
    wrow = pl.BlockSpec((tf, D_MODEL), lambda i, j: (slab(i, j), 0))
    in_specs, args = [pl.BlockSpec((tm, D_MODEL), lambda i, j: (jnp.maximum(i, n_done), 0))], [x]
    if x_tail is not None:
        in_specs.append(pl.BlockSpec(x_tail.shape, lambda i, j: (0, 0)))
        args.append(x_tail)
    in_specs += [vec, wcol, wcol, wrow, vec]
    args += [gpre, wg, wu, wd, 0.5 * gpost]
    out_shape = [jax.ShapeDtypeStruct((m, D_MODEL), F32)]
    out_specs = [row]
    if gnext is not None:
        in_specs.append(vec)
        args.append(gnext)
        out_shape.append(jax.ShapeDtypeStruct((m, D_MODEL), BF16))
        out_specs.append(row)
    assert len(done) in (0, len(out_shape))
    cast_in_specs, cast_out_specs, cast_shapes = _cast_specs(
        casts, steps, lambda i, j: jnp.where(i < n_done, 0, (i - n_done) * grid[1] + j))
    in_specs += cast_in_specs
    args += [w for w, _, _ in casts]
    out_specs += cast_out_specs
    out_shape += cast_shapes
    pieces = tm // DONE_ROWS
    assert not done or (pieces * DONE_ROWS == tm and pieces <= grid[1])
    for d in done:
        in_specs.append(pl.BlockSpec(
            (DONE_ROWS, D_MODEL),
            lambda i, j: (jnp.minimum(i, n_done - 1) * pieces + jnp.minimum(j, pieces - 1), 0)))
        args.append(d)
    if emit_w16:
        out_specs += [wcol, wcol, wrow]
        out_shape += [jax.ShapeDtypeStruct(w.shape, BF16) for w in (wg, wu, wd)]
    return pl.pallas_call(
        functools.partial(_ffn_kernel, split=x_tail is not None, n_next=gnext is not None,
                          cast_t=tuple(t for _, _, t in casts), emit_w16=emit_w16, n_done=n_done),
        grid=grid, in_specs=in_specs, out_specs=out_specs, out_shape=out_shape,
        scratch_shapes=[pltpu.VMEM((tm, D_MODEL), BF16)],
        compiler_params=_params(("arbitrary", "arbitrary")), name=name,
    )(*args)


N_GLU_STEPS = D_CONV // TN
N_Z_STEPS = D_SSM // TN
N_XBC_STEPS = D_XBC // TN


def _inproj_kernel(u_ref, w_ref, wb_ref, wdt_ref, dtb_ref, o_ref, dt_ref):
    j = pl.program_id(1)

    @pl.when(j < N_GLU_STEPS)
    def _():
        u = u_ref[...]
        o_ref[...] = _dot(u, w_ref[...]) * jax.nn.sigmoid(_dot(u, wb_ref[...]))

    @pl.when(j >= N_GLU_STEPS)
    def _():
        o_ref[...] = _dot(u_ref[...], w_ref[...])

    @pl.when(j == pl.num_programs(1) - 1)
    def _():
        x = _dot(u_ref[...], wdt_ref[...]) + dtb_ref[...]
        dt_ref[...] = jnp.maximum(x, 0.0) + jnp.log1p(jnp.exp(-jnp.abs(x)))


def _in_proj(u, w_in, w_dt, dt_bias):
    m = u.shape[0]
    g, nz, nx = N_GLU_STEPS, N_Z_STEPS, N_XBC_STEPS
    assert m % TMP == 0 and g == 1

    def out_slab(j):
        return jnp.where(j < g + nz, j + nx, j - g - nz)

    return pl.pallas_call(
        _inproj_kernel, grid=(m // TMP, g + nz + nx),
        in_specs=[pl.BlockSpec((TMP, D_MODEL), lambda i, j: (i, 0)),
                  pl.BlockSpec((D_MODEL, TN), lambda i, j: (0, jnp.where(j < g, j, j + g))),
                  pl.BlockSpec((D_MODEL, TN), lambda i, j: (0, g)),
                  pl.BlockSpec((D_MODEL, HEAD_PAD), lambda i, j: (0, 0)),
                  pl.BlockSpec((1, HEAD_PAD), lambda i, j: (0, 0))],
        out_specs=[pl.BlockSpec((TMP, TN), lambda i, j: (i, out_slab(j))),
                   pl.BlockSpec((TMP, HEAD_PAD), lambda i, j: (i, 0))],
        out_shape=[jax.ShapeDtypeStruct((m, MIX_W), F32), jax.ShapeDtypeStruct((m, HEAD_PAD), F32)],
        compiler_params=_params(("arbitrary", "arbitrary"), VMEM_LIMIT_MAX), name="in_proj",
    )(u, w_in, w_in, w_dt, dt_bias)


def _ln_swish(y, g, b):
    mu = jnp.mean(y, axis=-1, keepdims=True)
    yc = y - mu
    yn = yc * lax.rsqrt(jnp.mean(yc * yc, axis=-1, keepdims=True) + EPS) * g + b
    return _silu(yn)


def _pconv_kernel(v_ref, w_ref, b_ref, lg_ref, lb_ref, *rest, cast_t):
    n_cast = len(cast_t)
    cast_in, (o_ref, tail_ref), rest = rest[:n_cast], rest[n_cast:n_cast + 2], rest[n_cast + 2:]
    cast_out, (xpad_scr, conv_scr) = rest[:n_cast], rest[n_cast:]
    t = pl.program_id(1)
    _run_casts(cast_in, cast_out, cast_t)

    @pl.when(t == 0)
    def _():
        xpad_scr[0:CARRY, :] = jnp.zeros((CARRY, D_CONV), F32)

    xpad_scr[CARRY:CARRY + TL, :] = v_ref[...]
    first = CARRY - (CONV_W - 1)
    hb = CONV_ROWS
    n_rows = CARRY + TL
    for cb in range(D_CONV // LANE):
        lanes = slice(cb * LANE, (cb + 1) * LANE)
        xfull = xpad_scr[:, lanes]
        conv_scr[:, lanes] = jnp.broadcast_to(b_ref[:, lanes], (TL, LANE))
        for phase in range(SUBLANE):
            xs = pltpu.roll(xfull, n_rows - phase, axis=0) if phase else xfull
            taps = [k for k in range(CONV_W) if (first + k) % SUBLANE == phase]
            for base in range(0, TL, hb):
                acc = conv_scr[base:base + hb, lanes]
                for k in taps:
                    off = SUBLANE * ((first + k) // SUBLANE)
                    acc = acc + w_ref[k:k + 1, lanes] * xs[base + off:base + off + hb]
                conv_scr[base:base + hb, lanes] = acc
    xpad_scr[0:CARRY, :] = xpad_scr[TL:TL + CARRY, :]
    for r in range(0, TL, NORM_ROWS):
        rows = slice(r, r + NORM_ROWS)
        o_ref[rows, :] = _ln_swish(conv_scr[rows, :], lg_ref[...], lb_ref[...]).astype(BF16)

    @pl.when(t == pl.num_programs(1) - 1)
    def _():
        tail_ref[0] = xpad_scr[0:CARRY, :]


def _prompt_conv(mix, n_batch, seq, w, b, lg, lb, casts=()):
    vec = pl.BlockSpec((1, D_CONV), lambda bi, t: (0, 0))
    steps = seq // TL
    cast_in_specs, cast_out_specs, cast_shapes = _cast_specs(
        casts, n_batch * steps, lambda bi, t: bi * steps + t)
    return pl.pallas_call(
        functools.partial(_pconv_kernel, cast_t=tuple(t for _, _, t in casts)),
        grid=(n_batch, steps),
        in_specs=[pl.BlockSpec((TL, D_CONV), lambda bi, t: (bi * steps + t, GLU_COL)),
                  pl.BlockSpec((CONV_W, D_CONV), lambda bi, t: (0, 0)), vec, vec, vec] + cast_in_specs,
        out_specs=[pl.BlockSpec((TL, D_CONV), lambda bi, t: (bi * steps + t, 0)),
                   pl.BlockSpec((1, CARRY, D_CONV), lambda bi, t: (bi, 0, 0))] + cast_out_specs,
        out_shape=[jax.ShapeDtypeStruct((n_batch * seq, D_CONV), BF16),
                   jax.ShapeDtypeStruct((n_batch, CARRY, D_CONV), F32)] + cast_shapes,
        scratch_shapes=[pltpu.VMEM((CARRY + TL, D_CONV), F32), pltpu.VMEM((TL, D_CONV), F32)],
        compiler_params=_params(("arbitrary", "arbitrary")), name="prompt_conv",
    )(mix, w, b, lg, lb, *[c for c, _, _ in casts])


def _gated_norm(y, z, g):
    yg = y * _silu(z)
    return yg * lax.rsqrt(jnp.mean(yg * yg, axis=-1, keepdims=True) + EPS) * g


def _sample_state_update(step, dec_ref, h0_ref, bc_ref, hn_ref, yt_scr, xdt_t_scr):
    per_step = h0_ref.shape[0]
    nb = bc_ref.shape[0]
    seq_i = lax.broadcasted_iota(jnp.int32, (nb, D_STATE), 0)
    seqs = [step * per_step + bb for bb in range(per_step)]
    brows = [bc_ref[pl.ds(b, 1), :] for b in seqs]

    def one_hot_rows(g):
        cols = slice(g * D_STATE, (g + 1) * D_STATE)
        return jnp.concatenate([jnp.where(seq_i == b, brow[:, cols], 0.0).astype(BF16)
                                for b, brow in zip(seqs, brows)], axis=1)

    for g in range(N_GROUPS):
        rows = slice(g * GROUP_W, (g + 1) * GROUP_W)
        s_new = _dot(xdt_t_scr[rows, :], one_hot_rows(g))
        h_all = []
        for bb, b in enumerate(seqs):
            parts = []
            for r in range(HEADS_PER_GROUP):
                h = g * HEADS_PER_GROUP + r
                hr = slice(h * HEAD_DIM, (h + 1) * HEAD_DIM)
                parts.append(h0_ref[bb, hr, :] * dec_ref[b * N_HEADS + h]
                             + s_new[r * HEAD_DIM:(r + 1) * HEAD_DIM, bb * D_STATE:(bb + 1) * D_STATE])
            h_new = jnp.concatenate(parts, axis=0)
            hn_ref[bb, rows, :] = h_new
            h_all.append(h_new.astype(BF16))
        yt_scr[rows, :] += lax.dot_general(jnp.concatenate(h_all, axis=1), one_hot_rows(N_GROUPS + g),
                                           (((1,), (1,)), ((), ())), preferred_element_type=F32)


def _pssd_kernel(xbc_ref, z_ref, dt_ref, cw_ref, cb_ref, a_ref, expand_ref, dexp_ref, ng_ref,
                 dec_ref, h0_ref, xdt_ref, bc_ref, skip_ref, zs_ref,
                 y_ref, hfin_ref, tail_ref, hn_ref, ys_ref,
                 state_scr, xpad_scr, xc_scr, yt_scr, xdt_t_scr):
    c = pl.program_id(1)
    q = CHUNK
    step = pl.program_id(0) * pl.num_programs(1) + c

    @pl.when(step == 0)
    def _():
        yt_scr[...] = jnp.zeros_like(yt_scr)
        xdt_t_scr[...] = xdt_ref[...].T.astype(BF16)

    @pl.when(c == 0)
    def _():
        state_scr[...] = jnp.zeros_like(state_scr)
        xpad_scr[0:XCARRY, :] = jnp.zeros((XCARRY, D_XBC), F32)

    xpad_scr[XCARRY:XCARRY + q, :] = xbc_ref[...]
    for cb in range(D_XBC // 512):
        lanes = slice(cb * 512, (cb + 1) * 512)
        acc = jnp.broadcast_to(cb_ref[:, lanes], (q, 512))
        for j in range(SSM_CONV_W):
            k = SSM_CONV_W - 1 - j
            acc = acc + cw_ref[k:k + 1, lanes] * xpad_scr[XCARRY - j:XCARRY - j + q, lanes]
        xc_scr[:, lanes] = _silu(acc)
    xpad_scr[0:XCARRY, :] = xpad_scr[q:q + XCARRY, :]

    row_i = lax.broadcasted_iota(jnp.int32, (q, q), 0)
    col_i = lax.broadcasted_iota(jnp.int32, (q, q), 1)
    tril = row_i >= col_i
    tri = jnp.where(tril, 1.0, 0.0).astype(BF16)
    expand = expand_ref[...]

    dt = dt_ref[...]
    tri3 = jnp.concatenate([tri, tri, tri], axis=1)
    a_cs = _dot3_rhs(tri3, dt * a_ref[...])
    a_cs_t = a_cs.T
    dt_exp = _dot3_lhs(dt, expand)
    acs_exp = _dot3_lhs(a_cs, expand)
    last = acs_exp[q - 1:q, :]
    lane_lo = lax.broadcasted_iota(jnp.int32, (q, LANE), 1) < HEAD_DIM

    for g in range(N_GROUPS):
        ch = slice(g * GROUP_W, (g + 1) * GROUP_W)
        xs = xc_scr[:, ch]
        bg = xc_scr[:, D_SSM + g * D_STATE:D_SSM + (g + 1) * D_STATE]
        cg = xc_scr[:, D_SSM + (N_GROUPS + g) * D_STATE:D_SSM + (N_GROUPS + g + 1) * D_STATE]
        bg16 = bg.astype(BF16)
        cg16 = cg.astype(BF16)
        xdt = xs * dt_exp[:, ch]
        acs_g = acs_exp[:, ch]
        cb = lax.dot_general(cg16, bg16, (((1,), (1,)), ((), ())), preferred_element_type=F32)
        st = state_scr[:, ch]
        y = _dot(cg16, st.astype(BF16)) * jnp.exp(acs_g)
        pieces = []
        for pr in range(HEADS_PER_GROUP // 2):
            xpair = xdt[:, pr * LANE:(pr + 1) * LANE].astype(BF16)
            both = []
            for half in range(2):
                h = g * HEADS_PER_GROUP + 2 * pr + half
                seg = a_cs[:, h:h + 1] - a_cs_t[h:h + 1, :]
                decay = jnp.exp(jnp.where(tril, seg, NEG_BIG))
                both.append(_dot((cb * decay).astype(BF16), xpair))
            pieces.append(jnp.where(lane_lo, both[0], both[1]))
        y = y + jnp.concatenate(pieces, axis=1) + dexp_ref[:, ch] * xs
        xdec = (xdt * jnp.exp(last[:, ch] - acs_g)).astype(BF16)
        s_new = lax.dot_general(bg16, xdec, (((0,), (0,)), ((), ())), preferred_element_type=F32)
        state_scr[:, ch] = st * jnp.exp(last[:, ch]) + s_new
        y_ref[:, ch] = _gated_norm(y, z_ref[:, ch], ng_ref[:, ch]).astype(BF16)

    _sample_state_update(step, dec_ref, h0_ref, bc_ref, hn_ref, yt_scr, xdt_t_scr)

    @pl.when(c == pl.num_programs(1) - 1)
    def _():
        hfin_ref[0] = state_scr[...].T
        tail_ref[0] = xpad_scr[0:XCARRY, :]

    @pl.when(step == pl.num_programs(0) * pl.num_programs(1) - 1)
    def _():
        y = yt_scr[...].T + skip_ref[...]
        for g in range(N_GROUPS):
            ch = slice(g * GROUP_W, (g + 1) * GROUP_W)
            ys_ref[:, ch] = _gated_norm(y[:, ch], zs_ref[:, ch], ng_ref[:, ch]).astype(BF16)


def _ssd(mix, dt, n_batch, seq, cw, cb, a_pad, d_exp, ng, dec, h0, xdt, bc, skip):
    nc = seq // CHUNK
    n_prompt = n_batch * seq
    n_samp = h0.shape[0]
    per_step = n_samp // (n_batch * nc)
    assert per_step * n_batch * nc == n_samp and n_prompt % n_samp == 0

    def rows(w, col=0):
        return pl.BlockSpec((CHUNK, w), lambda bi, c: (bi * nc + c, col))

    def vec(*s):
        return pl.BlockSpec(s, lambda bi, c: (0,) * len(s))

    states = pl.BlockSpec((per_step, D_SSM, D_STATE), lambda bi, c: (bi * nc + c, 0, 0))
    return pl.pallas_call(
        _pssd_kernel, grid=(n_batch, nc),
        in_specs=[rows(D_XBC), rows(D_SSM, Z_COL), rows(HEAD_PAD), vec(SSM_CONV_W, D_XBC),
                  vec(1, D_XBC), vec(1, HEAD_PAD), vec(3 * HEAD_PAD, D_SSM), vec(1, D_SSM),
                  vec(1, D_SSM), pl.BlockSpec(memory_space=pltpu.SMEM), states, vec(n_samp, D_SSM),
                  vec(n_samp, D_XBC - D_SSM), vec(n_samp, D_SSM),
                  pl.BlockSpec((n_samp, D_SSM), lambda bi, c: (n_prompt // n_samp, Z_COL))],
        out_specs=[rows(D_SSM), pl.BlockSpec((1, D_SSM, D_STATE), lambda bi, c: (bi, 0, 0)),
                   pl.BlockSpec((1, XCARRY, D_XBC), lambda bi, c: (bi, 0, 0)),
                   states, vec(n_samp, D_SSM)],
        out_shape=[jax.ShapeDtypeStruct((n_prompt, D_SSM), BF16),
                   jax.ShapeDtypeStruct((n_batch, D_SSM, D_STATE), F32),
                   jax.ShapeDtypeStruct((n_batch, XCARRY, D_XBC), F32),
                   jax.ShapeDtypeStruct((n_samp, D_SSM, D_STATE), F32),
                   jax.ShapeDtypeStruct((n_samp, D_SSM), BF16)],
        scratch_shapes=[pltpu.VMEM((D_STATE, D_SSM), F32), pltpu.VMEM((XCARRY + CHUNK, D_XBC), F32),
                        pltpu.VMEM((CHUNK, D_XBC), F32),
                        pltpu.VMEM((D_SSM, n_samp), F32), pltpu.VMEM((D_SSM, n_samp), BF16)],
        compiler_params=_params(("arbitrary", "arbitrary")), name="ssd",
    )(mix, mix, dt, cw, cb, a_pad, _head_expand_matrix(), d_exp, ng, dec, h0, xdt, bc, skip, mix)


def _sprep_kernel(v_ref, cst_ref, w_ref, b_ref, lg_ref, lb_ref,
                  xbc_ref, xst_ref, cw_ref, cb_ref, dt_ref, a_ref, dexp_ref,
                  co_ref, ncst_ref, nxst_ref, xdt_ref, bc_ref, dec_ref, skip_ref):
    nb = v_ref.shape[0]
    v = v_ref[...]
    acc = jnp.broadcast_to(b_ref[...], (nb, D_CONV)) + w_ref[CONV_W - 1:CONV_W, :] * v
    for k in range(CONV_W - 1):
        acc = acc + w_ref[k:k + 1, :] * cst_ref[k]
    co_ref[...] = _ln_swish(acc, lg_ref[...], lb_ref[...]).astype(BF16)
    for k in range(CONV_W - 2):
        ncst_ref[k] = cst_ref[k + 1]
    ncst_ref[CONV_W - 2] = v

    xn = xbc_ref[...]
    acc = jnp.broadcast_to(cb_ref[...], (nb, D_XBC)) + cw_ref[SSM_CONV_W - 1:SSM_CONV_W, :] * xn
    for k in range(SSM_CONV_W - 1):
        acc = acc + cw_ref[k:k + 1, :] * xst_ref[k]
    xc = _silu(acc)
    for k in range(SSM_CONV_W - 2):
        nxst_ref[k] = xst_ref[k + 1]
    nxst_ref[SSM_CONV_W - 2] = xn

    xs = xc[:, :D_SSM]
    bc_ref[...] = xc[:, D_SSM:]
    dt = dt_ref[...]
    dec_ref[...] = jnp.exp(dt * a_ref[...])
    dt_exp = _dot3_lhs(dt, _head_expand_matrix())
    xdt_ref[...] = xs * dt_exp
    skip_ref[...] = dexp_ref[...] * xs


def _sample_prep(mix, dt, n_prompt, n_samp, cst, xst, mw, mb, lg, lb, cw, cb, a_pad, d_exp):
    sb = SAMPLE_BLOCK
    off = n_prompt // sb
    srow = lambda w, col=0: pl.BlockSpec((sb, w), lambda i: (off + i, col))
    orow = lambda w: pl.BlockSpec((sb, w), lambda i: (i, 0))
    taps = lambda k, w: pl.BlockSpec((k, sb, w), lambda i: (0, i, 0))
    const = lambda *s: pl.BlockSpec(s, lambda i: (0,) * len(s))
    cst_t = jnp.swapaxes(cst, 0, 1)
    xst_t = jnp.swapaxes(xst, 0, 1)
    c_samp, ncst_t, nxst_t, xdt, bc, dec, skip = pl.pallas_call(
        _sprep_kernel, grid=(n_samp // sb,),
        in_specs=[srow(D_CONV, GLU_COL), taps(CONV_W - 1, D_CONV), const(CONV_W, D_CONV),
                  const(1, D_CONV), const(1, D_CONV), const(1, D_CONV),
                  srow(D_XBC), taps(SSM_CONV_W - 1, D_XBC), const(SSM_CONV_W, D_XBC),
                  const(1, D_XBC), srow(HEAD_PAD), const(1, HEAD_PAD), const(1, D_SSM)],
        out_specs=[orow(D_CONV), taps(CONV_W - 1, D_CONV), taps(SSM_CONV_W - 1, D_XBC),
                   orow(D_SSM), orow(D_XBC - D_SSM), orow(HEAD_PAD), orow(D_SSM)],
        out_shape=[jax.ShapeDtypeStruct((n_samp, D_CONV), BF16),
                   jax.ShapeDtypeStruct((CONV_W - 1, n_samp, D_CONV), F32),
                   jax.ShapeDtypeStruct((SSM_CONV_W - 1, n_samp, D_XBC), F32),
                   jax.ShapeDtypeStruct((n_samp, D_SSM), F32),
                   jax.ShapeDtypeStruct((n_samp, D_XBC - D_SSM), F32),
                   jax.ShapeDtypeStruct((n_samp, HEAD_PAD), F32),
                   jax.ShapeDtypeStruct((n_samp, D_SSM), F32)],
        compiler_params=_params(("arbitrary",)), name="sample_prep",
    )(mix, cst_t, mw, mb, lg, lb, mix, xst_t, cw, cb, dt, a_pad, d_exp)
    dec_flat = dec[:, :N_HEADS].reshape(-1)
    return (c_samp, jnp.swapaxes(ncst_t, 0, 1), jnp.swapaxes(nxst_t, 0, 1), xdt, bc, dec_flat, skip)


def _outproj_kernel(c_ref, cs_ref, y_ref, ys_ref, w_ref, h_ref, g_ref, o_ref):
    i = pl.program_id(0)
    last = pl.num_programs(0) - 1
    na = TM - cs_ref.shape[0]

    def run(c, y):
        mix = _dot(c, w_ref[0:D_CONV, :]) + _dot(y, w_ref[D_CONV:, :])
        o_ref[...] = h_ref[...] + _rms(mix, g_ref[...])

    @pl.when(i != last)
    def _():
        run(c_ref[...], y_ref[...])

    @pl.when(i == last)
    def _():
        run(jnp.concatenate([c_ref[0:na, :], cs_ref[...]], axis=0),
            jnp.concatenate([y_ref[0:na, :], ys_ref[...]], axis=0))


def _out_proj(c_prompt, c_samp, y_prompt, y_samp, w_out, h, g):
    m = h.shape[0]
    n_samp = c_samp.shape[0]
    row = pl.BlockSpec((TM, D_MODEL), lambda i: (i, 0))
    return pl.pallas_call(
        _outproj_kernel, grid=(m // TM,),
        in_specs=[pl.BlockSpec((TM, D_CONV), lambda i: (i, 0)),
                  pl.BlockSpec((n_samp, D_CONV), lambda i: (0, 0)),
                  pl.BlockSpec((TM, D_SSM), lambda i: (i, 0)),
                  pl.BlockSpec((n_samp, D_SSM), lambda i: (0, 0)),
                  pl.BlockSpec((D_CONV + D_SSM, D_MODEL), lambda i: (0, 0)),
                  row, pl.BlockSpec((1, D_MODEL), lambda i: (0, 0))],
        out_specs=row, out_shape=jax.ShapeDtypeStruct((m, D_MODEL), F32),
        compiler_params=_params(("arbitrary",)), name="out_proj",
    )(c_prompt, c_samp, y_prompt, y_samp, w_out, h, g)


def _ple_kernel(h_ref, pa_ref, pb_ref, gpre_ref, wg_ref, wp_ref, gpost_ref, oa_ref, ob_ref, emb_scr):
    i = pl.program_id(0)

    def embed(rows, src, srows):
        emb_scr[rows, :] = _dot(src[srows, :].astype(BF16), wp_ref[...])
    _on_tile_rows(i, pa_ref, pb_ref, embed)

    h = h_ref[...]
    gate = jax.nn.sigmoid(_dot(_rms(h, gpre_ref[...]).astype(BF16), wg_ref[...]))
    oa_ref[...] = h + _rms(gate * emb_scr[...], gpost_ref[...])

    @pl.when(i == pl.num_programs(0) - 1)
    def _():
        nb = ob_ref.shape[0]
        ob_ref[...] = oa_ref[TM - nb:TM, :]


def _ple(h, p_prompt, p_samp, gpre, wg, wp, gpost):
    m = h.shape[0]
    n_prompt, n_samp = p_prompt.shape[0], p_samp.shape[0]
    row = pl.BlockSpec((TM, D_MODEL), lambda i: (i, 0))
    vec = pl.BlockSpec((1, D_MODEL), lambda i: (0, 0))
    return pl.pallas_call(
        _ple_kernel, grid=(m // TM,),
        in_specs=[row, pl.BlockSpec((TM, PLE_DIM), lambda i: (i, 0)),
                  pl.BlockSpec((n_samp, PLE_DIM), lambda i: (0, 0)), vec,
                  pl.BlockSpec((D_MODEL, D_MODEL), lambda i: (0, 0)),
                  pl.BlockSpec((PLE_DIM, D_MODEL), lambda i: (0, 0)), vec],
        out_specs=[row, pl.BlockSpec((n_samp, D_MODEL), lambda i: (0, 0))],
        out_shape=[jax.ShapeDtypeStruct((n_prompt, D_MODEL), F32),
                   jax.ShapeDtypeStruct((n_samp, D_MODEL), F32)],
        scratch_shapes=[pltpu.VMEM((TM, D_MODEL), F32)],
        compiler_params=_params(("arbitrary",)), name="ple")(h, p_prompt, p_samp, gpre, wg, wp, gpost)


def _layer(x_prompt, x_samp, p_prompt, p_samp, n_batch, seq, cst, xst, h0, lw):
    (norm_ffn1_pre, w_ffn1_gate, w_ffn1_up, w_ffn1_down, norm_ffn1_post,
     norm_mix_pre, w_in, conv_mod_w, conv_mod_b, conv_mod_ln_g, conv_mod_ln_b,
     ssm_conv_w, ssm_conv_b, dt_bias, a_log, d_skip, ssm_norm_g, w_out, norm_mix_post,
     norm_ffn2_pre, w_ffn2_gate, w_ffn2_up, w_ffn2_down, norm_ffn2_post,
     norm_ple_pre, w_ple_gate, w_ple_proj, norm_ple_post) = lw
    n_prompt, n_samp = x_prompt.shape[0], x_samp.shape[0]
    m = n_prompt + n_samp
    _check_split(n_prompt, n_samp)
    row2 = lambda t: t.reshape(1, -1)

    casts_ffn1 = ((jnp.swapaxes(w_in, 0, 1), LANE, True),)
    casts_conv = ((w_out, 128, False), (w_ffn2_gate, 64, False), (w_ffn2_up, 64, False),
                  (w_ffn2_down, 176, False), (w_ple_gate, 64, False), (w_ple_proj, 16, False))
    ffn1 = (row2(norm_ffn1_pre), row2(norm_ffn1_post), row2(norm_mix_pre))
    h1_head, u_head, wg1, wu1, wd1 = _ffn(
        x_prompt, ffn1[0], w_ffn1_gate, w_ffn1_up, w_ffn1_down, ffn1[1], gnext=ffn1[2],
        rows=FIRST_TILES * TM, emit_w16=True, tf=TF_FIRST, tm=FIRST_TILES * TM, name="ffn_first")
    h1, u, w_in16 = _ffn(
        x_prompt, ffn1[0], wg1, wu1, wd1, ffn1[1], gnext=ffn1[2], x_tail=x_samp, casts=casts_ffn1,
        done=(h1_head, u_head), name="ffn_rest")

    d_proj = w_in.shape[1]
    w_dt = jnp.pad(w_in16[:, d_proj - N_HEADS:], ((0, 0), (0, HEAD_PAD - N_HEADS)))
    pad_h = lambda t: jnp.pad(t.astype(F32), (0, HEAD_PAD - N_HEADS)).reshape(1, HEAD_PAD)
    mix, dt = _in_proj(u, w_in16, w_dt, pad_h(dt_bias))

    a = -jnp.exp(a_log.astype(F32))
    a_pad = pad_h(a)
    d_exp = row2(jnp.repeat(d_skip.astype(F32), HEAD_DIM))
    cw, cb = ssm_conv_w, row2(ssm_conv_b)
    mw, mb, lg, lb = conv_mod_w, row2(conv_mod_b), row2(conv_mod_ln_g), row2(conv_mod_ln_b)
    ng = row2(ssm_norm_g)

    c_prompt, glu_tail, w_out16, wg2, wu2, wd2, wpg, wpp = _prompt_conv(
        mix, n_batch, seq, mw, mb, lg, lb, casts=casts_conv)
    c_samp, new_cst_samp, new_xst_samp, xdt, bc, dec, skip = _sample_prep(
        mix, dt, n_prompt, n_samp, cst, xst, mw, mb, lg, lb, cw, cb, a_pad, d_exp)
    y_mix_prompt, hfin_prompt, xbc_tail, hn, y_mix_samp = _ssd(
        mix, dt, n_batch, seq, cw, cb, a_pad, d_exp, ng,
        dec, h0.reshape(n_samp, D_SSM, D_STATE), xdt, bc, skip)

    h2 = _out_proj(c_prompt, c_samp, y_mix_prompt, y_mix_samp, w_out16, h1, row2(norm_mix_post))
    (h3,) = _ffn(h2, row2(norm_ffn2_pre), wg2, wu2, wd2, row2(norm_ffn2_post))
    y_prompt, y_samp = _ple(h3, p_prompt, p_samp, row2(norm_ple_pre), wpg, wpp, row2(norm_ple_post))

    new_cst_prompt = glu_tail[:, CARRY - (CONV_W - 1):]
    new_xst_prompt = xbc_tail[:, XCARRY - (SSM_CONV_W - 1):]
    new_h_prompt = hfin_prompt.reshape(n_batch, N_HEADS, HEAD_DIM, D_STATE)
    new_h_samp = hn.reshape(n_samp, N_HEADS, HEAD_DIM, D_STATE)
    return (y_prompt, y_samp, new_cst_prompt, new_xst_prompt, new_h_prompt,
            new_cst_samp, new_xst_samp, new_h_samp)


def kernel(x_prompt, x_sample, state_conv_mod, state_ssm_conv, state_ssm, p_prompt, p_sample,
           norm_ffn1_pre, w_ffn1_gate, w_ffn1_up, w_ffn1_down, norm_ffn1_post,
           norm_mix_pre, w_in, conv_mod_w, conv_mod_b, conv_mod_ln_g, conv_mod_ln_b,
           ssm_conv_w, ssm_conv_b, dt_bias, a_log, d_skip, ssm_norm_g, w_out, norm_mix_post,
           norm_ffn2_pre, w_ffn2_gate, w_ffn2_up, w_ffn2_down, norm_ffn2_post,
           norm_ple_pre, w_ple_gate, w_ple_proj, norm_ple_post):
    weights = (norm_ffn1_pre, w_ffn1_gate, w_ffn1_up, w_ffn1_down, norm_ffn1_post,
               norm_mix_pre, w_in, conv_mod_w, conv_mod_b, conv_mod_ln_g, conv_mod_ln_b,
               ssm_conv_w, ssm_conv_b, dt_bias, a_log, d_skip, ssm_norm_g, w_out, norm_mix_post,
               norm_ffn2_pre, w_ffn2_gate, w_ffn2_up, w_ffn2_down, norm_ffn2_post,
               norm_ple_pre, w_ple_gate, w_ple_proj, norm_ple_post)
    n_batch, seq, _ = x_prompt.shape
    n_samp = x_sample.shape[0]
    n_prompt = n_batch * seq
    depth = norm_ffn1_pre.shape[0]
    xp = x_prompt.reshape(n_prompt, D_MODEL)
    xs = x_sample.reshape(n_samp, D_MODEL)
    outs = [[] for _ in range(6)]
    for i in range(depth):
        res = _layer(xp, xs, p_prompt[i].reshape(n_prompt, PLE_DIM), p_sample[i].reshape(n_samp, PLE_DIM),
                     n_batch, seq, state_conv_mod[i], state_ssm_conv[i], state_ssm[i],
                     tuple(w[i] for w in weights))
        xp, xs = res[0], res[1]
        for lst, r in zip(outs, res[2:]):
            lst.append(r)
    return ((xp.reshape(n_batch, seq, D_MODEL), xs.reshape(n_samp, 1, D_MODEL))
            + tuple(jnp.stack(lst, axis=0) for lst in outs))
```
